```python
import jax
import jax.numpy as jnp
from jax import lax

D_MODEL = 1024
BATCH = 2
SEQ = 8192
DEPTH = 1

CHUNK = 64
QBLOCK = 128
EPS = 1e-6

GLA_HEADS = 4
GLA_DK = 64
GLA_DV = 128
GLA_GATE_RANK = 16
GLA_GATE_NORMALIZER = 16.0
GLA_LOG_GATE_MIN = -1.0

MLA_HEADS = 8
MLA_Q_RANK = 256
MLA_KV_RANK = 128
MLA_NOPE = 64
MLA_ROPE = 32
MLA_V = 64
ROPE_BASE = 10000.0

GLA_QK_W = GLA_HEADS * GLA_DK
GLA_V_W = GLA_HEADS * GLA_DV
D_PROJ = 2 * GLA_QK_W + 2 * GLA_V_W + GLA_GATE_RANK + MLA_Q_RANK + MLA_KV_RANK + MLA_ROPE
D_MIX = GLA_V_W + MLA_HEADS * MLA_V

D_FF = 4 * D_MODEL

kernel_name = 'hymba_gla_mla_sqrelu_chunk_causal'


def _rmsnorm(x, g):
    xf = x.astype(jnp.float32)
    y = xf * lax.rsqrt(jnp.mean(xf * xf, axis=-1, keepdims=True) + EPS)
    return (y * g.astype(jnp.float32)).astype(x.dtype)


def _proj_split_points():
    widths = (GLA_QK_W, GLA_QK_W, GLA_V_W, GLA_GATE_RANK, GLA_V_W, MLA_Q_RANK, MLA_KV_RANK)
    pts, acc = [], 0
    for w in widths:
        acc += w
        pts.append(acc)
    return pts


def _gla_chunked(q, k, v, log_a):
    b_, s_, h_, dk = q.shape
    dv = v.shape[-1]
    n = s_ // CHUNK

    def to_chunks(t):
        return t.astype(jnp.float32).reshape(b_, n, CHUNK, h_, t.shape[-1]).transpose(0, 3, 1, 2, 4)

    q, k, v, log_a = to_chunks(q), to_chunks(k), to_chunks(v), to_chunks(log_a)
    cum = jnp.cumsum(log_a, axis=3)
    q_dec = q * (dk ** -0.5) * jnp.exp(cum)
    k_inv = k * jnp.exp(-cum)
    causal = jnp.tril(jnp.ones((CHUNK, CHUNK), dtype=bool))
    scores = jnp.where(causal, jnp.einsum('bhncd,bhnjd->bhncj', q_dec, k_inv), 0.0)
    o_intra = jnp.einsum('bhncj,bhnjv->bhncv', scores, v)

    cum_last = cum[:, :, :, -1, :]
    k_to_end = k * jnp.exp(cum_last[:, :, :, None, :] - cum)
    chunk_kv = jnp.einsum('bhncd,bhncv->bhndv', k_to_end, v)

    def step(state, inp):
        decay, kv = inp
        return jnp.exp(decay)[..., None] * state + kv, state

    init = jnp.zeros((b_, h_, dk, dv), jnp.float32)
    _, prev = lax.scan(step, init, (jnp.moveaxis(cum_last, 2, 0), jnp.moveaxis(chunk_kv, 2, 0)))
    prev = jnp.moveaxis(prev, 0, 2)
    o = o_intra + jnp.einsum('bhncd,bhndv->bhncv', q_dec, prev)
    return o.transpose(0, 2, 3, 1, 4).reshape(b_, s_, h_, dv)


def _rope_tables(positions):
    inv_freq = ROPE_BASE ** (-jnp.arange(0, MLA_ROPE, 2, dtype=jnp.float32) / MLA_ROPE)
    ang = positions.astype(jnp.float32)[..., None] * inv_freq
    return jnp.cos(ang)[:, :, None, :], jnp.sin(ang)[:, :, None, :]


def _apply_rope(x, cos, sin):
    half = MLA_ROPE // 2
    xf = x.astype(jnp.float32)
    x1, x2 = xf[..., :half], xf[..., half:]
    return jnp.concatenate([x1 * cos - x2 * sin, x1 * sin + x2 * cos], axis=-1).astype(x.dtype)


def _chunk_causal_attention(q, k, v):
    b_, s_, h_, dqk = q.shape
    dv = v.shape[-1]
    nb = s_ // QBLOCK
    scale = dqk ** -0.5
    qb = q.reshape(b_, nb, QBLOCK, h_, dqk).transpose(1, 0, 3, 2, 4)
    kh = k.transpose(0, 2, 1, 3)
    vh = v.transpose(0, 2, 1, 3)
    key_chunk = jnp.arange(s_) // CHUNK

    def one_block(args):
        q_blk, blk = args
        s = jnp.einsum('bhqd,bhkd->bhqk', q_blk, kh, preferred_element_type=jnp.float32) * scale
        q_chunk = (blk * QBLOCK + jnp.arange(QBLOCK)) // CHUNK
        mask = key_chunk[None, :] <= q_chunk[:, None]
        p = jax.nn.softmax(jnp.where(mask, s, -1e30), axis=-1)
        return jnp.einsum('bhqk,bhkv->bhqv', p.astype(vh.dtype), vh)

    out = lax.map(one_block, (qb, jnp.arange(nb)))
    return out.transpose(1, 0, 3, 2, 4).reshape(b_, s_, h_ * dv)


def setup_inputs(seed: int = 0) -> dict:
    key = jax.random.key(seed)
    ks = jax.random.split(key, 20)
    f32 = jnp.float32

    def w(k, shape, fan_in):
        return jax.random.normal(k, shape, f32) * fan_in ** -0.5

    def gain(k, shape):
        return 1.0 + 0.05 * jax.random.normal(k, shape, f32)

    x = jax.random.normal(ks[0], (BATCH, SEQ, D_MODEL), f32)
    offset = jax.random.randint(ks[1], (BATCH, 1), 0, 64, dtype=jnp.int32) * CHUNK
    positions = (offset + jnp.arange(SEQ, dtype=jnp.int32)[None, :]).astype(jnp.int32)
    return {
        'x': x,
        'positions': positions,
        'attn_norm': gain(ks[2], (DEPTH, D_MODEL)),
        'w_in': w(ks[3], (DEPTH, D_MODEL, D_PROJ), D_MODEL),
        'w_gate_up': w(ks[4], (DEPTH, GLA_GATE_RANK, GLA_QK_W), GLA_GATE_RANK),
        'b_gate': 0.1 * jax.random.normal(ks[5], (DEPTH, GLA_QK_W), f32),
        'gla_out_norm': gain(ks[6], (DEPTH, GLA_DV)),
        'q_a_norm': gain(ks[7], (DEPTH, MLA_Q_RANK)),
        'w_uq': w(ks[8], (DEPTH, MLA_Q_RANK, MLA_HEADS * (MLA_NOPE + MLA_ROPE)), MLA_Q_RANK),
        'kv_a_norm': gain(ks[9], (DEPTH, MLA_KV_RANK)),
        'w_ukv': w(ks[10], (DEPTH, MLA_KV_RANK, MLA_HEADS * (MLA_NOPE + MLA_V)), MLA_KV_RANK),
        'q_head_norm': gain(ks[11], (DEPTH, MLA_NOPE + MLA_ROPE)),
        'k_head_norm': gain(ks[12], (DEPTH, MLA_NOPE + MLA_ROPE)),
        'w_out': w(ks[13], (DEPTH, D_MIX, D_MODEL), D_MIX),
        'mlp_norm': gain(ks[14], (DEPTH, D_MODEL)),
        'w_up': w(ks[15], (DEPTH, D_MODEL, D_FF), D_MODEL),
        'w_down': w(ks[16], (DEPTH, D_FF, D_MODEL), D_FF),
    }


def reference(x, positions, attn_norm, w_in, w_gate_up, b_gate, gla_out_norm, q_a_norm, w_uq,
              kv_a_norm, w_ukv, q_head_norm, k_head_norm, w_out, mlp_norm, w_up, w_down):
    b_, s_, _ = x.shape
    cos, sin = _rope_tables(positions)
    split_pts = _proj_split_points()
    for l in range(DEPTH):
        h = _rmsnorm(x, attn_norm[l])
        z = h @ w_in[l]
        zq, zk, zv, zgate, zg, zcq, zckv, zkpe = jnp.split(z, split_pts, axis=-1)

        gq = zq.reshape(b_, s_, GLA_HEADS, GLA_DK)
        gk = zk.reshape(b_, s_, GLA_HEADS, GLA_DK)
        gv = zv.reshape(b_, s_, GLA_HEADS, GLA_DV)
        gate_logit = (zgate @ w_gate_up[l] + b_gate[l]).astype(jnp.float32)
        log_a = jnp.maximum(jax.nn.log_sigmoid(gate_logit) / GLA_GATE_NORMALIZER, GLA_LOG_GATE_MIN)
        log_a = log_a.reshape(b_, s_, GLA_HEADS, GLA_DK)
        o_gla = _gla_chunked(gq, gk, gv, log_a).astype(x.dtype)
        o_gla = _rmsnorm(o_gla, gla_out_norm[l]) * jax.nn.silu(zg.reshape(b_, s_, GLA_HEADS, GLA_DV))
        o_gla = o_gla.reshape(b_, s_, GLA_V_W)

        q = (_rmsnorm(zcq, q_a_norm[l]) @ w_uq[l]).reshape(b_, s_, MLA_HEADS, MLA_NOPE + MLA_ROPE)
        kv = (_rmsnorm(zckv, kv_a_norm[l]) @ w_ukv[l]).reshape(b_, s_, MLA_HEADS, MLA_NOPE + MLA_V)
        k_nope, v = kv[..., :MLA_NOPE], kv[..., MLA_NOPE:]
        k_pe = jnp.broadcast_to(zkpe[:, :, None, :], (b_, s_, MLA_HEADS, MLA_ROPE))
        k = jnp.concatenate([k_nope, k_pe], axis=-1)
        q = _rmsnorm(q, q_head_norm[l])
        k = _rmsnorm(k, k_head_norm[l])
        q = jnp.concatenate([q[..., :MLA_NOPE], _apply_rope(q[..., MLA_NOPE:], cos, sin)], axis=-1)
        k = jnp.concatenate([k[..., :MLA_NOPE], _apply_rope(k[..., MLA_NOPE:], cos, sin)], axis=-1)
        o_mla = _chunk_causal_attention(q, k, v)

        x = x + jnp.concatenate([o_gla, o_mla], axis=-1) @ w_out[l]

        h = _rmsnorm(x, mlp_norm[l])
        x = x + jnp.square(jax.nn.relu(h @ w_up[l])) @ w_down[l]
    return x
```

```python
import functools

import jax
import jax.numpy as jnp
from jax import lax
from jax.experimental import pallas as pl
from jax.experimental.pallas import tpu as pltpu

F32 = jnp.float32
BF16 = jnp.bfloat16
HIGHEST = lax.Precision.HIGHEST

CHUNK = 64
EPS = 1e-6

GLA_HEADS = 4
GLA_DK = 64
GLA_DV = 128
GLA_GATE_RANK = 16
GLA_GATE_NORMALIZER = 16.0
GLA_LOG_GATE_MIN = -1.0
GLA_QK_W = GLA_HEADS * GLA_DK
GLA_V_W = GLA_HEADS * GLA_DV

MLA_HEADS = 8
MLA_Q_RANK = 256
MLA_KV_RANK = 128
MLA_NOPE = 64
MLA_ROPE = 32
MLA_HALF = MLA_ROPE // 2
MLA_QK = MLA_NOPE + MLA_ROPE
MLA_V = 64
ROPE_BASE = 10000.0

LANES = 128
HEAD_PAD = LANES
NEG = -1e30

C_Q = 0
C_K = C_Q + GLA_QK_W
C_V = C_K + GLA_QK_W
C_G = C_V + GLA_V_W
C_CQ = C_G + GLA_V_W
C_CKV = C_CQ + MLA_Q_RANK
C_MISC = C_CKV + MLA_KV_RANK
D_PROJ_PAD = C_MISC + LANES
MISC_GATE = 0
MISC_PE = MLA_NOPE

TOKEN_BLOCK = 512
GLA_BLOCK = 512
ATTN_TQ = 512
ATTN_TK = 256
FF_BLOCK = 1024
VMEM_LIMIT = 56 * 1024 * 1024


def _nt(a, b):
    return lax.dot_general(a, b, (((1,), (1,)), ((), ())), preferred_element_type=F32)


def _tn(a, b):
    return lax.dot_general(a, b, (((0,), (0,)), ((), ())), preferred_element_type=F32)


def _rms(v):
    return v * lax.rsqrt(jnp.mean(v * v, axis=-1, keepdims=True) + EPS)


def _proj_kernel(x_ref, posc_ref, posr_ref, g_attn_ref, w_in_ref, w_gate_ref, b_gate_ref,
                 qa_g_ref, w_uqt_ref, kva_g_ref, w_k_ref, w_vt_ref,
                 qg_col_ref, kg_nope_ref, kg_rope_ref, invf_col_ref, invf_row_ref,
                 zqk_ref, loga_ref, zv_ref, gate_ref, qt_ref, k_ref, vt_ref):
    x = x_ref[0]
    h = _rms(x) * g_attn_ref[...]
    z = jnp.dot(h.astype(BF16), w_in_ref[...], preferred_element_type=F32)

    zqk_ref[0] = z[:, C_Q:C_V]
    zv_ref[0] = z[:, C_V:C_G].astype(BF16)
    zg = z[:, C_G:C_CQ]
    gate_ref[0] = (zg * jax.nn.sigmoid(zg)).astype(BF16)

    misc = z[:, C_MISC:C_MISC + LANES]
    logit = jnp.dot(misc, w_gate_ref[...], preferred_element_type=F32, precision=HIGHEST)
    logit = logit + b_gate_ref[...]
    log_sig = jnp.minimum(logit, 0.0) - jnp.log1p(jnp.exp(-jnp.abs(logit)))
    loga_ref[0] = jnp.maximum(log_sig / GLA_GATE_NORMALIZER, GLA_LOG_GATE_MIN)

    cq = (_rms(z[:, C_CQ:C_CKV]) * qa_g_ref[...]).astype(BF16)
    qt = _nt(w_uqt_ref[...], cq)
    ang_t = invf_col_ref[...] * posr_ref[0].astype(F32)
    cos_t = jnp.cos(ang_t)
    sin_t = jnp.sin(ang_t)
    qscale = MLA_QK ** -0.5
    r1 = MLA_NOPE
    r2 = MLA_NOPE + MLA_HALF
    r3 = MLA_QK
    for hh in range(MLA_HEADS):
        base = hh * HEAD_PAD
        blk = qt[base:base + HEAD_PAD, :]
        ssq = jnp.sum(blk * blk, axis=0, keepdims=True)
        rs = lax.rsqrt(ssq * (1.0 / MLA_QK) + EPS) * qscale
        qn = blk * rs * qg_col_ref[...]
        x1 = qn[r1:r2, :]
        x2 = qn[r2:r3, :]
        qt_ref[0, base:base + r1, :] = qn[:r1, :].astype(BF16)
        qt_ref[0, base + r1:base + r2, :] = (x1 * cos_t - x2 * sin_t).astype(BF16)
        qt_ref[0, base + r2:base + r3, :] = (x1 * sin_t + x2 * cos_t).astype(BF16)
        qt_ref[0, base + r3:base + HEAD_PAD, :] = jnp.zeros((HEAD_PAD - r3, blk.shape[1]), BF16)

    ckv = (_rms(z[:, C_CKV:C_MISC]) * kva_g_ref[...]).astype(BF16)
    vt_ref[0] = _nt(w_vt_ref[...], ckv).astype(BF16)
    kn = jnp.dot(ckv, w_k_ref[...], preferred_element_type=F32)

    lane = lax.broadcasted_iota(jnp.int32, (1, LANES), 1)
    is_lo = (lane >= MISC_PE) & (lane < MISC_PE + MLA_HALF)
    is_hi = (lane >= MISC_PE + MLA_HALF) & (lane < MISC_PE + MLA_ROPE)
    kpe = jnp.where(is_lo | is_hi, misc, 0.0)
    ssq_pe = jnp.sum(kpe * kpe, axis=-1, keepdims=True)
    xg = kpe * kg_rope_ref[...]
    ang = posc_ref[0].astype(F32) * invf_row_ref[...]
    cos_k = jnp.cos(ang)
    sin_k = jnp.sin(ang)
    rot = (xg * cos_k
           + pltpu.roll(xg, LANES - MLA_HALF, 1) * jnp.where(is_lo, -sin_k, 0.0)
           + pltpu.roll(xg, MLA_HALF, 1) * jnp.where(is_hi, sin_k, 0.0))
    for hh in range(MLA_HEADS):
        base = hh * HEAD_PAD
        kn_h = kn[:, base:base + HEAD_PAD]
        ssq = jnp.sum(kn_h * kn_h, axis=-1, keepdims=True) + ssq_pe
        rs = lax.rsqrt(ssq * (1.0 / MLA_QK) + EPS)
        k_ref[0, :, base:base + HEAD_PAD] = ((kn_h * kg_nope_ref[...] + rot) * rs).astype(BF16)


def _gla_kernel(zqk_ref, loga_ref, zv_ref, gate_ref, gout_ref, o_ref, st_ref, *, n_chunks):
    @pl.when(pl.program_id(1) == 0)
    def _():
        st_ref[...] = jnp.zeros(st_ref.shape, F32)

    r = lax.broadcasted_iota(jnp.int32, (CHUNK, CHUNK), 0)
    c = lax.broadcasted_iota(jnp.int32, (CHUNK, CHUNK), 1)
    tri = (r >= c).astype(F32)
    sr = lax.broadcasted_iota(jnp.int32, (GLA_HEADS * CHUNK, CHUNK), 0)
    sc_ = lax.broadcasted_iota(jnp.int32, (GLA_HEADS * CHUNK, CHUNK), 1)
    causal4 = (sr % CHUNK) >= sc_
    qr = lax.broadcasted_iota(jnp.int32, (GLA_HEADS * CHUNK, GLA_QK_W), 0)
    qc = lax.broadcasted_iota(jnp.int32, (GLA_HEADS * CHUNK, GLA_QK_W), 1)
    head_sel = (qr // CHUNK) == (qc // GLA_DK)
    br = lax.broadcasted_iota(jnp.int32, (GLA_V_W, GLA_QK_W), 0)
    bc = lax.broadcasted_iota(jnp.int32, (GLA_V_W, GLA_QK_W), 1)
    blockdiag = (br // GLA_DV) == (bc // GLA_DK)
    gout = gout_ref[...]

    for ci in range(n_chunks):
        rows = pl.ds(ci * CHUNK, CHUNK)
        la = loga_ref[0, rows, :]
        cum = jnp.dot(tri, la, preferred_element_type=F32, precision=HIGHEST)
        cl = cum[CHUNK - 1:CHUNK, :]
        q = zqk_ref[0, rows, 0:GLA_QK_W]
        k = zqk_ref[0, rows, GLA_QK_W:2 * GLA_QK_W]
        qd = (q * (GLA_DK ** -0.5) * jnp.exp(cum)).astype(BF16)
        k_inv = (k * jnp.exp(-cum)).astype(BF16)
        k_end = (k * jnp.exp(cl - cum)).astype(BF16)
        v = zv_ref[0, rows, :]

        q_stack = jnp.where(head_sel, jnp.concatenate([qd] * GLA_HEADS, axis=0), 0.0)
        scores = jnp.where(causal4, _nt(q_stack, k_inv), 0.0).astype(BF16)
        full = jnp.dot(scores, v, preferred_element_type=F32)
        st = st_ref[...]
        inter = _nt(qd, st.astype(BF16))
        kv_t = _tn(v, k_end)
        st_ref[...] = st * jnp.exp(cl) + jnp.where(blockdiag, kv_t, 0.0)

        for hh in range(GLA_HEADS):
            cols = slice(hh * GLA_DV, (hh + 1) * GLA_DV)
            o_h = inter[:, cols] + full[hh * CHUNK:(hh + 1) * CHUNK, cols]
            g_h = gate_ref[0, rows, cols].astype(F32)
            o_ref[0, rows, cols] = (_rms(o_h) * gout * g_h).astype(BF16)


def _attn_kernel(qt_ref, k_ref, vt_ref, o_ref, acc_ref, m_ref, l_ref, *, tq, tk):
    qi = pl.program_id(2)
    qt = qt_ref[0]
    m_ref[...] = jnp.full(m_ref.shape, NEG, F32)
    l_ref[...] = jnp.zeros(l_ref.shape, F32)
    acc_ref[...] = jnp.zeros(acc_ref.shape, F32)

    def step(start, mask):
        kb = k_ref[0, pl.ds(start, tk), :]
        vb = vt_ref[0, :, pl.ds(start, tk)]
        s = jnp.dot(kb, qt, preferred_element_type=F32)
        if mask is not None:
            s = jnp.where(mask, s, NEG)
        m_old = m_ref[...]
        m_new = jnp.maximum(m_old, jnp.max(s, axis=0, keepdims=True))
        alpha = jnp.exp(m_old - m_new)
        p = jnp.exp(s - m_new)
        l_ref[...] = alpha * l_ref[...] + jnp.sum(p, axis=0, keepdims=True)
        acc_ref[...] = alpha * acc_ref[...] + jnp.dot(vb, p.astype(BF16),
                                                      preferred_element_type=F32)
        m_ref[...] = m_new

    def body(j, carry):
        step(pl.multiple_of(j * tk, tk), None)
        return carry

    lax.fori_loop(0, qi * (tq // tk), body, 0)

    kr = lax.broadcasted_iota(jnp.int32, (tk, tq), 0)
    qc = lax.broadcasted_iota(jnp.int32, (tk, tq), 1)
    for d in range(tq // tk):
        mask = ((kr + d * tk) // CHUNK) <= (qc // CHUNK)
        step(pl.multiple_of(qi * tq + d * tk, tk), mask)

    o_ref[0] = (acc_ref[...] / l_ref[...]).astype(BF16)


def _mlp_kernel(x_ref, og_ref, ot_ref, w_og_ref, w_om_ref, g_mlp_ref, w_up_ref, w_dn_ref,
                y_ref, *, ff_block):
    mix = jnp.dot(og_ref[0], w_og_ref[...], preferred_element_type=F32)
    mix = mix + _tn(ot_ref[0], w_om_ref[...])
    x1 = x_ref[0] + mix
    h = (_rms(x1) * g_mlp_ref[...]).astype(BF16)
    acc = x1
    d_ff = w_up_ref.shape[1]
    for f in range(d_ff // ff_block):
        cols = slice(f * ff_block, (f + 1) * ff_block)
        u = jnp.dot(h, w_up_ref[:, cols], preferred_element_type=F32)
        a = jnp.square(jnp.maximum(u, 0.0)).astype(BF16)
        acc = acc + jnp.dot(a, w_dn_ref[cols, :], preferred_element_type=F32)
    y_ref[0] = acc


def _const_spec(shape):
    nd = len(shape)
    return pl.BlockSpec(shape, lambda *_: (0,) * nd)


def _pack_weights(w_in, w_gate_up, b_gate, w_uq, w_ukv, q_head_norm, k_head_norm, w_out):
    d_model = w_in.shape[0]
    o = 0
    parts = {}
    for name, width in (("q", GLA_QK_W), ("k", GLA_QK_W), ("v", GLA_V_W), ("gate", GLA_GATE_RANK),
                        ("g", GLA_V_W), ("cq", MLA_Q_RANK), ("ckv", MLA_KV_RANK), ("pe", MLA_ROPE)):
        parts[name] = w_in[:, o:o + width]
        o += width
    misc = jnp.zeros((d_model, LANES), w_in.dtype)
    misc = misc.at[:, MISC_GATE:MISC_GATE + GLA_GATE_RANK].set(parts["gate"])
    misc = misc.at[:, MISC_PE:MISC_PE + MLA_ROPE].set(parts["pe"])
    w_in_p = jnp.concatenate([parts["q"], parts["k"], parts["v"], parts["g"], parts["cq"],
                              parts["ckv"], misc], axis=1).astype(BF16)

    w_gate_p = jnp.zeros((LANES, GLA_QK_W), F32).at[MISC_GATE:MISC_GATE + GLA_GATE_RANK].set(w_gate_up)
    b_gate_p = b_gate.reshape(1, GLA_QK_W).astype(F32)

    w_uq_h = w_uq.reshape(MLA_Q_RANK, MLA_HEADS, MLA_QK)
    w_uq_h = jnp.pad(w_uq_h, ((0, 0), (0, 0), (0, HEAD_PAD - MLA_QK)))
    w_uqt = w_uq_h.reshape(MLA_Q_RANK, MLA_HEADS * HEAD_PAD).T.astype(BF16)

    w_ukv_h = w_ukv.reshape(MLA_KV_RANK, MLA_HEADS, MLA_NOPE + MLA_V)
    w_k = jnp.pad(w_ukv_h[:, :, :MLA_NOPE], ((0, 0), (0, 0), (0, HEAD_PAD - MLA_NOPE)))
    w_k = w_k.reshape(MLA_KV_RANK, MLA_HEADS * HEAD_PAD).astype(BF16)
    w_vt = w_ukv_h[:, :, MLA_NOPE:].reshape(MLA_KV_RANK, MLA_HEADS * MLA_V).T.astype(BF16)

    qg_col = jnp.pad(q_head_norm, (0, HEAD_PAD - MLA_QK)).reshape(HEAD_PAD, 1).astype(F32)
    kg_nope = jnp.pad(k_head_norm[:MLA_NOPE], (0, LANES - MLA_NOPE)).reshape(1, LANES).astype(F32)
    kg_rope = jnp.zeros((LANES,), F32).at[MISC_PE:MISC_PE + MLA_ROPE].set(k_head_norm[MLA_NOPE:])
    kg_rope = kg_rope.reshape(1, LANES)

    w_og = w_out[:GLA_V_W].astype(BF16)
    w_om = w_out[GLA_V_W:].astype(BF16)
    return w_in_p, w_gate_p, b_gate_p, w_uqt, w_k, w_vt, qg_col, kg_nope, kg_rope, w_og, w_om


def _layer(x, posc, posr, invf_col, invf_row, attn_norm, w_in, w_gate_up, b_gate, gla_out_norm,
           q_a_norm, w_uq, kv_a_norm, w_ukv, q_head_norm, k_head_norm, w_out, mlp_norm, w_up,
           w_down):
    b_, s_, d_model = x.shape
    tm = min(TOKEN_BLOCK, s_)
    tg = min(GLA_BLOCK, s_)
    tq = min(ATTN_TQ, s_)
    tk = min(ATTN_TK, tq)
    assert s_ % tm == 0 and s_ % tg == 0 and s_ % tq == 0 and tq % tk == 0 and tk % CHUNK == 0
    d_ff = w_up.shape[1]
    ff_block = min(FF_BLOCK, d_ff)

    (w_in_p, w_gate_p, b_gate_p, w_uqt, w_k, w_vt, qg_col, kg_nope, kg_rope, w_og,
     w_om) = _pack_weights(w_in, w_gate_up, b_gate, w_uq, w_ukv, q_head_norm, k_head_norm, w_out)
    row = lambda v: v.reshape(1, -1).astype(F32)

    tok = lambda w: pl.BlockSpec((1, tm, w), lambda b, i: (b, i, 0))
    tok_t = lambda hgt: pl.BlockSpec((1, hgt, tm), lambda b, i: (b, 0, i))
    consts = (row(attn_norm), w_in_p, w_gate_p, b_gate_p, row(q_a_norm), w_uqt, row(kv_a_norm),
              w_k, w_vt, qg_col, kg_nope, kg_rope, invf_col, invf_row)
    zqk, loga, zv, gate, qt, kk, vt = pl.pallas_call(
        _proj_kernel,
        grid=(b_, s_ // tm),
        in_specs=[tok(d_model), tok(1), tok_t(1)] + [_const_spec(c.shape) for c in consts],
        out_specs=[tok(2 * GLA_QK_W), tok(GLA_QK_W), tok(GLA_V_W), tok(GLA_V_W),
                   tok_t(MLA_HEADS * HEAD_PAD), tok(MLA_HEADS * HEAD_PAD), tok_t(MLA_HEADS * MLA_V)],
        out_shape=[jax.ShapeDtypeStruct((b_, s_, 2 * GLA_QK_W), F32),
                   jax.ShapeDtypeStruct((b_, s_, GLA_QK_W), F32),
                   jax.ShapeDtypeStruct((b_, s_, GLA_V_W), BF16),
                   jax.ShapeDtypeStruct((b_, s_, GLA_V_W), BF16),
                   jax.ShapeDtypeStruct((b_, MLA_HEADS * HEAD_PAD, s_), BF16),
                   jax.ShapeDtypeStruct((b_, s_, MLA_HEADS * HEAD_PAD), BF16),
                   jax.ShapeDtypeStruct((b_, MLA_HEADS * MLA_V, s_), BF16)],
        compiler_params=pltpu.CompilerParams(dimension_semantics=("parallel", "parallel"),
                                             vmem_limit_bytes=VMEM_LIMIT),
        name="proj",
    )(x, posc, posr, *consts)

    gtok = lambda w: pl.BlockSpec((1, tg, w), lambda b, i: (b, i, 0))
    o_gla = pl.pallas_call(
        functools.partial(_gla_kernel, n_chunks=tg // CHUNK),
        grid=(b_, s_ // tg),
        in_specs=[gtok(2 * GLA_QK_W), gtok(GLA_QK_W), gtok(GLA_V_W), gtok(GLA_V_W),
                  _const_spec((1, GLA_DV))],
        out_specs=gtok(GLA_V_W),
        out_shape=jax.ShapeDtypeStruct((b_, s_, GLA_V_W), BF16),
        scratch_shapes=[pltpu.VMEM((GLA_V_W, GLA_QK_W), F32)],
        compiler_params=pltpu.CompilerParams(dimension_semantics=("parallel", "arbitrary"),
                                             vmem_limit_bytes=VMEM_LIMIT),
        name="gla",
    )(zqk, loga, zv, gate, row(gla_out_norm))

    o_t = pl.pallas_call(
        functools.partial(_attn_kernel, tq=tq, tk=tk),
        grid=(b_, MLA_HEADS, s_ // tq),
        in_specs=[pl.BlockSpec((1, HEAD_PAD, tq), lambda b, h, i: (b, h, i)),
                  pl.BlockSpec((1, s_, HEAD_PAD), lambda b, h, i: (b, 0, h)),
                  pl.BlockSpec((1, MLA_V, s_), lambda b, h, i: (b, h, 0))],
        out_specs=pl.BlockSpec((1, MLA_V, tq), lambda b, h, i: (b, h, i)),
        out_shape=jax.ShapeDtypeStruct((b_, MLA_HEADS * MLA_V, s_), BF16),
        scratch_shapes=[pltpu.VMEM((MLA_V, tq), F32), pltpu.VMEM((1, tq), F32),
                        pltpu.VMEM((1, tq), F32)],
        compiler_params=pltpu.CompilerParams(
            dimension_semantics=("parallel", "parallel", "arbitrary"),
            vmem_limit_bytes=VMEM_LIMIT),
        name="attn",
    )(qt, kk, vt)

    y = pl.pallas_call(
        functools.partial(_mlp_kernel, ff_block=ff_block),
        grid=(b_, s_ // tm),
        in_specs=[tok(d_model), tok(GLA_V_W), tok_t(MLA_HEADS * MLA_V),
                  _const_spec(w_og.shape), _const_spec(w_om.shape), _const_spec((1, d_model)),
                  _const_spec(w_up.shape), _const_spec(w_down.shape)],
        out_specs=tok(d_model),
        out_shape=jax.ShapeDtypeStruct((b_, s_, d_model), x.dtype),
        compiler_params=pltpu.CompilerParams(dimension_semantics=("parallel", "parallel"),
                                             vmem_limit_bytes=VMEM_LIMIT),
        name="mlp",
    )(x, o_gla, o_t, w_og, w_om, row(mlp_norm), w_up.astype(BF16), w_down.astype(BF16))
    return y


def kernel(x, positions, attn_norm, w_in, w_gate_up, b_gate, gla_out_norm, q_a_norm, w_uq,
           kv_a_norm, w_ukv, q_head_norm, k_head_norm, w_out, mlp_norm, w_up, w_down):
    b_, s_, _ = x.shape
    posc = positions.reshape(b_, s_, 1)
    posr = positions.reshape(b_, 1, s_)
    inv_freq = ROPE_BASE ** (-jnp.arange(0, MLA_ROPE, 2, dtype=F32) / MLA_ROPE)
    invf_col = inv_freq.reshape(MLA_HALF, 1)
    invf_row = jnp.zeros((LANES,), F32)
    invf_row = invf_row.at[MISC_PE:MISC_PE + MLA_HALF].set(inv_freq)
    invf_row = invf_row.at[MISC_PE + MLA_HALF:MISC_PE + MLA_ROPE].set(inv_freq).reshape(1, LANES)
    for l in range(attn_norm.shape[0]):
        x = _layer(x, posc, posr, invf_col, invf_row, attn_norm[l], w_in[l], w_gate_up[l],
                   b_gate[l], gla_out_norm[l], q_a_norm[l], w_uq[l], kv_a_norm[l], w_ukv[l],
                   q_head_norm[l], k_head_norm[l], w_out[l], mlp_norm[l], w_up[l], w_down[l])
    return x
```

```python
import functools
import math

import jax
import jax.numpy as jnp
from jax import lax
from jax.experimental import pallas as pl
from jax.experimental.pallas import tpu as pltpu

F32 = jnp.float32
BF16 = jnp.bfloat16
HIGHEST = lax.Precision.HIGHEST

CHUNK = 64
EPS = 1e-6

GLA_HEADS = 4
GLA_DK = 64
GLA_DV = 128
GLA_GATE_RANK = 16
GLA_GATE_NORMALIZER = 16.0
GLA_LOG_GATE_MIN = -1.0
GLA_QK_W = GLA_HEADS * GLA_DK
GLA_V_W = GLA_HEADS * GLA_DV

MLA_HEADS = 8
MLA_Q_RANK = 256
MLA_KV_RANK = 128
MLA_NOPE = 64
MLA_ROPE = 32
MLA_HALF = MLA_ROPE // 2
MLA_QK = MLA_NOPE + MLA_ROPE
MLA_V = 64
ROPE_BASE = 10000.0

LANES = 128
HEAD_PAD = LANES
BF16_SUBLANES = 16
LOG2_E = math.log2(math.e)
NEG = -1e30

C_Q = 0
C_K = C_Q + GLA_QK_W
C_V = C_K + GLA_QK_W
C_G = C_V + GLA_V_W
C_CQ = C_G + GLA_V_W
C_CKV = C_CQ + MLA_Q_RANK
C_MISC = C_CKV + MLA_KV_RANK
D_PROJ_PAD = C_MISC + LANES
MISC_GATE = 0
MISC_PE = MLA_NOPE

TOKEN_BLOCK = 512
GLA_BLOCK = 512
ATTN_TQ = 1024
ATTN_TK = 256
FF_BLOCK = 1024
VMEM_LIMIT = 56 * 1024 * 1024


def _nt(a, b):
    return lax.dot_general(a, b, (((1,), (1,)), ((), ())), preferred_element_type=F32)


def _tn(a, b):
    return lax.dot_general(a, b, (((0,), (0,)), ((), ())), preferred_element_type=F32)


def _rms(v):
    return v * lax.rsqrt(jnp.mean(v * v, axis=-1, keepdims=True) + EPS)


def _proj_kernel(x_ref, posc_ref, posr_ref, g_attn_ref, w_in_ref, w_gate_ref, b_gate_ref,
                 qa_g_ref, w_uqt_ref, kva_g_ref, w_k_ref, w_vt_ref,
                 qg_col_ref, kg_nope_ref, kg_rope_ref, invf_col_ref, invf_row_ref,
                 zqk_ref, loga_ref, zv_ref, gate_ref, qt_ref, k_ref, vt_ref):
    x = x_ref[0]
    h = _rms(x) * g_attn_ref[...]
    z = jnp.dot(h.astype(BF16), w_in_ref[...], preferred_element_type=F32)

    zqk_ref[0] = z[:, C_Q:C_V]
    zv_ref[0] = z[:, C_V:C_G].astype(BF16)
    zg = z[:, C_G:C_CQ]
    gate_ref[0] = (zg * jax.nn.sigmoid(zg)).astype(BF16)

    misc = z[:, C_MISC:C_MISC + LANES]
    logit = jnp.dot(misc, w_gate_ref[...], preferred_element_type=F32, precision=HIGHEST)
    logit = logit + b_gate_ref[...]
    log_sig = jnp.minimum(logit, 0.0) - jnp.log1p(jnp.exp(-jnp.abs(logit)))
    loga_ref[0] = jnp.maximum(log_sig / GLA_GATE_NORMALIZER, GLA_LOG_GATE_MIN)

    cq = (_rms(z[:, C_CQ:C_CKV]) * qa_g_ref[...]).astype(BF16)
    qt = _nt(w_uqt_ref[...], cq)
    ang_t = invf_col_ref[...] * posr_ref[0].astype(F32)
    cos_t = jnp.cos(ang_t)
    sin_t = jnp.sin(ang_t)
    qscale = MLA_QK ** -0.5 * LOG2_E
    r1 = MLA_NOPE
    r2 = MLA_NOPE + MLA_HALF
    r3 = MLA_QK
    for hh in range(MLA_HEADS):
        base = hh * HEAD_PAD
        blk = qt[base:base + HEAD_PAD, :]
        ssq = jnp.sum(blk * blk, axis=0, keepdims=True)
        rs = lax.rsqrt(ssq * (1.0 / MLA_QK) + EPS) * qscale
        qn = blk * rs * qg_col_ref[...]
        x1 = qn[r1:r2, :]
        x2 = qn[r2:r3, :]
        qt_ref[0, base:base + r1, :] = qn[:r1, :].astype(BF16)
        qt_ref[0, base + r1:base + r2, :] = (x1 * cos_t - x2 * sin_t).astype(BF16)
        qt_ref[0, base + r2:base + r3, :] = (x1 * sin_t + x2 * cos_t).astype(BF16)
        qt_ref[0, base + r3:base + HEAD_PAD, :] = jnp.zeros((HEAD_PAD - r3, blk.shape[1]), BF16)

    ckv = (_rms(z[:, C_CKV:C_MISC]) * kva_g_ref[...]).astype(BF16)
    vt_ref[0] = _nt(w_vt_ref[...], ckv).astype(BF16)
    kn = jnp.dot(ckv, w_k_ref[...], preferred_element_type=F32)

    lane = lax.broadcasted_iota(jnp.int32, (1, LANES), 1)
    is_lo = (lane >= MISC_PE) & (lane < MISC_PE + MLA_HALF)
    is_hi = (lane >= MISC_PE + MLA_HALF) & (lane < MISC_PE + MLA_ROPE)
    kpe = jnp.where(is_lo | is_hi, misc, 0.0)
    ssq_pe = jnp.sum(kpe * kpe, axis=-1, keepdims=True)
    xg = kpe * kg_rope_ref[...]
    ang = posc_ref[0].astype(F32) * invf_row_ref[...]
    cos_k = jnp.cos(ang)
    sin_k = jnp.sin(ang)
    rot = (xg * cos_k
           + pltpu.roll(xg, LANES - MLA_HALF, 1) * jnp.where(is_lo, -sin_k, 0.0)
           + pltpu.roll(xg, MLA_HALF, 1) * jnp.where(is_hi, sin_k, 0.0))
    for hh in range(MLA_HEADS):
        base = hh * HEAD_PAD
        kn_h = kn[:, base:base + HEAD_PAD]
        ssq = jnp.sum(kn_h * kn_h, axis=-1, keepdims=True) + ssq_pe
        rs = lax.rsqrt(ssq * (1.0 / MLA_QK) + EPS)
        k_ref[0, :, base:base + HEAD_PAD] = ((kn_h * kg_nope_ref[...] + rot) * rs).astype(BF16)


def _gla_kernel(zqk_ref, loga_ref, zv_ref, gate_ref, gout_ref, o_ref, st_ref, *, n_chunks):
    @pl.when(pl.program_id(1) == 0)
    def _():
        st_ref[...] = jnp.zeros(st_ref.shape, F32)

    r = lax.broadcasted_iota(jnp.int32, (CHUNK, CHUNK), 0)
    c = lax.broadcasted_iota(jnp.int32, (CHUNK, CHUNK), 1)
    tri = (r >= c).astype(F32)
    sr = lax.broadcasted_iota(jnp.int32, (GLA_HEADS * CHUNK, CHUNK), 0)
    sc_ = lax.broadcasted_iota(jnp.int32, (GLA_HEADS * CHUNK, CHUNK), 1)
    causal4 = (sr % CHUNK) >= sc_
    qr = lax.broadcasted_iota(jnp.int32, (GLA_HEADS * CHUNK, GLA_QK_W), 0)
    qc = lax.broadcasted_iota(jnp.int32, (GLA_HEADS * CHUNK, GLA_QK_W), 1)
    head_sel = (qr // CHUNK) == (qc // GLA_DK)
    br = lax.broadcasted_iota(jnp.int32, (GLA_V_W, GLA_QK_W), 0)
    bc = lax.broadcasted_iota(jnp.int32, (GLA_V_W, GLA_QK_W), 1)
    blockdiag = (br // GLA_DV) == (bc // GLA_DK)
    gout = gout_ref[...]

    for ci in range(n_chunks):
        rows = pl.ds(ci * CHUNK, CHUNK)
        la = loga_ref[0, rows, :]
        cum = jnp.dot(tri, la, preferred_element_type=F32, precision=HIGHEST)
        cl = cum[CHUNK - 1:CHUNK, :]
        q = zqk_ref[0, rows, 0:GLA_QK_W]
        k = zqk_ref[0, rows, GLA_QK_W:2 * GLA_QK_W]
        qd = (q * (GLA_DK ** -0.5) * jnp.exp(cum)).astype(BF16)
        k_inv = (k * jnp.exp(-cum)).astype(BF16)
        k_end = (k * jnp.exp(cl - cum)).astype(BF16)
        v = zv_ref[0, rows, :]

        q_stack = jnp.where(head_sel, jnp.concatenate([qd] * GLA_HEADS, axis=0), 0.0)
        scores = jnp.where(causal4, _nt(q_stack, k_inv), 0.0).astype(BF16)
        full = jnp.dot(scores, v, preferred_element_type=F32)
        st = st_ref[...]
        inter = _nt(qd, st.astype(BF16))
        kv_t = _tn(v, k_end)
        st_ref[...] = st * jnp.exp(cl) + jnp.where(blockdiag, kv_t, 0.0)

        for hh in range(GLA_HEADS):
            cols = slice(hh * GLA_DV, (hh + 1) * GLA_DV)
            o_h = inter[:, cols] + full[hh * CHUNK:(hh + 1) * CHUNK, cols]
            g_h = gate_ref[0, rows, cols].astype(F32)
            o_ref[0, rows, cols] = (_rms(o_h) * gout * g_h).astype(BF16)


def _attn_kernel(qt_ref, k_ref, vt_ref, o_ref, s_ref, p_ref, acc_ref, m_ref, a_ref, bm_ref, *,
                 tq, tk):
    qi = pl.program_id(2)
    ndiag = tq // tk
    ones = jnp.ones((BF16_SUBLANES, tk), BF16)

    m_ref[...] = jnp.full(m_ref.shape, NEG, F32)
    acc_ref[...] = jnp.zeros(acc_ref.shape, F32)
    a_ref[...] = jnp.ones(a_ref.shape, F32)
    p_ref[2] = jnp.zeros(p_ref.shape[1:], BF16)
    p_ref[3] = jnp.zeros(p_ref.shape[1:], BF16)

    kr = lax.broadcasted_iota(jnp.int32, (tk, tk), 0)
    qc = lax.broadcasted_iota(jnp.int32, (tk, tk), 1)
    square_mask = (kr // CHUNK) <= (qc // CHUNK)

    def qk(j, slot, c0=0):
        start = pl.multiple_of(j * tk, tk)
        s = jnp.dot(k_ref[0, pl.ds(start, tk), :], qt_ref[0, :, c0:], preferred_element_type=F32)
        s_ref[slot, :, c0:] = s
        bm_ref[slot, :, c0:] = jnp.max(s, axis=0, keepdims=True)

    def mask_square(slot, c0):
        cols = slice(c0, c0 + tk)
        s = jnp.where(square_mask, s_ref[slot, :, cols], NEG)
        s_ref[slot, :, cols] = s
        bm_ref[slot, :, cols] = jnp.max(s, axis=0, keepdims=True)

    def softmax(slot, c0=0):
        m_old = m_ref[:, c0:]
        m_new = jnp.maximum(m_old, bm_ref[slot, :, c0:])
        a_ref[slot, :, c0:] = jnp.exp2(m_old - m_new)
        p_ref[slot, :, c0:] = jnp.exp2(s_ref[slot, :, c0:] - m_new).astype(BF16)
        m_ref[:, c0:] = m_new

    def pv_acc(j, slot, c0=0):
        start = pl.multiple_of(jnp.maximum(j, 0) * tk, tk)
        vb = jnp.concatenate([vt_ref[0, :, pl.ds(start, tk)], ones], axis=0)
        pv = jnp.dot(vb, p_ref[slot, :, c0:], preferred_element_type=F32)
        acc_ref[:, c0:] = a_ref[slot, :, c0:] * acc_ref[:, c0:] + pv

    qk(0, 0)
    qk(1, 1)

    def body(i, carry):
        for half in range(2):
            j = 4 * i + 2 * half
            cur = (2 * half, 2 * half + 1)
            nxt = (2 - 2 * half, 3 - 2 * half)
            qk(j + 2, nxt[0])
            qk(j + 3, nxt[1])
            pv_acc(j - 2, nxt[0])
            pv_acc(j - 1, nxt[1])
            softmax(cur[0])
            softmax(cur[1])
        return carry

    lax.fori_loop(0, qi * (ndiag // 4), body, 0)

    nfull = qi * ndiag
    c0 = lambda d: max(d, 0) * tk
    for d in range(0, ndiag, 2):
        for e in (d + 2, d + 3):
            if e < ndiag:
                qk(nfull + e, e % 4, c0(e))
        for e in (d - 2, d - 1):
            pv_acc(nfull + e, e % 4, c0(e))
        for e in (d, d + 1):
            mask_square(e % 4, c0(e))
            softmax(e % 4, c0(e))
    for e in (ndiag - 2, ndiag - 1):
        pv_acc(nfull + e, e % 4, c0(e))

    acc = acc_ref[...]
    o_ref[0] = (acc[:MLA_V, :] / acc[MLA_V:MLA_V + 1, :]).astype(BF16)


def _mlp_kernel(x_ref, og_ref, ot_ref, w_og_ref, w_om_ref, g_mlp_ref, w_up_ref, w_dn_ref,
                y_ref, *, ff_block):
    mix = jnp.dot(og_ref[0], w_og_ref[...], preferred_element_type=F32)
    mix = mix + _tn(ot_ref[0], w_om_ref[...])
    x1 = x_ref[0] + mix
    h = (_rms(x1) * g_mlp_ref[...]).astype(BF16)
    acc = x1
    d_ff = w_up_ref.shape[1]
    for f in range(d_ff // ff_block):
        cols = slice(f * ff_block, (f + 1) * ff_block)
        u = jnp.dot(h, w_up_ref[:, cols], preferred_element_type=F32)
        a = jnp.square(jnp.maximum(u, 0.0)).astype(BF16)
        acc = acc + jnp.dot(a, w_dn_ref[cols, :], preferred_element_type=F32)
    y_ref[0] = acc


def _const_spec(shape):
    nd = len(shape)
    return pl.BlockSpec(shape, lambda *_: (0,) * nd)


def _pack_weights(w_in, w_gate_up, b_gate, w_uq, w_ukv, q_head_norm, k_head_norm, w_out):
    d_model = w_in.shape[0]
    o = 0
    parts = {}
    for name, width in (("q", GLA_QK_W), ("k", GLA_QK_W), ("v", GLA_V_W), ("gate", GLA_GATE_RANK),
                        ("g", GLA_V_W), ("cq", MLA_Q_RANK), ("ckv", MLA_KV_RANK), ("pe", MLA_ROPE)):
        parts[name] = w_in[:, o:o + width]
        o += width
    misc = jnp.zeros((d_model, LANES), w_in.dtype)
    misc = misc.at[:, MISC_GATE:MISC_GATE + GLA_GATE_RANK].set(parts["gate"])
    misc = misc.at[:, MISC_PE:MISC_PE + MLA_ROPE].set(parts["pe"])
    w_in_p = jnp.concatenate([parts["q"], parts["k"], parts["v"], parts["g"], parts["cq"],
                              parts["ckv"], misc], axis=1).astype(BF16)

    w_gate_p = jnp.zeros((LANES, GLA_QK_W), F32).at[MISC_GATE:MISC_GATE + GLA_GATE_RANK].set(w_gate_up)
    b_gate_p = b_gate.reshape(1, GLA_QK_W).astype(F32)

    w_uq_h = w_uq.reshape(MLA_Q_RANK, MLA_HEADS, MLA_QK)
    w_uq_h = jnp.pad(w_uq_h, ((0, 0), (0, 0), (0, HEAD_PAD - MLA_QK)))
    w_uqt = w_uq_h.reshape(MLA_Q_RANK, MLA_HEADS * HEAD_PAD).T.astype(BF16)

    w_ukv_h = w_ukv.reshape(MLA_KV_RANK, MLA_HEADS, MLA_NOPE + MLA_V)
    w_k = jnp.pad(w_ukv_h[:, :, :MLA_NOPE], ((0, 0), (0, 0), (0, HEAD_PAD - MLA_NOPE)))
    w_k = w_k.reshape(MLA_KV_RANK, MLA_HEADS * HEAD_PAD).astype(BF16)
    w_vt = w_ukv_h[:, :, MLA_NOPE:].reshape(MLA_KV_RANK, MLA_HEADS * MLA_V).T.astype(BF16)

    qg_col = jnp.pad(q_head_norm, (0, HEAD_PAD - MLA_QK)).reshape(HEAD_PAD, 1).astype(F32)
    kg_nope = jnp.pad(k_head_norm[:MLA_NOPE], (0, LANES - MLA_NOPE)).reshape(1, LANES).astype(F32)
    kg_rope = jnp.zeros((LANES,), F32).at[MISC_PE:MISC_PE + MLA_ROPE].set(k_head_norm[MLA_NOPE:])
    kg_rope = kg_rope.reshape(1, LANES)

    w_og = w_out[:GLA_V_W].astype(BF16)
    w_om = w_out[GLA_V_W:].astype(BF16)
    return w_in_p, w_gate_p, b_gate_p, w_uqt, w_k, w_vt, qg_col, kg_nope, kg_rope, w_og, w_om


def _layer(x, posc, posr, invf_col, invf_row, attn_norm, w_in, w_gate_up, b_gate, gla_out_norm,
           q_a_norm, w_uq, kv_a_norm, w_ukv, q_head_norm, k_head_norm, w_out, mlp_norm, w_up,
           w_down):
    b_, s_, d_model = x.shape
    tm = min(TOKEN_BLOCK, s_)
    tg = min(GLA_BLOCK, s_)
    tq = min(ATTN_TQ, s_)
    tk = min(ATTN_TK, tq)
    assert s_ % tm == 0 and s_ % tg == 0 and s_ % tq == 0 and tk % CHUNK == 0
    assert tq % (4 * tk) == 0
    d_ff = w_up.shape[1]
    ff_block = min(FF_BLOCK, d_ff)

    (w_in_p, w_gate_p, b_gate_p, w_uqt, w_k, w_vt, qg_col, kg_nope, kg_rope, w_og,
     w_om) = _pack_weights(w_in, w_gate_up, b_gate, w_uq, w_ukv, q_head_norm, k_head_norm, w_out)
    row = lambda v: v.reshape(1, -1).astype(F32)

    tok = lambda w: pl.BlockSpec((1, tm, w), lambda b, i: (b, i, 0))
    tok_t = lambda hgt: pl.BlockSpec((1, hgt, tm), lambda b, i: (b, 0, i))
    consts = (row(attn_norm), w_in_p, w_gate_p, b_gate_p, row(q_a_norm), w_uqt, row(kv_a_norm),
              w_k, w_vt, qg_col, kg_nope, kg_rope, invf_col, invf_row)
    zqk, loga, zv, gate, qt, kk, vt = pl.pallas_call(
        _proj_kernel,
        grid=(b_, s_ // tm),
        in_specs=[tok(d_model), tok(1), tok_t(1)] + [_const_spec(c.shape) for c in consts],
        out_specs=[tok(2 * GLA_QK_W), tok(GLA_QK_W), tok(GLA_V_W), tok(GLA_V_W),
                   tok_t(MLA_HEADS * HEAD_PAD), tok(MLA_HEADS * HEAD_PAD), tok_t(MLA_HEADS * MLA_V)],
        out_shape=[jax.ShapeDtypeStruct((b_, s_, 2 * GLA_QK_W), F32),
                   jax.ShapeDtypeStruct((b_, s_, GLA_QK_W), F32),
                   jax.ShapeDtypeStruct((b_, s_, GLA_V_W), BF16),
                   jax.ShapeDtypeStruct((b_, s_, GLA_V_W), BF16),
                   jax.ShapeDtypeStruct((b_, MLA_HEADS * HEAD_PAD, s_), BF16),
                   jax.ShapeDtypeStruct((b_, s_, MLA_HEADS * HEAD_PAD), BF16),
                   jax.ShapeDtypeStruct((b_, MLA_HEADS * MLA_V, s_), BF16)],
        compiler_params=pltpu.CompilerParams(dimension_semantics=("parallel", "parallel"),
                                             vmem_limit_bytes=VMEM_LIMIT),
        name="proj",
    )(x, posc, posr, *consts)

    gtok = lambda w: pl.BlockSpec((1, tg, w), lambda b, i: (b, i, 0))
    o_gla = pl.pallas_call(
        functools.partial(_gla_kernel, n_chunks=tg // CHUNK),
        grid=(b_, s_ // tg),
        in_specs=[gtok(2 * GLA_QK_W), gtok(GLA_QK_W), gtok(GLA_V_W), gtok(GLA_V_W),
                  _const_spec((1, GLA_DV))],
        out_specs=gtok(GLA_V_W),
        out_shape=jax.ShapeDtypeStruct((b_, s_, GLA_V_W), BF16),
        scratch_shapes=[pltpu.VMEM((GLA_V_W, GLA_QK_W), F32)],
        compiler_params=pltpu.CompilerParams(dimension_semantics=("parallel", "arbitrary"),
                                             vmem_limit_bytes=VMEM_LIMIT),
        name="gla",
    )(zqk, loga, zv, gate, row(gla_out_norm))

    o_t = pl.pallas_call(
        functools.partial(_attn_kernel, tq=tq, tk=tk),
        grid=(b_, MLA_HEADS, s_ // tq),
        in_specs=[pl.BlockSpec((1, HEAD_PAD, tq), lambda b, h, i: (b, h, i)),
                  pl.BlockSpec((1, s_, HEAD_PAD), lambda b, h, i: (b, 0, h)),
                  pl.BlockSpec((1, MLA_V, s_), lambda b, h, i: (b, h, 0))],
        out_specs=pl.BlockSpec((1, MLA_V, tq), lambda b, h, i: (b, h, i)),
        out_shape=jax.ShapeDtypeStruct((b_, MLA_HEADS * MLA_V, s_), BF16),
        scratch_shapes=[pltpu.VMEM((4, tk, tq), F32), pltpu.VMEM((4, tk, tq), BF16),
                        pltpu.VMEM((MLA_V + BF16_SUBLANES, tq), F32), pltpu.VMEM((1, tq), F32),
                        pltpu.VMEM((4, 1, tq), F32), pltpu.VMEM((4, 1, tq), F32)],
        compiler_params=pltpu.CompilerParams(
            dimension_semantics=("parallel", "parallel", "arbitrary"),
            vmem_limit_bytes=VMEM_LIMIT),
        name="attn",
    )(qt, kk, vt)

    y = pl.pallas_call(
        functools.partial(_mlp_kernel, ff_block=ff_block),
        grid=(b_, s_ // tm),
        in_specs=[tok(d_model), tok(GLA_V_W), tok_t(MLA_HEADS * MLA_V),
                  _const_spec(w_og.shape), _const_spec(w_om.shape), _const_spec((1, d_model)),
                  _const_spec(w_up.shape), _const_spec(w_down.shape)],
        out_specs=tok(d_model),
        out_shape=jax.ShapeDtypeStruct((b_, s_, d_model), x.dtype),
        compiler_params=pltpu.CompilerParams(dimension_semantics=("parallel", "parallel"),
                                             vmem_limit_bytes=VMEM_LIMIT),
        name="mlp",
    )(x, o_gla, o_t, w_og, w_om, row(mlp_norm), w_up.astype(BF16), w_down.astype(BF16))
    return y


def kernel(x, positions, attn_norm, w_in, w_gate_up, b_gate, gla_out_norm, q_a_norm, w_uq,
           kv_a_norm, w_ukv, q_head_norm, k_head_norm, w_out, mlp_norm, w_up, w_down):
    b_, s_, _ = x.shape
    posc = positions.reshape(b_, s_, 1)
    posr = positions.reshape(b_, 1, s_)
    inv_freq = ROPE_BASE ** (-jnp.arange(0, MLA_ROPE, 2, dtype=F32) / MLA_ROPE)
    invf_col = inv_freq.reshape(MLA_HALF, 1)
    invf_row = jnp.zeros((LANES,), F32)
    invf_row = invf_row.at[MISC_PE:MISC_PE + MLA_HALF].set(inv_freq)
    invf_row = invf_row.at[MISC_PE + MLA_HALF:MISC_PE + MLA_ROPE].set(inv_freq).reshape(1, LANES)
    for l in range(attn_norm.shape[0]):
        x = _layer(x, posc, posr, invf_col, invf_row, attn_norm[l], w_in[l], w_gate_up[l],
                   b_gate[l], gla_out_norm[l], q_a_norm[l], w_uq[l], kv_a_norm[l], w_ukv[l],
                   q_head_norm[l], k_head_norm[l], w_out[l], mlp_norm[l], w_up[l], w_down[l])
    return x
```

```python
import functools
import math

import jax
import jax.numpy as jnp
from jax import lax
from jax.experimental import pallas as pl
from jax.experimental.pallas import tpu as pltpu

F32 = jnp.float32
BF16 = jnp.bfloat16
HIGHEST = lax.Precision.HIGHEST

CHUNK = 64
EPS = 1e-6

GLA_HEADS = 4
GLA_DK = 64
GLA_DV = 128
GLA_GATE_RANK = 16
GLA_GATE_NORMALIZER = 16.0
GLA_LOG_GATE_MIN = -1.0
GLA_QK_W = GLA_HEADS * GLA_DK
GLA_V_W = GLA_HEADS * GLA_DV

MLA_HEADS = 8
MLA_Q_RANK = 256
MLA_KV_RANK = 128
MLA_NOPE = 64
MLA_ROPE = 32
MLA_HALF = MLA_ROPE // 2
MLA_QK = MLA_NOPE + MLA_ROPE
MLA_V = 64
ROPE_BASE = 10000.0

LANES = 128
HEAD_PAD = LANES
BF16_SUBLANES = 16
V_AUG = MLA_V + BF16_SUBLANES
LOG2_E = math.log2(math.e)
NEG = -1e30

C_Q = 0
C_K = C_Q + GLA_QK_W
C_V = C_K + GLA_QK_W
C_G = C_V + GLA_V_W
C_CQ = C_G + GLA_V_W
C_CKV = C_CQ + MLA_Q_RANK
C_MISC = C_CKV + MLA_KV_RANK
D_PROJ_PAD = C_MISC + LANES
MISC_GATE = 0
MISC_PE = MLA_NOPE

TOKEN_BLOCK = 512
GLA_BLOCK = 512
ATTN_TQ = 2048
ATTN_TK = 256
QUERY_GROUP = 256
FF_BLOCK = 1024
VMEM_LIMIT = 56 * 1024 * 1024


def _nt(a, b):
    return lax.dot_general(a, b, (((1,), (1,)), ((), ())), preferred_element_type=F32)


def _tn(a, b):
    return lax.dot_general(a, b, (((0,), (0,)), ((), ())), preferred_element_type=F32)


def _rms(v):
    return v * lax.rsqrt(jnp.mean(v * v, axis=-1, keepdims=True) + EPS)


def _proj_kernel(x_ref, posc_ref, posr_ref, g_attn_ref, w_in_ref, w_gate_ref, b_gate_ref,
                 qa_g_ref, w_uqt_ref, kva_g_ref, w_k_ref, w_vt_ref,
                 qg_col_ref, kg_nope_ref, kg_rope_ref, invf_col_ref, invf_row_ref,
                 zqk_ref, loga_ref, zv_ref, gate_ref, qt_ref, k_ref, vt_ref):
    x = x_ref[0]
    h = _rms(x) * g_attn_ref[...]
    z = jnp.dot(h.astype(BF16), w_in_ref[...], preferred_element_type=F32)

    zqk_ref[0] = z[:, C_Q:C_V]
    zv_ref[0] = z[:, C_V:C_G].astype(BF16)
    zg = z[:, C_G:C_CQ]
    gate_ref[0] = (zg * jax.nn.sigmoid(zg)).astype(BF16)

    misc = z[:, C_MISC:C_MISC + LANES]
    logit = jnp.dot(misc, w_gate_ref[...], preferred_element_type=F32, precision=HIGHEST)
    logit = logit + b_gate_ref[...]
    log_sig = jnp.minimum(logit, 0.0) - jnp.log1p(jnp.exp(-jnp.abs(logit)))
    loga_ref[0] = jnp.maximum(log_sig / GLA_GATE_NORMALIZER, GLA_LOG_GATE_MIN)

    cq = (_rms(z[:, C_CQ:C_CKV]) * qa_g_ref[...]).astype(BF16)
    qt = _nt(w_uqt_ref[...], cq)
    ang_t = invf_col_ref[...] * posr_ref[0].astype(F32)
    cos_t = jnp.cos(ang_t)
    sin_t = jnp.sin(ang_t)
    qscale = MLA_QK ** -0.5 * LOG2_E
    r1 = MLA_NOPE
    r2 = MLA_NOPE + MLA_HALF
    r3 = MLA_QK
    for hh in range(MLA_HEADS):
        base = hh * HEAD_PAD
        blk = qt[base:base + HEAD_PAD, :]
        ssq = jnp.sum(blk * blk, axis=0, keepdims=True)
        rs = lax.rsqrt(ssq * (1.0 / MLA_QK) + EPS) * qscale
        qn = blk * rs * qg_col_ref[...]
        x1 = qn[r1:r2, :]
        x2 = qn[r2:r3, :]
        qt_ref[0, base:base + r1, :] = qn[:r1, :].astype(BF16)
        qt_ref[0, base + r1:base + r2, :] = (x1 * cos_t - x2 * sin_t).astype(BF16)
        qt_ref[0, base + r2:base + r3, :] = (x1 * sin_t + x2 * cos_t).astype(BF16)
        qt_ref[0, base + r3:base + HEAD_PAD, :] = jnp.zeros((HEAD_PAD - r3, blk.shape[1]), BF16)

    ckv = (_rms(z[:, C_CKV:C_MISC]) * kva_g_ref[...]).astype(BF16)
    vt = _nt(w_vt_ref[...], ckv).astype(BF16)
    ones_rows = jnp.ones((V_AUG - MLA_V, vt.shape[1]), BF16)
    for hh in range(MLA_HEADS):
        vt_ref[0, hh * V_AUG:hh * V_AUG + MLA_V, :] = vt[hh * MLA_V:(hh + 1) * MLA_V, :]
        vt_ref[0, hh * V_AUG + MLA_V:(hh + 1) * V_AUG, :] = ones_rows
    kn = jnp.dot(ckv, w_k_ref[...], preferred_element_type=F32)

    lane = lax.broadcasted_iota(jnp.int32, (1, LANES), 1)
    is_lo = (lane >= MISC_PE) & (lane < MISC_PE + MLA_HALF)
    is_hi = (lane >= MISC_PE + MLA_HALF) & (lane < MISC_PE + MLA_ROPE)
    kpe = jnp.where(is_lo | is_hi, misc, 0.0)
    ssq_pe = jnp.sum(kpe * kpe, axis=-1, keepdims=True)
    xg = kpe * kg_rope_ref[...]
    ang = posc_ref[0].astype(F32) * invf_row_ref[...]
    cos_k = jnp.cos(ang)
    sin_k = jnp.sin(ang)
    rot = (xg * cos_k
           + pltpu.roll(xg, LANES - MLA_HALF, 1) * jnp.where(is_lo, -sin_k, 0.0)
           + pltpu.roll(xg, MLA_HALF, 1) * jnp.where(is_hi, sin_k, 0.0))
    for hh in range(MLA_HEADS):
        base = hh * HEAD_PAD
        kn_h = kn[:, base:base + HEAD_PAD]
        ssq = jnp.sum(kn_h * kn_h, axis=-1, keepdims=True) + ssq_pe
        rs = lax.rsqrt(ssq * (1.0 / MLA_QK) + EPS)
        k_ref[0, :, base:base + HEAD_PAD] = ((kn_h * kg_nope_ref[...] + rot) * rs).astype(BF16)


def _gla_kernel(zqk_ref, loga_ref, zv_ref, gate_ref, gout_ref, o_ref, st_ref, *, n_chunks):
    @pl.when(pl.program_id(1) == 0)
    def _():
        st_ref[...] = jnp.zeros(st_ref.shape, F32)

    r = lax.broadcasted_iota(jnp.int32, (CHUNK, CHUNK), 0)
    c = lax.broadcasted_iota(jnp.int32, (CHUNK, CHUNK), 1)
    tri = (r >= c).astype(F32)
    sr = lax.broadcasted_iota(jnp.int32, (GLA_HEADS * CHUNK, CHUNK), 0)
    sc_ = lax.broadcasted_iota(jnp.int32, (GLA_HEADS * CHUNK, CHUNK), 1)
    causal4 = (sr % CHUNK) >= sc_
    qr = lax.broadcasted_iota(jnp.int32, (GLA_HEADS * CHUNK, GLA_QK_W), 0)
    qc = lax.broadcasted_iota(jnp.int32, (GLA_HEADS * CHUNK, GLA_QK_W), 1)
    head_sel = (qr // CHUNK) == (qc // GLA_DK)
    br = lax.broadcasted_iota(jnp.int32, (GLA_V_W, GLA_QK_W), 0)
    bc = lax.broadcasted_iota(jnp.int32, (GLA_V_W, GLA_QK_W), 1)
    blockdiag = (br // GLA_DV) == (bc // GLA_DK)
    gout = gout_ref[...]

    for ci in range(n_chunks):
        rows = pl.ds(ci * CHUNK, CHUNK)
        la = loga_ref[0, rows, :]
        cum = jnp.dot(tri, la, preferred_element_type=F32, precision=HIGHEST)
        cl = cum[CHUNK - 1:CHUNK, :]
        q = zqk_ref[0, rows, 0:GLA_QK_W]
        k = zqk_ref[0, rows, GLA_QK_W:2 * GLA_QK_W]
        qd = (q * (GLA_DK ** -0.5) * jnp.exp(cum)).astype(BF16)
        k_inv = (k * jnp.exp(-cum)).astype(BF16)
        k_end = (k * jnp.exp(cl - cum)).astype(BF16)
        v = zv_ref[0, rows, :]

        q_stack = jnp.where(head_sel, jnp.concatenate([qd] * GLA_HEADS, axis=0), 0.0)
        scores = jnp.where(causal4, _nt(q_stack, k_inv), 0.0).astype(BF16)
        full = jnp.dot(scores, v, preferred_element_type=F32)
        st = st_ref[...]
        inter = _nt(qd, st.astype(BF16))
        kv_t = _tn(v, k_end)
        st_ref[...] = st * jnp.exp(cl) + jnp.where(blockdiag, kv_t, 0.0)

        for hh in range(GLA_HEADS):
            cols = slice(hh * GLA_DV, (hh + 1) * GLA_DV)
            o_h = inter[:, cols] + full[hh * CHUNK:(hh + 1) * CHUNK, cols]
            g_h = gate_ref[0, rows, cols].astype(F32)
            o_ref[0, rows, cols] = (_rms(o_h) * gout * g_h).astype(BF16)


def _attn_kernel(qt_ref, k_ref, vt_ref, o_ref, s_ref, p_ref, acc_ref, m_ref, a_ref, bm_ref, *,
                 tq, tk):
    qi = pl.program_id(2)
    ndiag = tq // tk

    m_ref[...] = jnp.full(m_ref.shape, NEG, F32)
    acc_ref[...] = jnp.zeros(acc_ref.shape, F32)
    a_ref[...] = jnp.ones(a_ref.shape, F32)
    p_ref[2] = jnp.zeros(p_ref.shape[1:], BF16)
    p_ref[3] = jnp.zeros(p_ref.shape[1:], BF16)

    kr = lax.broadcasted_iota(jnp.int32, (tk, tk), 0)
    qc = lax.broadcasted_iota(jnp.int32, (tk, tk), 1)
    square_mask = (kr // CHUNK) <= (qc // CHUNK)

    def qk(j, slot, c0=0, c1=tq):
        start = pl.multiple_of(j * tk, tk)
        s = jnp.dot(k_ref[0, pl.ds(start, tk), :], qt_ref[0, :, c0:c1],
                    preferred_element_type=F32)
        s_ref[slot, :, c0:c1] = s
        bm_ref[slot, :, c0:c1] = jnp.max(s, axis=0, keepdims=True)

    def mask_square(slot, c0):
        cols = slice(c0, c0 + tk)
        s = jnp.where(square_mask, s_ref[slot, :, cols], NEG)
        s_ref[slot, :, cols] = s
        bm_ref[slot, :, cols] = jnp.max(s, axis=0, keepdims=True)

    def softmax(slot, c0=0, c1=tq):
        m_old = m_ref[:, c0:c1]
        m_new = jnp.maximum(m_old, bm_ref[slot, :, c0:c1])
        a_ref[slot, :, c0:c1] = jnp.exp2(m_old - m_new)
        p_ref[slot, :, c0:c1] = jnp.exp2(s_ref[slot, :, c0:c1] - m_new).astype(BF16)
        m_ref[:, c0:c1] = m_new

    def pv_acc(j, slot, c0=0, c1=tq):
        start = pl.multiple_of(jnp.maximum(j, 0) * tk, tk)
        pv = jnp.dot(vt_ref[0, :, pl.ds(start, tk)], p_ref[slot, :, c0:c1],
                     preferred_element_type=F32)
        acc_ref[:, c0:c1] = a_ref[slot, :, c0:c1] * acc_ref[:, c0:c1] + pv

    qk(0, 0)
    qk(1, 1)

    def body(i, carry):
        for half in range(2):
            j = 4 * i + 2 * half
            cur = (2 * half, 2 * half + 1)
            nxt = (2 - 2 * half, 3 - 2 * half)
            for g0 in range(0, tq, QUERY_GROUP):
                g1 = g0 + QUERY_GROUP
                qk(j + 2, nxt[0], g0, g1)
                qk(j + 3, nxt[1], g0, g1)
                pv_acc(j - 2, nxt[0], g0, g1)
                pv_acc(j - 1, nxt[1], g0, g1)
                softmax(cur[0], g0, g1)
                softmax(cur[1], g0, g1)
        return carry

    lax.fori_loop(0, qi * (ndiag // 4), body, 0)

    nfull = qi * ndiag
    c0 = lambda d: max(d, 0) * tk
    for d in range(0, ndiag, 2):
        for e in (d + 2, d + 3):
            if e < ndiag:
                qk(nfull + e, e % 4, c0(e))
        for e in (d - 2, d - 1):
            pv_acc(nfull + e, e % 4, c0(e))
        for e in (d, d + 1):
            mask_square(e % 4, c0(e))
            softmax(e % 4, c0(e))
    for e in (ndiag - 2, ndiag - 1):
        pv_acc(nfull + e, e % 4, c0(e))

    acc = acc_ref[...]
    o_ref[0] = (acc[:MLA_V, :] / acc[MLA_V:MLA_V + 1, :]).astype(BF16)


def _mlp_kernel(x_ref, og_ref, ot_ref, w_og_ref, w_om_ref, g_mlp_ref, w_up_ref, w_dn_ref,
                y_ref, *, ff_block):
    mix = jnp.dot(og_ref[0], w_og_ref[...], preferred_element_type=F32)
    mix = mix + _tn(ot_ref[0], w_om_ref[...])
    x1 = x_ref[0] + mix
    h = (_rms(x1) * g_mlp_ref[...]).astype(BF16)
    acc = x1
    d_ff = w_up_ref.shape[1]
    for f in range(d_ff // ff_block):
        cols = slice(f * ff_block, (f + 1) * ff_block)
        u = jnp.dot(h, w_up_ref[:, cols], preferred_element_type=F32)
        a = jnp.square(jnp.maximum(u, 0.0)).astype(BF16)
        acc = acc + jnp.dot(a, w_dn_ref[cols, :], preferred_element_type=F32)
    y_ref[0] = acc


def _const_spec(shape):
    nd = len(shape)
    return pl.BlockSpec(shape, lambda *_: (0,) * nd)


def _pack_weights(w_in, w_gate_up, b_gate, w_uq, w_ukv, q_head_norm, k_head_norm, w_out):
    d_model = w_in.shape[0]
    o = 0
    parts = {}
    for name, width in (("q", GLA_QK_W), ("k", GLA_QK_W), ("v", GLA_V_W), ("gate", GLA_GATE_RANK),
                        ("g", GLA_V_W), ("cq", MLA_Q_RANK), ("ckv", MLA_KV_RANK), ("pe", MLA_ROPE)):
        parts[name] = w_in[:, o:o + width]
        o += width
    misc = jnp.zeros((d_model, LANES), w_in.dtype)
    misc = misc.at[:, MISC_GATE:MISC_GATE + GLA_GATE_RANK].set(parts["gate"])
    misc = misc.at[:, MISC_PE:MISC_PE + MLA_ROPE].set(parts["pe"])
    w_in_p = jnp.concatenate([parts["q"], parts["k"], parts["v"], parts["g"], parts["cq"],
                              parts["ckv"], misc], axis=1).astype(BF16)

    w_gate_p = jnp.zeros((LANES, GLA_QK_W), F32).at[MISC_GATE:MISC_GATE + GLA_GATE_RANK].set(w_gate_up)
    b_gate_p = b_gate.reshape(1, GLA_QK_W).astype(F32)

    w_uq_h = w_uq.reshape(MLA_Q_RANK, MLA_HEADS, MLA_QK)
    w_uq_h = jnp.pad(w_uq_h, ((0, 0), (0, 0), (0, HEAD_PAD - MLA_QK)))
    w_uqt = w_uq_h.reshape(MLA_Q_RANK, MLA_HEADS * HEAD_PAD).T.astype(BF16)

    w_ukv_h = w_ukv.reshape(MLA_KV_RANK, MLA_HEADS, MLA_NOPE + MLA_V)
    w_k = jnp.pad(w_ukv_h[:, :, :MLA_NOPE], ((0, 0), (0, 0), (0, HEAD_PAD - MLA_NOPE)))
    w_k = w_k.reshape(MLA_KV_RANK, MLA_HEADS * HEAD_PAD).astype(BF16)
    w_vt = w_ukv_h[:, :, MLA_NOPE:].reshape(MLA_KV_RANK, MLA_HEADS * MLA_V).T.astype(BF16)

    qg_col = jnp.pad(q_head_norm, (0, HEAD_PAD - MLA_QK)).reshape(HEAD_PAD, 1).astype(F32)
    kg_nope = jnp.pad(k_head_norm[:MLA_NOPE], (0, LANES - MLA_NOPE)).reshape(1, LANES).astype(F32)
    kg_rope = jnp.zeros((LANES,), F32).at[MISC_PE:MISC_PE + MLA_ROPE].set(k_head_norm[MLA_NOPE:])
    kg_rope = kg_rope.reshape(1, LANES)

    w_og = w_out[:GLA_V_W].astype(BF16)
    w_om = w_out[GLA_V_W:].astype(BF16)
    return w_in_p, w_gate_p, b_gate_p, w_uqt, w_k, w_vt, qg_col, kg_nope, kg_rope, w_og, w_om


def _layer(x, posc, posr, invf_col, invf_row, attn_norm, w_in, w_gate_up, b_gate, gla_out_norm,
           q_a_norm, w_uq, kv_a_norm, w_ukv, q_head_norm, k_head_norm, w_out, mlp_norm, w_up,
           w_down):
    b_, s_, d_model = x.shape
    tm = min(TOKEN_BLOCK, s_)
    tg = min(GLA_BLOCK, s_)
    tq = min(ATTN_TQ, s_)
    tk = min(ATTN_TK, tq)
    assert s_ % tm == 0 and s_ % tg == 0 and s_ % tq == 0 and tk % CHUNK == 0
    assert tq % (4 * tk) == 0
    d_ff = w_up.shape[1]
    ff_block = min(FF_BLOCK, d_ff)

    (w_in_p, w_gate_p, b_gate_p, w_uqt, w_k, w_vt, qg_col, kg_nope, kg_rope, w_og,
     w_om) = _pack_weights(w_in, w_gate_up, b_gate, w_uq, w_ukv, q_head_norm, k_head_norm, w_out)
    row = lambda v: v.reshape(1, -1).astype(F32)

    tok = lambda w: pl.BlockSpec((1, tm, w), lambda b, i: (b, i, 0))
    tok_t = lambda hgt: pl.BlockSpec((1, hgt, tm), lambda b, i: (b, 0, i))
    consts = (row(attn_norm), w_in_p, w_gate_p, b_gate_p, row(q_a_norm), w_uqt, row(kv_a_norm),
              w_k, w_vt, qg_col, kg_nope, kg_rope, invf_col, invf_row)
    zqk, loga, zv, gate, qt, kk, vt = pl.pallas_call(
        _proj_kernel,
        grid=(b_, s_ // tm),
        in_specs=[tok(d_model), tok(1), tok_t(1)] + [_const_spec(c.shape) for c in consts],
        out_specs=[tok(2 * GLA_QK_W), tok(GLA_QK_W), tok(GLA_V_W), tok(GLA_V_W),
                   tok_t(MLA_HEADS * HEAD_PAD), tok(MLA_HEADS * HEAD_PAD), tok_t(MLA_HEADS * V_AUG)],
        out_shape=[jax.ShapeDtypeStruct((b_, s_, 2 * GLA_QK_W), F32),
                   jax.ShapeDtypeStruct((b_, s_, GLA_QK_W), F32),
                   jax.ShapeDtypeStruct((b_, s_, GLA_V_W), BF16),
                   jax.ShapeDtypeStruct((b_, s_, GLA_V_W), BF16),
                   jax.ShapeDtypeStruct((b_, MLA_HEADS * HEAD_PAD, s_), BF16),
                   jax.ShapeDtypeStruct((b_, s_, MLA_HEADS * HEAD_PAD), BF16),
                   jax.ShapeDtypeStruct((b_, MLA_HEADS * V_AUG, s_), BF16)],
        compiler_params=pltpu.CompilerParams(dimension_semantics=("parallel", "parallel"),
                                             vmem_limit_bytes=VMEM_LIMIT),
        name="proj",
    )(x, posc, posr, *consts)

    gtok = lambda w: pl.BlockSpec((1, tg, w), lambda b, i: (b, i, 0))
    o_gla = pl.pallas_call(
        functools.partial(_gla_kernel, n_chunks=tg // CHUNK),
        grid=(b_, s_ // tg),
        in_specs=[gtok(2 * GLA_QK_W), gtok(GLA_QK_W), gtok(GLA_V_W), gtok(GLA_V_W),
                  _const_spec((1, GLA_DV))],
        out_specs=gtok(GLA_V_W),
        out_shape=jax.ShapeDtypeStruct((b_, s_, GLA_V_W), BF16),
        scratch_shapes=[pltpu.VMEM((GLA_V_W, GLA_QK_W), F32)],
        compiler_params=pltpu.CompilerParams(dimension_semantics=("parallel", "arbitrary"),
                                             vmem_limit_bytes=VMEM_LIMIT),
        name="gla",
    )(zqk, loga, zv, gate, row(gla_out_norm))

    o_t = pl.pallas_call(
        functools.partial(_attn_kernel, tq=tq, tk=tk),
        grid=(b_, MLA_HEADS, s_ // tq),
        in_specs=[pl.BlockSpec((1, HEAD_PAD, tq), lambda b, h, i: (b, h, i)),
                  pl.BlockSpec((1, s_, HEAD_PAD), lambda b, h, i: (b, 0, h)),
                  pl.BlockSpec((1, V_AUG, s_), lambda b, h, i: (b, h, 0))],
        out_specs=pl.BlockSpec((1, MLA_V, tq), lambda b, h, i: (b, h, i)),
        out_shape=jax.ShapeDtypeStruct((b_, MLA_HEADS * MLA_V, s_), BF16),
        scratch_shapes=[pltpu.VMEM((4, tk, tq), F32), pltpu.VMEM((4, tk, tq), BF16),
                        pltpu.VMEM((V_AUG, tq), F32), pltpu.VMEM((1, tq), F32),
                        pltpu.VMEM((4, 1, tq), F32), pltpu.VMEM((4, 1, tq), F32)],
        compiler_params=pltpu.CompilerParams(
            dimension_semantics=("parallel", "parallel", "arbitrary"),
            vmem_limit_bytes=VMEM_LIMIT),
        name="attn",
    )(qt, kk, vt)

    y = pl.pallas_call(
        functools.partial(_mlp_kernel, ff_block=ff_block),
        grid=(b_, s_ // tm),
        in_specs=[tok(d_model), tok(GLA_V_W), tok_t(MLA_HEADS * MLA_V),
                  _const_spec(w_og.shape), _const_spec(w_om.shape), _const_spec((1, d_model)),
                  _const_spec(w_up.shape), _const_spec(w_down.shape)],
        out_specs=tok(d_model),
        out_shape=jax.ShapeDtypeStruct((b_, s_, d_model), x.dtype),
        compiler_params=pltpu.CompilerParams(dimension_semantics=("parallel", "parallel"),
                                             vmem_limit_bytes=VMEM_LIMIT),
        name="mlp",
    )(x, o_gla, o_t, w_og, w_om, row(mlp_norm), w_up.astype(BF16), w_down.astype(BF16))
    return y


def kernel(x, positions, attn_norm, w_in, w_gate_up, b_gate, gla_out_norm, q_a_norm, w_uq,
           kv_a_norm, w_ukv, q_head_norm, k_head_norm, w_out, mlp_norm, w_up, w_down):
    b_, s_, _ = x.shape
    posc = positions.reshape(b_, s_, 1)
    posr = positions.reshape(b_, 1, s_)
    inv_freq = ROPE_BASE ** (-jnp.arange(0, MLA_ROPE, 2, dtype=F32) / MLA_ROPE)
    invf_col = inv_freq.reshape(MLA_HALF, 1)
    invf_row = jnp.zeros((LANES,), F32)
    invf_row = invf_row.at[MISC_PE:MISC_PE + MLA_HALF].set(inv_freq)
    invf_row = invf_row.at[MISC_PE + MLA_HALF:MISC_PE + MLA_ROPE].set(inv_freq).reshape(1, LANES)
    for l in range(attn_norm.shape[0]):
        x = _layer(x, posc, posr, invf_col, invf_row, attn_norm[l], w_in[l], w_gate_up[l],
                   b_gate[l], gla_out_norm[l], q_a_norm[l], w_uq[l], kv_a_norm[l], w_ukv[l],
                   q_head_norm[l], k_head_norm[l], w_out[l], mlp_norm[l], w_up[l], w_down[l])
    return x
```

```python
import functools
import math

import jax
import jax.numpy as jnp
from jax import lax
from jax.experimental import pallas as pl
from jax.experimental.pallas import tpu as pltpu

F32 = jnp.float32
BF16 = jnp.bfloat16
HIGHEST = lax.Precision.HIGHEST

CHUNK = 64
EPS = 1e-6

GLA_HEADS = 4
GLA_DK = 64
GLA_DV = 128
GLA_GATE_RANK = 16
GLA_GATE_NORMALIZER = 16.0
GLA_LOG_GATE_MIN = -1.0
GLA_QK_W = GLA_HEADS * GLA_DK
GLA_V_W = GLA_HEADS * GLA_DV

MLA_HEADS = 8
MLA_Q_RANK = 256
MLA_KV_RANK = 128
MLA_NOPE = 64
MLA_ROPE = 32
MLA_HALF = MLA_ROPE // 2
MLA_QK = MLA_NOPE + MLA_ROPE
MLA_V = 64
ROPE_BASE = 10000.0

LANES = 128
HEAD_PAD = LANES
BF16_SUBLANES = 16
V_AUG = MLA_V + BF16_SUBLANES
LOG2_E = math.log2(math.e)
NEG = -1e30

C_Q = 0
C_K = C_Q + GLA_QK_W
C_V = C_K + GLA_QK_W
C_G = C_V + GLA_V_W
C_CQ = C_G + GLA_V_W
C_CKV = C_CQ + MLA_Q_RANK
C_MISC = C_CKV + MLA_KV_RANK
D_PROJ_PAD = C_MISC + LANES
MISC_GATE = 0
MISC_PE = MLA_NOPE

TOKEN_BLOCK = 512
GLA_BLOCK = 512
ATTN_TQ = 2048
ATTN_TK = 256
QUERY_GROUP = 256
FF_BLOCK = 1024
VMEM_LIMIT = 56 * 1024 * 1024


def _nt(a, b):
    return lax.dot_general(a, b, (((1,), (1,)), ((), ())), preferred_element_type=F32)


def _tn(a, b):
    return lax.dot_general(a, b, (((0,), (0,)), ((), ())), preferred_element_type=F32)


def _rms(v):
    return v * lax.rsqrt(jnp.mean(v * v, axis=-1, keepdims=True) + EPS)


def _proj_kernel(x_ref, posr_ref, g_attn_ref, w_in_ref, w_gate_ref, b_gate_ref,
                 qa_g_ref, w_uqt_ref, kva_g_ref, w_k_ref, w_vt_ref,
                 qg_col_ref, kg_nope_ref, kg_rope_ref, invf_col_ref, rope_sel_ref,
                 zqk_ref, loga_ref, zv_ref, gate_ref, qt_ref, k_ref, vt_ref):
    x = x_ref[0]
    h = _rms(x) * g_attn_ref[...]
    z = jnp.dot(h.astype(BF16), w_in_ref[...], preferred_element_type=F32)

    zqk_ref[0] = z[:, C_Q:C_V]
    zv_ref[0] = z[:, C_V:C_G].astype(BF16)
    zg = z[:, C_G:C_CQ]
    gate_ref[0] = (zg * jax.nn.sigmoid(zg)).astype(BF16)

    misc = z[:, C_MISC:C_MISC + LANES]
    logit = jnp.dot(misc, w_gate_ref[...], preferred_element_type=F32, precision=HIGHEST)
    logit = logit + b_gate_ref[...]
    log_sig = jnp.minimum(logit, 0.0) - jnp.log1p(jnp.exp(-jnp.abs(logit)))
    loga_ref[0] = jnp.maximum(log_sig / GLA_GATE_NORMALIZER, GLA_LOG_GATE_MIN)

    cq = (_rms(z[:, C_CQ:C_CKV]) * qa_g_ref[...]).astype(BF16)
    qt = _nt(w_uqt_ref[...], cq)
    ang_t = invf_col_ref[...] * posr_ref[0].astype(F32)
    cos_t = jnp.cos(ang_t)
    sin_t = jnp.sin(ang_t)
    qscale = MLA_QK ** -0.5 * LOG2_E
    r1 = MLA_NOPE
    r2 = MLA_NOPE + MLA_HALF
    r3 = MLA_QK
    for hh in range(MLA_HEADS):
        base = hh * HEAD_PAD
        blk = qt[base:base + HEAD_PAD, :]
        ssq = jnp.sum(blk * blk, axis=0, keepdims=True)
        rs = lax.rsqrt(ssq * (1.0 / MLA_QK) + EPS) * qscale
        qn = blk * rs * qg_col_ref[...]
        x1 = qn[r1:r2, :]
        x2 = qn[r2:r3, :]
        qt_ref[0, base:base + r1, :] = qn[:r1, :].astype(BF16)
        qt_ref[0, base + r1:base + r2, :] = (x1 * cos_t - x2 * sin_t).astype(BF16)
        qt_ref[0, base + r2:base + r3, :] = (x1 * sin_t + x2 * cos_t).astype(BF16)
        qt_ref[0, base + r3:base + HEAD_PAD, :] = jnp.zeros((HEAD_PAD - r3, blk.shape[1]), BF16)

    ckv = (_rms(z[:, C_CKV:C_MISC]) * kva_g_ref[...]).astype(BF16)
    vt = _nt(w_vt_ref[...], ckv).astype(BF16)
    ones_rows = jnp.ones((V_AUG - MLA_V, vt.shape[1]), BF16)
    for hh in range(MLA_HEADS):
        vt_ref[0, hh * V_AUG:hh * V_AUG + MLA_V, :] = vt[hh * MLA_V:(hh + 1) * MLA_V, :]
        vt_ref[0, hh * V_AUG + MLA_V:(hh + 1) * V_AUG, :] = ones_rows
    kn = jnp.dot(ckv, w_k_ref[...], preferred_element_type=F32)

    lane = lax.broadcasted_iota(jnp.int32, (1, LANES), 1)
    is_lo = (lane >= MISC_PE) & (lane < MISC_PE + MLA_HALF)
    is_hi = (lane >= MISC_PE + MLA_HALF) & (lane < MISC_PE + MLA_ROPE)
    kpe = jnp.where(is_lo | is_hi, misc, 0.0)
    ssq_pe = jnp.sum(kpe * kpe, axis=-1, keepdims=True)
    xg = kpe * kg_rope_ref[...]
    cs3 = jnp.concatenate(_split3(jnp.concatenate([cos_t, sin_t], axis=0)), axis=0)
    tabs = _tn(cs3, rope_sel_ref[...])
    rot = (xg * tabs[:, :LANES]
           + pltpu.roll(xg, LANES - MLA_HALF, 1) * tabs[:, LANES:2 * LANES]
           + pltpu.roll(xg, MLA_HALF, 1) * tabs[:, 2 * LANES:])
    for hh in range(MLA_HEADS):
        base = hh * HEAD_PAD
        kn_h = kn[:, base:base + HEAD_PAD]
        ssq = jnp.sum(kn_h * kn_h, axis=-1, keepdims=True) + ssq_pe
        rs = lax.rsqrt(ssq * (1.0 / MLA_QK) + EPS)
        k_ref[0, :, base:base + HEAD_PAD] = ((kn_h * kg_nope_ref[...] + rot) * rs).astype(BF16)


def _split3(v):
    hi = v.astype(BF16)
    r1 = v - hi.astype(F32)
    mid = r1.astype(BF16)
    lo = (r1 - mid.astype(F32)).astype(BF16)
    return hi, mid, lo


def _gla_kernel(zqk_ref, loga_ref, zv_ref, gate_ref, gout_ref, tri_ref, o_ref, st_ref, *,
                n_chunks):
    @pl.when(pl.program_id(1) == 0)
    def _():
        st_ref[...] = jnp.zeros(st_ref.shape, F32)

    sr = lax.broadcasted_iota(jnp.int32, (GLA_HEADS * CHUNK, CHUNK), 0)
    sc_ = lax.broadcasted_iota(jnp.int32, (GLA_HEADS * CHUNK, CHUNK), 1)
    causal4 = (sr % CHUNK) >= sc_
    qr = lax.broadcasted_iota(jnp.int32, (GLA_HEADS * CHUNK, GLA_QK_W), 0)
    qc = lax.broadcasted_iota(jnp.int32, (GLA_HEADS * CHUNK, GLA_QK_W), 1)
    head_sel = (qr // CHUNK) == (qc // GLA_DK)
    lane_head = lax.broadcasted_iota(jnp.int32, (GLA_DV, GLA_QK_W), 1) // GLA_DK
    gout = gout_ref[...]
    chunks = [slice(ci * CHUNK, (ci + 1) * CHUNK) for ci in range(n_chunks)]

    la = loga_ref[0]
    parts = [p for r in chunks for p in _split3(la[r])]
    tri_out = jnp.dot(tri_ref[...], jnp.concatenate(parts, axis=1), preferred_element_type=F32)
    w = GLA_QK_W
    cum = [tri_out[:, (3 * ci) * w:(3 * ci + 1) * w] + tri_out[:, (3 * ci + 1) * w:(3 * ci + 2) * w]
           + tri_out[:, (3 * ci + 2) * w:(3 * ci + 3) * w] for ci in range(n_chunks)]
    cl = [c[CHUNK - 1:CHUNK, :] for c in cum]

    def prep(ci):
        r = chunks[ci]
        q = zqk_ref[0, r, 0:GLA_QK_W]
        k = zqk_ref[0, r, GLA_QK_W:2 * GLA_QK_W]
        qd = (q * (GLA_DK ** -0.5) * jnp.exp(cum[ci])).astype(BF16)
        k_inv = (k * jnp.exp(-cum[ci])).astype(BF16)
        k_end = (k * jnp.exp(cl[ci] - cum[ci])).astype(BF16)
        q_stack = jnp.where(head_sel, jnp.concatenate([qd] * GLA_HEADS, axis=0), 0.0)
        return q_stack, k_inv, k_end

    def intra_scores(ci):
        return jnp.where(causal4, _nt(pre[ci][0], pre[ci][1]), 0.0).astype(BF16)

    def intra_out_and_kv(ci):
        v = zv_ref[0, chunks[ci], :]
        full = [jnp.dot(scores[ci][hh * CHUNK:(hh + 1) * CHUNK, :],
                        v[:, hh * GLA_DV:(hh + 1) * GLA_DV], preferred_element_type=F32)
                for hh in range(GLA_HEADS)]
        kv_t = _tn(v, pre[ci][2])
        kv = kv_t[(GLA_HEADS - 1) * GLA_DV:, :]
        for hh in range(GLA_HEADS - 2, -1, -1):
            kv = jnp.where(lane_head == hh, kv_t[hh * GLA_DV:(hh + 1) * GLA_DV, :], kv)
        return full, kv

    pre, scores, intra = {}, {}, {}
    for t in range(n_chunks + 2):
        if t < n_chunks:
            pre[t] = prep(t)
        if 0 <= t - 1 < n_chunks:
            scores[t - 1] = intra_scores(t - 1)
        if 0 <= t - 2 < n_chunks:
            intra[t - 2] = intra_out_and_kv(t - 2)

    st = st_ref[...]
    st_in = []
    for ci in range(n_chunks):
        st_in.append(st.astype(BF16))
        st = st * jnp.exp(cl[ci]) + intra[ci][1]
    st_ref[...] = st

    def finish(ci, inter):
        r = chunks[ci]
        for hh in range(GLA_HEADS):
            cols = slice(hh * GLA_DV, (hh + 1) * GLA_DV)
            o_h = inter[hh * CHUNK:(hh + 1) * CHUNK, :] + intra[ci][0][hh]
            g_h = gate_ref[0, r, cols].astype(F32)
            o_ref[0, r, cols] = (_rms(o_h) * gout * g_h).astype(BF16)

    inter = {}
    for t in range(n_chunks + 1):
        if t < n_chunks:
            inter[t] = _nt(pre[t][0], st_in[t])
        if t >= 1:
            finish(t - 1, inter[t - 1])


def _attn_kernel(qt_ref, k_ref, vt_ref, o_ref, s_ref, p_ref, acc_ref, m_ref, a_ref, bm_ref, *,
                 tq, tk):
    qi = pl.program_id(2)
    ndiag = tq // tk

    m_ref[...] = jnp.full(m_ref.shape, NEG, F32)
    acc_ref[...] = jnp.zeros(acc_ref.shape, F32)
    a_ref[...] = jnp.ones(a_ref.shape, F32)
    p_ref[2] = jnp.zeros(p_ref.shape[1:], BF16)
    p_ref[3] = jnp.zeros(p_ref.shape[1:], BF16)

    kr = lax.broadcasted_iota(jnp.int32, (tk, tk), 0)
    qc = lax.broadcasted_iota(jnp.int32, (tk, tk), 1)
    square_mask = (kr // CHUNK) <= (qc // CHUNK)

    def qk(j, slot, c0=0, c1=tq):
        start = pl.multiple_of(j * tk, tk)
        s = jnp.dot(k_ref[0, pl.ds(start, tk), :], qt_ref[0, :, c0:c1],
                    preferred_element_type=F32)
        s_ref[slot, :, c0:c1] = s
        bm_ref[slot, :, c0:c1] = jnp.max(s, axis=0, keepdims=True)

    def mask_square(slot, c0):
        cols = slice(c0, c0 + tk)
        s = jnp.where(square_mask, s_ref[slot, :, cols], NEG)
        s_ref[slot, :, cols] = s
        bm_ref[slot, :, cols] = jnp.max(s, axis=0, keepdims=True)

    def softmax(slot, c0=0, c1=tq):
        m_old = m_ref[:, c0:c1]
        m_new = jnp.maximum(m_old, bm_ref[slot, :, c0:c1])
        a_ref[slot, :, c0:c1] = jnp.exp2(m_old - m_new)
        p_ref[slot, :, c0:c1] = jnp.exp2(s_ref[slot, :, c0:c1] - m_new).astype(BF16)
        m_ref[:, c0:c1] = m_new

    def pv_acc(j, slot, c0=0, c1=tq):
        start = pl.multiple_of(jnp.maximum(j, 0) * tk, tk)
        pv = jnp.dot(vt_ref[0, :, pl.ds(start, tk)], p_ref[slot, :, c0:c1],
                     preferred_element_type=F32)
        acc_ref[:, c0:c1] = a_ref[slot, :, c0:c1] * acc_ref[:, c0:c1] + pv

    qk(0, 0)
    qk(1, 1)

    def body(i, carry):
        for half in range(2):
            j = 4 * i + 2 * half
            cur = (2 * half, 2 * half + 1)
            nxt = (2 - 2 * half, 3 - 2 * half)
            for g0 in range(0, tq, QUERY_GROUP):
                g1 = g0 + QUERY_GROUP
                qk(j + 2, nxt[0], g0, g1)
                qk(j + 3, nxt[1], g0, g1)
                pv_acc(j - 2, nxt[0], g0, g1)
                pv_acc(j - 1, nxt[1], g0, g1)
                softmax(cur[0], g0, g1)
                softmax(cur[1], g0, g1)
        return carry

    lax.fori_loop(0, qi * (ndiag // 4), body, 0)

    nfull = qi * ndiag
    c0 = lambda d: max(d, 0) * tk
    for d in range(0, ndiag, 2):
        for e in (d + 2, d + 3):
            if e < ndiag:
                qk(nfull + e, e % 4, c0(e))
        for e in (d - 2, d - 1):
            pv_acc(nfull + e, e % 4, c0(e))
        for e in (d, d + 1):
            mask_square(e % 4, c0(e))
            softmax(e % 4, c0(e))
    for e in (ndiag - 2, ndiag - 1):
        pv_acc(nfull + e, e % 4, c0(e))

    acc = acc_ref[...]
    o_ref[0] = (acc[:MLA_V, :] / acc[MLA_V:MLA_V + 1, :]).astype(BF16)


def _mlp_kernel(x_ref, og_ref, ot_ref, w_og_ref, w_om_ref, g_mlp_ref, w_up_ref, w_dn_ref,
                y_ref, *, ff_block):
    mix = jnp.dot(og_ref[0], w_og_ref[...], preferred_element_type=F32)
    mix = mix + _tn(ot_ref[0], w_om_ref[...])
    x1 = x_ref[0] + mix
    h = (_rms(x1) * g_mlp_ref[...]).astype(BF16)
    acc = x1
    d_ff = w_up_ref.shape[1]
    for f in range(d_ff // ff_block):
        cols = slice(f * ff_block, (f + 1) * ff_block)
        u = jnp.dot(h, w_up_ref[:, cols], preferred_element_type=F32)
        a = jnp.square(jnp.maximum(u, 0.0)).astype(BF16)
        acc = acc + jnp.dot(a, w_dn_ref[cols, :], preferred_element_type=F32)
    y_ref[0] = acc


def _const_spec(shape):
    nd = len(shape)
    return pl.BlockSpec(shape, lambda *_: (0,) * nd)


def _pack_weights(w_in, w_gate_up, b_gate, w_uq, w_ukv, q_head_norm, k_head_norm, w_out):
    d_model = w_in.shape[0]
    o = 0
    parts = {}
    for name, width in (("q", GLA_QK_W), ("k", GLA_QK_W), ("v", GLA_V_W), ("gate", GLA_GATE_RANK),
                        ("g", GLA_V_W), ("cq", MLA_Q_RANK), ("ckv", MLA_KV_RANK), ("pe", MLA_ROPE)):
        parts[name] = w_in[:, o:o + width]
        o += width
    misc = jnp.zeros((d_model, LANES), w_in.dtype)
    misc = misc.at[:, MISC_GATE:MISC_GATE + GLA_GATE_RANK].set(parts["gate"])
    misc = misc.at[:, MISC_PE:MISC_PE + MLA_ROPE].set(parts["pe"])
    w_in_p = jnp.concatenate([parts["q"], parts["k"], parts["v"], parts["g"], parts["cq"],
                              parts["ckv"], misc], axis=1).astype(BF16)

    w_gate_p = jnp.zeros((LANES, GLA_QK_W), F32).at[MISC_GATE:MISC_GATE + GLA_GATE_RANK].set(w_gate_up)
    b_gate_p = b_gate.reshape(1, GLA_QK_W).astype(F32)

    w_uq_h = w_uq.reshape(MLA_Q_RANK, MLA_HEADS, MLA_QK)
    w_uq_h = jnp.pad(w_uq_h, ((0, 0), (0, 0), (0, HEAD_PAD - MLA_QK)))
    w_uqt = w_uq_h.reshape(MLA_Q_RANK, MLA_HEADS * HEAD_PAD).T.astype(BF16)

    w_ukv_h = w_ukv.reshape(MLA_KV_RANK, MLA_HEADS, MLA_NOPE + MLA_V)
    w_k = jnp.pad(w_ukv_h[:, :, :MLA_NOPE], ((0, 0), (0, 0), (0, HEAD_PAD - MLA_NOPE)))
    w_k = w_k.reshape(MLA_KV_RANK, MLA_HEADS * HEAD_PAD).astype(BF16)
    w_vt = w_ukv_h[:, :, MLA_NOPE:].reshape(MLA_KV_RANK, MLA_HEADS * MLA_V).T.astype(BF16)

    qg_col = jnp.pad(q_head_norm, (0, HEAD_PAD - MLA_QK)).reshape(HEAD_PAD, 1).astype(F32)
    kg_nope = jnp.pad(k_head_norm[:MLA_NOPE], (0, LANES - MLA_NOPE)).reshape(1, LANES).astype(F32)
    kg_rope = jnp.zeros((LANES,), F32).at[MISC_PE:MISC_PE + MLA_ROPE].set(k_head_norm[MLA_NOPE:])
    kg_rope = kg_rope.reshape(1, LANES)

    w_og = w_out[:GLA_V_W].astype(BF16)
    w_om = w_out[GLA_V_W:].astype(BF16)
    return w_in_p, w_gate_p, b_gate_p, w_uqt, w_k, w_vt, qg_col, kg_nope, kg_rope, w_og, w_om


def _layer(x, posr, invf_col, rope_sel, attn_norm, w_in, w_gate_up, b_gate, gla_out_norm,
           q_a_norm, w_uq, kv_a_norm, w_ukv, q_head_norm, k_head_norm, w_out, mlp_norm, w_up,
           w_down):
    b_, s_, d_model = x.shape
    tm = min(TOKEN_BLOCK, s_)
    tg = min(GLA_BLOCK, s_)
    tq = min(ATTN_TQ, s_)
    tk = min(ATTN_TK, tq)
    assert s_ % tm == 0 and s_ % tg == 0 and s_ % tq == 0 and tk % CHUNK == 0
    assert tq % (4 * tk) == 0
    d_ff = w_up.shape[1]
    ff_block = min(FF_BLOCK, d_ff)

    (w_in_p, w_gate_p, b_gate_p, w_uqt, w_k, w_vt, qg_col, kg_nope, kg_rope, w_og,
     w_om) = _pack_weights(w_in, w_gate_up, b_gate, w_uq, w_ukv, q_head_norm, k_head_norm, w_out)
    row = lambda v: v.reshape(1, -1).astype(F32)

    tok = lambda w: pl.BlockSpec((1, tm, w), lambda b, i: (b, i, 0))
    tok_t = lambda hgt: pl.BlockSpec((1, hgt, tm), lambda b, i: (b, 0, i))
    consts = (row(attn_norm), w_in_p, w_gate_p, b_gate_p, row(q_a_norm), w_uqt, row(kv_a_norm),
              w_k, w_vt, qg_col, kg_nope, kg_rope, invf_col, rope_sel)
    zqk, loga, zv, gate, qt, kk, vt = pl.pallas_call(
        _proj_kernel,
        grid=(b_, s_ // tm),
        in_specs=[tok(d_model), tok_t(1)] + [_const_spec(c.shape) for c in consts],
        out_specs=[tok(2 * GLA_QK_W), tok(GLA_QK_W), tok(GLA_V_W), tok(GLA_V_W),
                   tok_t(MLA_HEADS * HEAD_PAD), tok(MLA_HEADS * HEAD_PAD), tok_t(MLA_HEADS * V_AUG)],
        out_shape=[jax.ShapeDtypeStruct((b_, s_, 2 * GLA_QK_W), F32),
                   jax.ShapeDtypeStruct((b_, s_, GLA_QK_W), F32),
                   jax.ShapeDtypeStruct((b_, s_, GLA_V_W), BF16),
                   jax.ShapeDtypeStruct((b_, s_, GLA_V_W), BF16),
                   jax.ShapeDtypeStruct((b_, MLA_HEADS * HEAD_PAD, s_), BF16),
                   jax.ShapeDtypeStruct((b_, s_, MLA_HEADS * HEAD_PAD), BF16),
                   jax.ShapeDtypeStruct((b_, MLA_HEADS * V_AUG, s_), BF16)],
        compiler_params=pltpu.CompilerParams(dimension_semantics=("parallel", "parallel"),
                                             vmem_limit_bytes=VMEM_LIMIT),
        name="proj",
    )(x, posr, *consts)

    gtok = lambda w: pl.BlockSpec((1, tg, w), lambda b, i: (b, i, 0))
    tpos = jnp.arange(CHUNK, dtype=jnp.int32)
    tri = (tpos[:, None] >= tpos[None, :]).astype(BF16)
    o_gla = pl.pallas_call(
        functools.partial(_gla_kernel, n_chunks=tg // CHUNK),
        grid=(b_, s_ // tg),
        in_specs=[gtok(2 * GLA_QK_W), gtok(GLA_QK_W), gtok(GLA_V_W), gtok(GLA_V_W),
                  _const_spec((1, GLA_DV)), _const_spec((CHUNK, CHUNK))],
        out_specs=gtok(GLA_V_W),
        out_shape=jax.ShapeDtypeStruct((b_, s_, GLA_V_W), BF16),
        scratch_shapes=[pltpu.VMEM((GLA_DV, GLA_QK_W), F32)],
        compiler_params=pltpu.CompilerParams(dimension_semantics=("parallel", "arbitrary"),
                                             vmem_limit_bytes=VMEM_LIMIT),
        name="gla",
    )(zqk, loga, zv, gate, row(gla_out_norm), tri)

    o_t = pl.pallas_call(
        functools.partial(_attn_kernel, tq=tq, tk=tk),
        grid=(b_, MLA_HEADS, s_ // tq),
        in_specs=[pl.BlockSpec((1, HEAD_PAD, tq), lambda b, h, i: (b, h, i)),
                  pl.BlockSpec((1, s_, HEAD_PAD), lambda b, h, i: (b, 0, h)),
                  pl.BlockSpec((1, V_AUG, s_), lambda b, h, i: (b, h, 0))],
        out_specs=pl.BlockSpec((1, MLA_V, tq), lambda b, h, i: (b, h, i)),
        out_shape=jax.ShapeDtypeStruct((b_, MLA_HEADS * MLA_V, s_), BF16),
        scratch_shapes=[pltpu.VMEM((4, tk, tq), F32), pltpu.VMEM((4, tk, tq), BF16),
                        pltpu.VMEM((V_AUG, tq), F32), pltpu.VMEM((1, tq), F32),
                        pltpu.VMEM((4, 1, tq), F32), pltpu.VMEM((4, 1, tq), F32)],
        compiler_params=pltpu.CompilerParams(
            dimension_semantics=("parallel", "parallel", "arbitrary"),
            vmem_limit_bytes=VMEM_LIMIT),
        name="attn",
    )(qt, kk, vt)

    y = pl.pallas_call(
        functools.partial(_mlp_kernel, ff_block=ff_block),
        grid=(b_, s_ // tm),
        in_specs=[tok(d_model), tok(GLA_V_W), tok_t(MLA_HEADS * MLA_V),
                  _const_spec(w_og.shape), _const_spec(w_om.shape), _const_spec((1, d_model)),
                  _const_spec(w_up.shape), _const_spec(w_down.shape)],
        out_specs=tok(d_model),
        out_shape=jax.ShapeDtypeStruct((b_, s_, d_model), x.dtype),
        compiler_params=pltpu.CompilerParams(dimension_semantics=("parallel", "parallel"),
                                             vmem_limit_bytes=VMEM_LIMIT),
        name="mlp",
    )(x, o_gla, o_t, w_og, w_om, row(mlp_norm), w_up.astype(BF16), w_down.astype(BF16))
    return y


def kernel(x, positions, attn_norm, w_in, w_gate_up, b_gate, gla_out_norm, q_a_norm, w_uq,
           kv_a_norm, w_ukv, q_head_norm, k_head_norm, w_out, mlp_norm, w_up, w_down):
    b_, s_, _ = x.shape
    posr = positions.reshape(b_, 1, s_)
    inv_freq = ROPE_BASE ** (-jnp.arange(0, MLA_ROPE, 2, dtype=F32) / MLA_ROPE)
    invf_col = inv_freq.reshape(MLA_HALF, 1)
    i = jnp.arange(MLA_HALF)
    sel = jnp.zeros((MLA_ROPE, 3 * LANES), F32)
    sel = sel.at[i, MISC_PE + i].set(1.0).at[i, MISC_PE + MLA_HALF + i].set(1.0)
    sel = sel.at[MLA_HALF + i, LANES + MISC_PE + i].set(-1.0)
    sel = sel.at[MLA_HALF + i, 2 * LANES + MISC_PE + MLA_HALF + i].set(1.0)
    rope_sel = jnp.tile(sel, (3, 1)).astype(BF16)
    for l in range(attn_norm.shape[0]):
        x = _layer(x, posr, invf_col, rope_sel, attn_norm[l], w_in[l], w_gate_up[l],
                   b_gate[l], gla_out_norm[l], q_a_norm[l], w_uq[l], kv_a_norm[l], w_ukv[l],
                   q_head_norm[l], k_head_norm[l], w_out[l], mlp_norm[l], w_up[l], w_down[l])
    return x
```

```python
import functools
import math

import jax
import jax.numpy as jnp
import numpy as np
from jax import lax
from jax.experimental import pallas as pl
from jax.experimental.pallas import tpu as pltpu

F32 = jnp.float32
BF16 = jnp.bfloat16
HIGHEST = lax.Precision.HIGHEST

CHUNK = 64
EPS = 1e-6

GLA_HEADS = 4
GLA_DK = 64
GLA_DV = 128
GLA_GATE_RANK = 16
GLA_GATE_NORMALIZER = 16.0
GLA_LOG_GATE_MIN = -1.0
GLA_QK_W = GLA_HEADS * GLA_DK
GLA_V_W = GLA_HEADS * GLA_DV

MLA_HEADS = 8
MLA_Q_RANK = 256
MLA_KV_RANK = 128
MLA_NOPE = 64
MLA_ROPE = 32
MLA_HALF = MLA_ROPE // 2
MLA_QK = MLA_NOPE + MLA_ROPE
MLA_V = 64
ROPE_BASE = 10000.0

LANES = 128
HEAD_PAD = LANES
BF16_SUBLANES = 16
V_AUG = MLA_V + BF16_SUBLANES
LOG2_E = math.log2(math.e)
NEG = -1e30

C_Q = 0
C_K = C_Q + GLA_QK_W
C_V = C_K + GLA_QK_W
C_G = C_V + GLA_V_W
C_CQ = C_G + GLA_V_W
C_CKV = C_CQ + MLA_Q_RANK
C_MISC = C_CKV + MLA_KV_RANK
D_PROJ_PAD = C_MISC + LANES
MISC_GATE = 0
MISC_PE = MLA_NOPE

TOKEN_BLOCK = 512
GLA_BLOCK = 512
ATTN_TQ = 2048
ATTN_TK = 256
QUERY_GROUP = 256
FF_BLOCK = 1024
VMEM_LIMIT = 56 * 1024 * 1024


def _nt(a, b):
    return lax.dot_general(a, b, (((1,), (1,)), ((), ())), preferred_element_type=F32)


def _tn(a, b):
    return lax.dot_general(a, b, (((0,), (0,)), ((), ())), preferred_element_type=F32)


def _rms(v):
    return v * lax.rsqrt(jnp.mean(v * v, axis=-1, keepdims=True) + EPS)


def _proj_kernel(x_ref, posr_ref, g_attn_ref, w_in_ref, w_gate_ref, b_gate_ref,
                 qa_g_ref, w_uqt_ref, kva_g_ref, w_k_ref, w_vt_ref,
                 qg_col_ref, kg_nope_ref, kg_rope_ref, invf_col_ref, rope_sel_ref,
                 zqk_ref, loga_ref, zv_ref, gate_ref, qt_ref, k_ref, vt_ref):
    hb = (_rms(x_ref[0]) * g_attn_ref[...]).astype(BF16)

    def in_proj(c0, c1):
        return jnp.dot(hb, w_in_ref[:, c0:c1], preferred_element_type=F32)

    z_mla = in_proj(C_CQ, D_PROJ_PAD)
    zcq = z_mla[:, :C_CKV - C_CQ]
    zckv = z_mla[:, C_CKV - C_CQ:C_MISC - C_CQ]
    misc = z_mla[:, C_MISC - C_CQ:]

    cq = (_rms(zcq) * qa_g_ref[...]).astype(BF16)
    qt = _nt(w_uqt_ref[...], cq)
    ang_t = invf_col_ref[...] * posr_ref[0].astype(F32)
    cos_t = jnp.cos(ang_t)
    sin_t = jnp.sin(ang_t)
    qscale = MLA_QK ** -0.5 * LOG2_E
    r1 = MLA_NOPE
    r2 = MLA_NOPE + MLA_HALF
    r3 = MLA_QK

    def q_head(hh):
        base = hh * HEAD_PAD
        blk = qt[base:base + HEAD_PAD, :]
        ssq = jnp.sum(blk * blk, axis=0, keepdims=True)
        rs = lax.rsqrt(ssq * (1.0 / MLA_QK) + EPS) * qscale
        qn = blk * rs * qg_col_ref[...]
        x1 = qn[r1:r2, :]
        x2 = qn[r2:r3, :]
        qt_ref[0, base:base + r1, :] = qn[:r1, :].astype(BF16)
        qt_ref[0, base + r1:base + r2, :] = (x1 * cos_t - x2 * sin_t).astype(BF16)
        qt_ref[0, base + r2:base + r3, :] = (x1 * sin_t + x2 * cos_t).astype(BF16)
        qt_ref[0, base + r3:base + HEAD_PAD, :] = jnp.zeros((HEAD_PAD - r3, blk.shape[1]), BF16)

    zqk_ref[0] = in_proj(C_Q, C_V)
    for hh in range(0, MLA_HEADS // 2):
        q_head(hh)
    zv_ref[0] = in_proj(C_V, C_G).astype(BF16)
    for hh in range(MLA_HEADS // 2, MLA_HEADS):
        q_head(hh)
    zg = in_proj(C_G, C_CQ)
    gate_ref[0] = (zg * jax.nn.sigmoid(zg)).astype(BF16)

    logit = jnp.dot(misc, w_gate_ref[...], preferred_element_type=F32, precision=HIGHEST)
    logit = logit + b_gate_ref[...]
    log_sig = jnp.minimum(logit, 0.0) - jnp.log1p(jnp.exp(-jnp.abs(logit)))
    loga_ref[0] = jnp.maximum(log_sig / GLA_GATE_NORMALIZER, GLA_LOG_GATE_MIN)

    ckv = (_rms(zckv) * kva_g_ref[...]).astype(BF16)
    vt = _nt(w_vt_ref[...], ckv).astype(BF16)
    ones_rows = jnp.ones((V_AUG - MLA_V, vt.shape[1]), BF16)
    for hh in range(MLA_HEADS):
        vt_ref[0, hh * V_AUG:hh * V_AUG + MLA_V, :] = vt[hh * MLA_V:(hh + 1) * MLA_V, :]
        vt_ref[0, hh * V_AUG + MLA_V:(hh + 1) * V_AUG, :] = ones_rows
    kn = jnp.dot(ckv, w_k_ref[...], preferred_element_type=F32)

    lane = lax.broadcasted_iota(jnp.int32, (1, LANES), 1)
    is_lo = (lane >= MISC_PE) & (lane < MISC_PE + MLA_HALF)
    is_hi = (lane >= MISC_PE + MLA_HALF) & (lane < MISC_PE + MLA_ROPE)
    kpe = jnp.where(is_lo | is_hi, misc, 0.0)
    ssq_pe = jnp.sum(kpe * kpe, axis=-1, keepdims=True)
    xg = kpe * kg_rope_ref[...]
    cs3 = jnp.concatenate(_split3(jnp.concatenate([cos_t, sin_t], axis=0)), axis=0)
    tabs = _tn(cs3, rope_sel_ref[...])
    rot = (xg * tabs[:, :LANES]
           + pltpu.roll(xg, LANES - MLA_HALF, 1) * tabs[:, LANES:2 * LANES]
           + pltpu.roll(xg, MLA_HALF, 1) * tabs[:, 2 * LANES:])
    def k_head(hh):
        base = hh * HEAD_PAD
        kn_h = kn[:, base:base + HEAD_PAD]
        ssq = jnp.sum(kn_h * kn_h, axis=-1, keepdims=True) + ssq_pe
        rs = lax.rsqrt(ssq * (1.0 / MLA_QK) + EPS)
        k_ref[0, :, base:base + HEAD_PAD] = ((kn_h * kg_nope_ref[...] + rot) * rs).astype(BF16)

    for hh in range(MLA_HEADS):
        k_head(hh)


def _split3(v):
    hi = v.astype(BF16)
    r1 = v - hi.astype(F32)
    mid = r1.astype(BF16)
    lo = (r1 - mid.astype(F32)).astype(BF16)
    return hi, mid, lo


def _gla_kernel(zqk_ref, loga_ref, zv_ref, gate_ref, gout_ref, tri_ref, o_ref, st_ref, *,
                n_chunks):
    @pl.when(pl.program_id(1) == 0)
    def _():
        st_ref[...] = jnp.zeros(st_ref.shape, F32)

    sr = lax.broadcasted_iota(jnp.int32, (GLA_HEADS * CHUNK, CHUNK), 0)
    sc_ = lax.broadcasted_iota(jnp.int32, (GLA_HEADS * CHUNK, CHUNK), 1)
    causal4 = (sr % CHUNK) >= sc_
    qr = lax.broadcasted_iota(jnp.int32, (GLA_HEADS * CHUNK, GLA_QK_W), 0)
    qc = lax.broadcasted_iota(jnp.int32, (GLA_HEADS * CHUNK, GLA_QK_W), 1)
    head_sel = (qr // CHUNK) == (qc // GLA_DK)
    lane_head = lax.broadcasted_iota(jnp.int32, (GLA_DV, GLA_QK_W), 1) // GLA_DK
    gout = gout_ref[...]
    chunks = [slice(ci * CHUNK, (ci + 1) * CHUNK) for ci in range(n_chunks)]

    la = loga_ref[0]
    parts = [p for r in chunks for p in _split3(la[r])]
    tri_out = jnp.dot(tri_ref[...], jnp.concatenate(parts, axis=1), preferred_element_type=F32)
    w = GLA_QK_W
    cum = [tri_out[:, (3 * ci) * w:(3 * ci + 1) * w] + tri_out[:, (3 * ci + 1) * w:(3 * ci + 2) * w]
           + tri_out[:, (3 * ci + 2) * w:(3 * ci + 3) * w] for ci in range(n_chunks)]
    cl = [c[CHUNK - 1:CHUNK, :] for c in cum]

    def prep(ci):
        r = chunks[ci]
        q = zqk_ref[0, r, 0:GLA_QK_W]
        k = zqk_ref[0, r, GLA_QK_W:2 * GLA_QK_W]
        qd = (q * (GLA_DK ** -0.5) * jnp.exp(cum[ci])).astype(BF16)
        k_inv = (k * jnp.exp(-cum[ci])).astype(BF16)
        k_end = (k * jnp.exp(cl[ci] - cum[ci])).astype(BF16)
        q_stack = jnp.where(head_sel, jnp.concatenate([qd] * GLA_HEADS, axis=0), 0.0)
        return q_stack, k_inv, k_end

    def intra_scores(ci):
        return jnp.where(causal4, _nt(pre[ci][0], pre[ci][1]), 0.0).astype(BF16)

    def intra_out_and_kv(ci):
        v = zv_ref[0, chunks[ci], :]
        full = [jnp.dot(scores[ci][hh * CHUNK:(hh + 1) * CHUNK, :],
                        v[:, hh * GLA_DV:(hh + 1) * GLA_DV], preferred_element_type=F32)
                for hh in range(GLA_HEADS)]
        kv_t = _tn(v, pre[ci][2])
        kv = kv_t[(GLA_HEADS - 1) * GLA_DV:, :]
        for hh in range(GLA_HEADS - 2, -1, -1):
            kv = jnp.where(lane_head == hh, kv_t[hh * GLA_DV:(hh + 1) * GLA_DV, :], kv)
        return full, kv

    pre, scores, intra = {}, {}, {}
    for t in range(n_chunks + 2):
        if t < n_chunks:
            pre[t] = prep(t)
        if 0 <= t - 1 < n_chunks:
            scores[t - 1] = intra_scores(t - 1)
        if 0 <= t - 2 < n_chunks:
            intra[t - 2] = intra_out_and_kv(t - 2)

    st = st_ref[...]
    st_in = []
    for ci in range(n_chunks):
        st_in.append(st.astype(BF16))
        st = st * jnp.exp(cl[ci]) + intra[ci][1]
    st_ref[...] = st

    def finish(ci, inter):
        r = chunks[ci]
        for hh in range(GLA_HEADS):
            cols = slice(hh * GLA_DV, (hh + 1) * GLA_DV)
            o_h = inter[hh * CHUNK:(hh + 1) * CHUNK, :] + intra[ci][0][hh]
            g_h = gate_ref[0, r, cols].astype(F32)
            o_ref[0, r, cols] = (_rms(o_h) * gout * g_h).astype(BF16)

    inter = {}
    for t in range(n_chunks + 1):
        if t < n_chunks:
            inter[t] = _nt(pre[t][0], st_in[t])
        if t >= 1:
            finish(t - 1, inter[t - 1])


def _attn_kernel(qt_ref, k_ref, vt_ref, o_ref, s_ref, acc_ref, m_ref, bm_ref, *, tq, tk):
    qi = pl.program_id(2)
    ndiag = tq // tk

    m_ref[...] = jnp.full(m_ref.shape, NEG, F32)
    acc_ref[...] = jnp.zeros(acc_ref.shape, F32)

    kr = lax.broadcasted_iota(jnp.int32, (tk, tk), 0)
    qc = lax.broadcasted_iota(jnp.int32, (tk, tk), 1)
    square_mask = (kr // CHUNK) <= (qc // CHUNK)

    def qk(j, slot, c0=0, c1=tq):
        start = pl.multiple_of(j * tk, tk)
        s = jnp.dot(k_ref[0, pl.ds(start, tk), :], qt_ref[0, :, c0:c1],
                    preferred_element_type=F32)
        s_ref[slot, :, c0:c1] = s
        bm_ref[slot, :, c0:c1] = jnp.max(s, axis=0, keepdims=True)

    def mask_square(slot, c0):
        cols = slice(c0, c0 + tk)
        s = jnp.where(square_mask, s_ref[slot, :, cols], NEG)
        s_ref[slot, :, cols] = s
        bm_ref[slot, :, cols] = jnp.max(s, axis=0, keepdims=True)

    def softmax_pv(j, slots, c0=0, c1=tq):
        m_old = m_ref[:, c0:c1]
        m_new = m_old
        for slot in slots:
            m_new = jnp.maximum(m_new, bm_ref[slot, :, c0:c1])
        m_ref[:, c0:c1] = m_new
        p = jnp.concatenate([jnp.exp2(s_ref[slot, :, c0:c1] - m_new).astype(BF16)
                             for slot in slots], axis=0)
        start = pl.multiple_of(j * tk, tk)
        pv = jnp.dot(vt_ref[0, :, pl.ds(start, len(slots) * tk)], p,
                     preferred_element_type=F32)
        acc_ref[:, c0:c1] = jnp.exp2(m_old - m_new) * acc_ref[:, c0:c1] + pv

    qk(0, 0)
    qk(1, 1)

    def body(i, carry):
        for half in range(2):
            j = 4 * i + 2 * half
            cur = (2 * half, 2 * half + 1)
            nxt = (2 - 2 * half, 3 - 2 * half)
            for g0 in range(0, tq, QUERY_GROUP):
                g1 = g0 + QUERY_GROUP
                qk(j + 2, nxt[0], g0, g1)
                qk(j + 3, nxt[1], g0, g1)
                softmax_pv(j, cur, g0, g1)
        return carry

    lax.fori_loop(0, qi * (ndiag // 4), body, 0)

    nfull = qi * ndiag
    for d in range(0, ndiag, 2):
        for e in (d + 2, d + 3):
            if e < ndiag:
                qk(nfull + e, e % 4, e * tk)
        mask_square(d % 4, d * tk)
        mask_square((d + 1) % 4, (d + 1) * tk)
        softmax_pv(nfull + d, (d % 4,), d * tk, (d + 1) * tk)
        softmax_pv(nfull + d, (d % 4, (d + 1) % 4), (d + 1) * tk, tq)

    acc = acc_ref[...]
    o_ref[0] = (acc[:MLA_V, :] / acc[MLA_V:MLA_V + 1, :]).astype(BF16)


def _mlp_kernel(x_ref, og_ref, ot_ref, w_og_ref, w_om_ref, g_mlp_ref, w_up_ref, w_dn_ref,
                y_ref, *, ff_block):
    mix = jnp.dot(og_ref[0], w_og_ref[...], preferred_element_type=F32)
    mix = mix + _tn(ot_ref[0], w_om_ref[...])
    x1 = x_ref[0] + mix
    h = (_rms(x1) * g_mlp_ref[...]).astype(BF16)
    acc = x1
    d_ff = w_up_ref.shape[1]
    for f in range(d_ff // ff_block):
        cols = slice(f * ff_block, (f + 1) * ff_block)
        u = jnp.dot(h, w_up_ref[:, cols], preferred_element_type=F32)
        a = jnp.square(jnp.maximum(u, 0.0)).astype(BF16)
        acc = acc + jnp.dot(a, w_dn_ref[cols, :], preferred_element_type=F32)
    y_ref[0] = acc


def _const_spec(shape):
    nd = len(shape)
    return pl.BlockSpec(shape, lambda *_: (0,) * nd)


def _pack_weights(w_in, w_gate_up, b_gate, w_uq, w_ukv, q_head_norm, k_head_norm, w_out):
    d_model = w_in.shape[0]
    o = 0
    parts = {}
    for name, width in (("q", GLA_QK_W), ("k", GLA_QK_W), ("v", GLA_V_W), ("gate", GLA_GATE_RANK),
                        ("g", GLA_V_W), ("cq", MLA_Q_RANK), ("ckv", MLA_KV_RANK), ("pe", MLA_ROPE)):
        parts[name] = w_in[:, o:o + width]
        o += width
    gap = lambda n: jnp.zeros((d_model, n), w_in.dtype)
    assert MISC_GATE == 0
    w_in_p = jnp.concatenate(
        [parts["q"], parts["k"], parts["v"], parts["g"], parts["cq"], parts["ckv"],
         parts["gate"], gap(MISC_PE - GLA_GATE_RANK), parts["pe"],
         gap(LANES - MISC_PE - MLA_ROPE)], axis=1).astype(BF16)

    w_gate_p = jnp.pad(w_gate_up.astype(F32), ((0, LANES - GLA_GATE_RANK), (0, 0)))
    b_gate_p = b_gate.reshape(1, GLA_QK_W).astype(F32)

    w_uq_h = w_uq.reshape(MLA_Q_RANK, MLA_HEADS, MLA_QK)
    w_uq_h = jnp.pad(w_uq_h, ((0, 0), (0, 0), (0, HEAD_PAD - MLA_QK)))
    w_uqt = w_uq_h.reshape(MLA_Q_RANK, MLA_HEADS * HEAD_PAD).T.astype(BF16)

    w_ukv_h = w_ukv.reshape(MLA_KV_RANK, MLA_HEADS, MLA_NOPE + MLA_V)
    w_k = jnp.pad(w_ukv_h[:, :, :MLA_NOPE], ((0, 0), (0, 0), (0, HEAD_PAD - MLA_NOPE)))
    w_k = w_k.reshape(MLA_KV_RANK, MLA_HEADS * HEAD_PAD).astype(BF16)
    w_vt = w_ukv_h[:, :, MLA_NOPE:].reshape(MLA_KV_RANK, MLA_HEADS * MLA_V).T.astype(BF16)

    qg_col = jnp.pad(q_head_norm, (0, HEAD_PAD - MLA_QK)).reshape(HEAD_PAD, 1).astype(F32)
    kg_nope = jnp.pad(k_head_norm[:MLA_NOPE], (0, LANES - MLA_NOPE)).reshape(1, LANES).astype(F32)
    kg_rope = jnp.pad(k_head_norm[MLA_NOPE:], (MISC_PE, LANES - MISC_PE - MLA_ROPE))
    kg_rope = kg_rope.reshape(1, LANES).astype(F32)

    w_og = w_out[:GLA_V_W].astype(BF16)
    w_om = w_out[GLA_V_W:].astype(BF16)
    return w_in_p, w_gate_p, b_gate_p, w_uqt, w_k, w_vt, qg_col, kg_nope, kg_rope, w_og, w_om


def _layer(x, posr, invf_col, rope_sel, attn_norm, w_in, w_gate_up, b_gate, gla_out_norm,
           q_a_norm, w_uq, kv_a_norm, w_ukv, q_head_norm, k_head_norm, w_out, mlp_norm, w_up,
           w_down):
    b_, s_, d_model = x.shape
    tm = min(TOKEN_BLOCK, s_)
    tg = min(GLA_BLOCK, s_)
    tq = min(ATTN_TQ, s_)
    tk = min(ATTN_TK, tq)
    assert s_ % tm == 0 and s_ % tg == 0 and s_ % tq == 0 and tk % CHUNK == 0
    assert tq % (4 * tk) == 0
    d_ff = w_up.shape[1]
    ff_block = min(FF_BLOCK, d_ff)

    (w_in_p, w_gate_p, b_gate_p, w_uqt, w_k, w_vt, qg_col, kg_nope, kg_rope, w_og,
     w_om) = _pack_weights(w_in, w_gate_up, b_gate, w_uq, w_ukv, q_head_norm, k_head_norm, w_out)
    row = lambda v: v.reshape(1, -1).astype(F32)

    tok = lambda w: pl.BlockSpec((1, tm, w), lambda b, i: (b, i, 0))
    tok_t = lambda hgt: pl.BlockSpec((1, hgt, tm), lambda b, i: (b, 0, i))
    consts = (row(attn_norm), w_in_p, w_gate_p, b_gate_p, row(q_a_norm), w_uqt, row(kv_a_norm),
              w_k, w_vt, qg_col, kg_nope, kg_rope, invf_col, rope_sel)
    zqk, loga, zv, gate, qt, kk, vt = pl.pallas_call(
        _proj_kernel,
        grid=(b_, s_ // tm),
        in_specs=[tok(d_model), tok_t(1)] + [_const_spec(c.shape) for c in consts],
        out_specs=[tok(2 * GLA_QK_W), tok(GLA_QK_W), tok(GLA_V_W), tok(GLA_V_W),
                   tok_t(MLA_HEADS * HEAD_PAD), tok(MLA_HEADS * HEAD_PAD), tok_t(MLA_HEADS * V_AUG)],
        out_shape=[jax.ShapeDtypeStruct((b_, s_, 2 * GLA_QK_W), F32),
                   jax.ShapeDtypeStruct((b_, s_, GLA_QK_W), F32),
                   jax.ShapeDtypeStruct((b_, s_, GLA_V_W), BF16),
                   jax.ShapeDtypeStruct((b_, s_, GLA_V_W), BF16),
                   jax.ShapeDtypeStruct((b_, MLA_HEADS * HEAD_PAD, s_), BF16),
                   jax.ShapeDtypeStruct((b_, s_, MLA_HEADS * HEAD_PAD), BF16),
                   jax.ShapeDtypeStruct((b_, MLA_HEADS * V_AUG, s_), BF16)],
        compiler_params=pltpu.CompilerParams(dimension_semantics=("parallel", "parallel"),
                                             vmem_limit_bytes=VMEM_LIMIT),
        name="proj",
    )(x, posr, *consts)

    gtok = lambda w: pl.BlockSpec((1, tg, w), lambda b, i: (b, i, 0))
    tri = jnp.asarray(np.tril(np.ones((CHUNK, CHUNK), np.float32)), BF16)
    o_gla = pl.pallas_call(
        functools.partial(_gla_kernel, n_chunks=tg // CHUNK),
        grid=(b_, s_ // tg),
        in_specs=[gtok(2 * GLA_QK_W), gtok(GLA_QK_W), gtok(GLA_V_W), gtok(GLA_V_W),
                  _const_spec((1, GLA_DV)), _const_spec((CHUNK, CHUNK))],
        out_specs=gtok(GLA_V_W),
        out_shape=jax.ShapeDtypeStruct((b_, s_, GLA_V_W), BF16),
        scratch_shapes=[pltpu.VMEM((GLA_DV, GLA_QK_W), F32)],
        compiler_params=pltpu.CompilerParams(dimension_semantics=("parallel", "arbitrary"),
                                             vmem_limit_bytes=VMEM_LIMIT),
        name="gla",
    )(zqk, loga, zv, gate, row(gla_out_norm), tri)

    o_t = pl.pallas_call(
        functools.partial(_attn_kernel, tq=tq, tk=tk),
        grid=(b_, MLA_HEADS, s_ // tq),
        in_specs=[pl.BlockSpec((1, HEAD_PAD, tq), lambda b, h, i: (b, h, i)),
                  pl.BlockSpec((1, s_, HEAD_PAD), lambda b, h, i: (b, 0, h)),
                  pl.BlockSpec((1, V_AUG, s_), lambda b, h, i: (b, h, 0))],
        out_specs=pl.BlockSpec((1, MLA_V, tq), lambda b, h, i: (b, h, i)),
        out_shape=jax.ShapeDtypeStruct((b_, MLA_HEADS * MLA_V, s_), BF16),
        scratch_shapes=[pltpu.VMEM((4, tk, tq), F32), pltpu.VMEM((V_AUG, tq), F32),
                        pltpu.VMEM((1, tq), F32), pltpu.VMEM((4, 1, tq), F32)],
        compiler_params=pltpu.CompilerParams(
            dimension_semantics=("parallel", "parallel", "arbitrary"),
            vmem_limit_bytes=VMEM_LIMIT),
        name="attn",
    )(qt, kk, vt)

    y = pl.pallas_call(
        functools.partial(_mlp_kernel, ff_block=ff_block),
        grid=(b_, s_ // tm),
        in_specs=[tok(d_model), tok(GLA_V_W), tok_t(MLA_HEADS * MLA_V),
                  _const_spec(w_og.shape), _const_spec(w_om.shape), _const_spec((1, d_model)),
                  _const_spec(w_up.shape), _const_spec(w_down.shape)],
        out_specs=tok(d_model),
        out_shape=jax.ShapeDtypeStruct((b_, s_, d_model), x.dtype),
        compiler_params=pltpu.CompilerParams(dimension_semantics=("parallel", "parallel"),
                                             vmem_limit_bytes=VMEM_LIMIT),
        name="mlp",
    )(x, o_gla, o_t, w_og, w_om, row(mlp_norm), w_up.astype(BF16), w_down.astype(BF16))
    return y


def kernel(x, positions, attn_norm, w_in, w_gate_up, b_gate, gla_out_norm, q_a_norm, w_uq,
           kv_a_norm, w_ukv, q_head_norm, k_head_norm, w_out, mlp_norm, w_up, w_down):
    b_, s_, _ = x.shape
    posr = positions.reshape(b_, 1, s_)
    inv_freq = ROPE_BASE ** (-jnp.arange(0, MLA_ROPE, 2, dtype=F32) / MLA_ROPE)
    invf_col = inv_freq.reshape(MLA_HALF, 1)
    i = np.arange(MLA_HALF)
    sel = np.zeros((MLA_ROPE, 3 * LANES), np.float32)
    sel[i, MISC_PE + i] = 1.0
    sel[i, MISC_PE + MLA_HALF + i] = 1.0
    sel[MLA_HALF + i, LANES + MISC_PE + i] = -1.0
    sel[MLA_HALF + i, 2 * LANES + MISC_PE + MLA_HALF + i] = 1.0
    rope_sel = jnp.asarray(np.tile(sel, (3, 1)), BF16)
    for l in range(attn_norm.shape[0]):
        x = _layer(x, posr, invf_col, rope_sel, attn_norm[l], w_in[l], w_gate_up[l],
                   b_gate[l], gla_out_norm[l], q_a_norm[l], w_uq[l], kv_a_norm[l], w_ukv[l],
                   q_head_norm[l], k_head_norm[l], w_out[l], mlp_norm[l], w_up[l], w_down[l])
    return x
```

```python
import functools
import math

import jax
import jax.numpy as jnp
import numpy as np
from jax import lax
from jax.experimental import pallas as pl
from jax.experimental.pallas import tpu as pltpu

F32 = jnp.float32
BF16 = jnp.bfloat16
HIGHEST = lax.Precision.HIGHEST

CHUNK = 64
EPS = 1e-6

GLA_HEADS = 4
GLA_DK = 64
GLA_DV = 128
GLA_GATE_RANK = 16
GLA_GATE_NORMALIZER = 16.0
GLA_LOG_GATE_MIN = -1.0
GLA_QK_W = GLA_HEADS * GLA_DK
GLA_V_W = GLA_HEADS * GLA_DV

MLA_HEADS = 8
MLA_Q_RANK = 256
MLA_KV_RANK = 128
MLA_NOPE = 64
MLA_ROPE = 32
MLA_HALF = MLA_ROPE // 2
MLA_QK = MLA_NOPE + MLA_ROPE
MLA_V = 64
ROPE_BASE = 10000.0

LANES = 128
HEAD_PAD = LANES
BF16_SUBLANES = 16
V_AUG = MLA_V + BF16_SUBLANES
LOG2_E = math.log2(math.e)
NEG = -1e30

C_Q = 0
C_K = C_Q + GLA_QK_W
C_V = C_K + GLA_QK_W
C_G = C_V + GLA_V_W
C_CQ = C_G + GLA_V_W
C_CKV = C_CQ + MLA_Q_RANK
C_MISC = C_CKV + MLA_KV_RANK
D_PROJ_PAD = C_MISC + LANES
MISC_GATE = 0
MISC_PE = MLA_NOPE

TOKEN_BLOCK = 512
GLA_BLOCK = 512
ATTN_TQ = 2048
ATTN_TK = 256
QUERY_GROUP = 256
QK_LOOKAHEAD = 3
FF_BLOCK = 1024
VMEM_LIMIT = 56 * 1024 * 1024


def _nt(a, b):
    return lax.dot_general(a, b, (((1,), (1,)), ((), ())), preferred_element_type=F32)


def _tn(a, b):
    return lax.dot_general(a, b, (((0,), (0,)), ((), ())), preferred_element_type=F32)


def _rms(v):
    return v * lax.rsqrt(jnp.mean(v * v, axis=-1, keepdims=True) + EPS)


def _proj_kernel(x_ref, posr_ref, g_attn_ref, w_in_ref, w_gate_ref, b_gate_ref,
                 qa_g_ref, w_uqt_ref, kva_g_ref, w_k_ref, w_vt_ref,
                 qg_col_ref, kg_nope_ref, kg_rope_ref, invf_col_ref, rope_sel_ref,
                 zqk_ref, loga_ref, zv_ref, gate_ref, qt_ref, k_ref, vt_ref):
    hb = (_rms(x_ref[0]) * g_attn_ref[...]).astype(BF16)

    def in_proj(c0, c1):
        return jnp.dot(hb, w_in_ref[:, c0:c1], preferred_element_type=F32)

    z_mla = in_proj(C_CQ, D_PROJ_PAD)
    zcq = z_mla[:, :C_CKV - C_CQ]
    zckv = z_mla[:, C_CKV - C_CQ:C_MISC - C_CQ]
    misc = z_mla[:, C_MISC - C_CQ:]

    cq = (_rms(zcq) * qa_g_ref[...]).astype(BF16)
    qt = _nt(w_uqt_ref[...], cq)
    ang_t = invf_col_ref[...] * posr_ref[0].astype(F32)
    cos_t = jnp.cos(ang_t)
    sin_t = jnp.sin(ang_t)
    qscale = MLA_QK ** -0.5 * LOG2_E
    r1 = MLA_NOPE
    r2 = MLA_NOPE + MLA_HALF
    r3 = MLA_QK

    def q_head(hh):
        base = hh * HEAD_PAD
        blk = qt[base:base + HEAD_PAD, :]
        ssq = jnp.sum(blk * blk, axis=0, keepdims=True)
        rs = lax.rsqrt(ssq * (1.0 / MLA_QK) + EPS) * qscale
        qn = blk * rs * qg_col_ref[...]
        x1 = qn[r1:r2, :]
        x2 = qn[r2:r3, :]
        qt_ref[0, base:base + r1, :] = qn[:r1, :].astype(BF16)
        qt_ref[0, base + r1:base + r2, :] = (x1 * cos_t - x2 * sin_t).astype(BF16)
        qt_ref[0, base + r2:base + r3, :] = (x1 * sin_t + x2 * cos_t).astype(BF16)
        qt_ref[0, base + r3:base + HEAD_PAD, :] = jnp.zeros((HEAD_PAD - r3, blk.shape[1]), BF16)

    zqk_ref[0] = in_proj(C_Q, C_V)
    for hh in range(0, MLA_HEADS // 2):
        q_head(hh)
    zv_ref[0] = in_proj(C_V, C_G).astype(BF16)
    for hh in range(MLA_HEADS // 2, MLA_HEADS):
        q_head(hh)
    zg = in_proj(C_G, C_CQ)
    gate_ref[0] = (zg * jax.nn.sigmoid(zg)).astype(BF16)

    logit = jnp.dot(misc, w_gate_ref[...], preferred_element_type=F32, precision=HIGHEST)
    logit = logit + b_gate_ref[...]
    log_sig = jnp.minimum(logit, 0.0) - jnp.log1p(jnp.exp(-jnp.abs(logit)))
    loga_ref[0] = jnp.maximum(log_sig / GLA_GATE_NORMALIZER, GLA_LOG_GATE_MIN)

    ckv = (_rms(zckv) * kva_g_ref[...]).astype(BF16)
    vt = _nt(w_vt_ref[...], ckv).astype(BF16)
    ones_rows = jnp.ones((V_AUG - MLA_V, vt.shape[1]), BF16)
    for hh in range(MLA_HEADS):
        vt_ref[0, hh * V_AUG:hh * V_AUG + MLA_V, :] = vt[hh * MLA_V:(hh + 1) * MLA_V, :]
        vt_ref[0, hh * V_AUG + MLA_V:(hh + 1) * V_AUG, :] = ones_rows
    kn = jnp.dot(ckv, w_k_ref[...], preferred_element_type=F32)

    lane = lax.broadcasted_iota(jnp.int32, (1, LANES), 1)
    is_lo = (lane >= MISC_PE) & (lane < MISC_PE + MLA_HALF)
    is_hi = (lane >= MISC_PE + MLA_HALF) & (lane < MISC_PE + MLA_ROPE)
    kpe = jnp.where(is_lo | is_hi, misc, 0.0)
    ssq_pe = jnp.sum(kpe * kpe, axis=-1, keepdims=True)
    xg = kpe * kg_rope_ref[...]
    cs3 = jnp.concatenate(_split3(jnp.concatenate([cos_t, sin_t], axis=0)), axis=0)
    tabs = _tn(cs3, rope_sel_ref[...])
    rot = (xg * tabs[:, :LANES]
           + pltpu.roll(xg, LANES - MLA_HALF, 1) * tabs[:, LANES:2 * LANES]
           + pltpu.roll(xg, MLA_HALF, 1) * tabs[:, 2 * LANES:])
    def k_head(hh):
        base = hh * HEAD_PAD
        kn_h = kn[:, base:base + HEAD_PAD]
        ssq = jnp.sum(kn_h * kn_h, axis=-1, keepdims=True) + ssq_pe
        rs = lax.rsqrt(ssq * (1.0 / MLA_QK) + EPS)
        k_ref[0, :, base:base + HEAD_PAD] = ((kn_h * kg_nope_ref[...] + rot) * rs).astype(BF16)

    for hh in range(MLA_HEADS):
        k_head(hh)


def _split3(v):
    hi = v.astype(BF16)
    r1 = v - hi.astype(F32)
    mid = r1.astype(BF16)
    lo = (r1 - mid.astype(F32)).astype(BF16)
    return hi, mid, lo


def _gla_kernel(zqk_ref, loga_ref, zv_ref, gate_ref, gout_ref, tri_ref, o_ref, st_ref, *,
                n_chunks):
    @pl.when(pl.program_id(1) == 0)
    def _():
        st_ref[...] = jnp.zeros(st_ref.shape, F32)

    sr = lax.broadcasted_iota(jnp.int32, (GLA_HEADS * CHUNK, CHUNK), 0)
    sc_ = lax.broadcasted_iota(jnp.int32, (GLA_HEADS * CHUNK, CHUNK), 1)
    causal4 = (sr % CHUNK) >= sc_
    qr = lax.broadcasted_iota(jnp.int32, (GLA_HEADS * CHUNK, GLA_QK_W), 0)
    qc = lax.broadcasted_iota(jnp.int32, (GLA_HEADS * CHUNK, GLA_QK_W), 1)
    head_sel = (qr // CHUNK) == (qc // GLA_DK)
    lane_head = lax.broadcasted_iota(jnp.int32, (GLA_DV, GLA_QK_W), 1) // GLA_DK
    gout = gout_ref[...]
    chunks = [slice(ci * CHUNK, (ci + 1) * CHUNK) for ci in range(n_chunks)]

    la = loga_ref[0]
    parts = [p for r in chunks for p in _split3(la[r])]
    tri_out = jnp.dot(tri_ref[...], jnp.concatenate(parts, axis=1), preferred_element_type=F32)
    w = GLA_QK_W
    cum = [tri_out[:, (3 * ci) * w:(3 * ci + 1) * w] + tri_out[:, (3 * ci + 1) * w:(3 * ci + 2) * w]
           + tri_out[:, (3 * ci + 2) * w:(3 * ci + 3) * w] for ci in range(n_chunks)]
    cl = [c[CHUNK - 1:CHUNK, :] for c in cum]

    def prep(ci):
        r = chunks[ci]
        q = zqk_ref[0, r, 0:GLA_QK_W]
        k = zqk_ref[0, r, GLA_QK_W:2 * GLA_QK_W]
        qd = (q * (GLA_DK ** -0.5) * jnp.exp(cum[ci])).astype(BF16)
        k_inv = (k * jnp.exp(-cum[ci])).astype(BF16)
        k_end = (k * jnp.exp(cl[ci] - cum[ci])).astype(BF16)
        q_stack = jnp.where(head_sel, jnp.concatenate([qd] * GLA_HEADS, axis=0), 0.0)
        return q_stack, k_inv, k_end

    def intra_scores(ci):
        return jnp.where(causal4, _nt(pre[ci][0], pre[ci][1]), 0.0).astype(BF16)

    def intra_out_and_kv(ci):
        v = zv_ref[0, chunks[ci], :]
        full = [jnp.dot(scores[ci][hh * CHUNK:(hh + 1) * CHUNK, :],
                        v[:, hh * GLA_DV:(hh + 1) * GLA_DV], preferred_element_type=F32)
                for hh in range(GLA_HEADS)]
        kv_t = _tn(v, pre[ci][2])
        kv = kv_t[(GLA_HEADS - 1) * GLA_DV:, :]
        for hh in range(GLA_HEADS - 2, -1, -1):
            kv = jnp.where(lane_head == hh, kv_t[hh * GLA_DV:(hh + 1) * GLA_DV, :], kv)
        return full, kv

    pre, scores, intra = {}, {}, {}
    for t in range(n_chunks + 2):
        if t < n_chunks:
            pre[t] = prep(t)
        if 0 <= t - 1 < n_chunks:
            scores[t - 1] = intra_scores(t - 1)
        if 0 <= t - 2 < n_chunks:
            intra[t - 2] = intra_out_and_kv(t - 2)

    st = st_ref[...]
    st_in = []
    for ci in range(n_chunks):
        st_in.append(st.astype(BF16))
        st = st * jnp.exp(cl[ci]) + intra[ci][1]
    st_ref[...] = st

    def finish(ci, inter):
        r = chunks[ci]
        for hh in range(GLA_HEADS):
            cols = slice(hh * GLA_DV, (hh + 1) * GLA_DV)
            o_h = inter[hh * CHUNK:(hh + 1) * CHUNK, :] + intra[ci][0][hh]
            g_h = gate_ref[0, r, cols].astype(F32)
            o_ref[0, r, cols] = (_rms(o_h) * gout * g_h).astype(BF16)

    inter = {}
    for t in range(n_chunks + 1):
        if t < n_chunks:
            inter[t] = _nt(pre[t][0], st_in[t])
        if t >= 1:
            finish(t - 1, inter[t - 1])


def _attn_kernel(qt_ref, k_ref, vt_ref, o_ref, s_ref, acc_ref, m_ref, bm_ref, *, tq, tk):
    qi = pl.program_id(2)
    ndiag = tq // tk

    m_ref[...] = jnp.full(m_ref.shape, NEG, F32)
    acc_ref[...] = jnp.zeros(acc_ref.shape, F32)

    kr = lax.broadcasted_iota(jnp.int32, (tk, tk), 0)
    qc = lax.broadcasted_iota(jnp.int32, (tk, tk), 1)
    square_mask = (kr // CHUNK) <= (qc // CHUNK)

    def qk(j, slot, c0=0, c1=tq):
        start = pl.multiple_of(j * tk, tk)
        s = jnp.dot(k_ref[0, pl.ds(start, tk), :], qt_ref[0, :, c0:c1],
                    preferred_element_type=F32)
        s_ref[slot, :, c0:c1] = s
        bm_ref[slot, :, c0:c1] = jnp.max(s, axis=0, keepdims=True)

    def mask_square(slot, c0):
        cols = slice(c0, c0 + tk)
        s = jnp.where(square_mask, s_ref[slot, :, cols], NEG)
        s_ref[slot, :, cols] = s
        bm_ref[slot, :, cols] = jnp.max(s, axis=0, keepdims=True)

    def softmax_pv(j, slots, c0=0, c1=tq):
        m_old = m_ref[:, c0:c1]
        m_new = m_old
        for slot in slots:
            m_new = jnp.maximum(m_new, bm_ref[slot, :, c0:c1])
        m_ref[:, c0:c1] = m_new
        p = jnp.concatenate([jnp.exp2(s_ref[slot, :, c0:c1] - m_new).astype(BF16)
                             for slot in slots], axis=0)
        start = pl.multiple_of(j * tk, tk)
        pv = jnp.dot(vt_ref[0, :, pl.ds(start, len(slots) * tk)], p,
                     preferred_element_type=F32)
        acc_ref[:, c0:c1] = jnp.exp2(m_old - m_new) * acc_ref[:, c0:c1] + pv

    ngroups = tq // QUERY_GROUP
    per_iter = 2 * ngroups

    def cols(g):
        return g * QUERY_GROUP, (g + 1) * QUERY_GROUP

    def qk_item(j, ps, g, both=True):
        qk(j, 2 * ps, *cols(g))
        if both:
            qk(j + 1, 2 * ps + 1, *cols(g))

    for u in range(QK_LOOKAHEAD):
        qk_item(0, 0, u)

    def body(i, carry):
        for u in range(per_iter):
            v = u + QK_LOOKAHEAD
            if v < per_iter:
                qk_item(4 * i + 2 * (v // ngroups), v // ngroups, v % ngroups)
            else:
                qk_item(4 * i + 4, 0, v - per_iter)
            ps, g = divmod(u, ngroups)
            softmax_pv(4 * i + 2 * ps, (2 * ps, 2 * ps + 1), *cols(g))
        return carry

    lax.fori_loop(0, qi * (ndiag // 4), body, 0)

    nfull = qi * ndiag
    items = [(p, g) for p in range(ndiag // 2) for g in range(2 * p, ngroups)]
    for t, (p, g) in enumerate(items):
        if t + QK_LOOKAHEAD < len(items):
            p2, g2 = items[t + QK_LOOKAHEAD]
            qk_item(nfull + 2 * p2, p2 % 2, g2, both=(g2 != 2 * p2))
        d = 2 * p
        slots = (2 * (p % 2), 2 * (p % 2) + 1)
        if g == d:
            mask_square(slots[0], d * tk)
            softmax_pv(nfull + d, slots[:1], *cols(g))
        else:
            if g == d + 1:
                mask_square(slots[1], (d + 1) * tk)
            softmax_pv(nfull + d, slots, *cols(g))

    acc = acc_ref[...]
    o_ref[0] = (acc[:MLA_V, :] / acc[MLA_V:MLA_V + 1, :]).astype(BF16)


def _mlp_kernel(x_ref, og_ref, ot_ref, w_og_ref, w_om_ref, g_mlp_ref, w_up_ref, w_dn_ref,
                y_ref, *, ff_block):
    mix = jnp.dot(og_ref[0], w_og_ref[...], preferred_element_type=F32)
    mix = mix + _tn(ot_ref[0], w_om_ref[...])
    x1 = x_ref[0] + mix
    h = (_rms(x1) * g_mlp_ref[...]).astype(BF16)
    acc = x1
    d_ff = w_up_ref.shape[1]
    for f in range(d_ff // ff_block):
        cols = slice(f * ff_block, (f + 1) * ff_block)
        u = jnp.dot(h, w_up_ref[:, cols], preferred_element_type=F32)
        a = jnp.square(jnp.maximum(u, 0.0)).astype(BF16)
        acc = acc + jnp.dot(a, w_dn_ref[cols, :], preferred_element_type=F32)
    y_ref[0] = acc


def _const_spec(shape):
    nd = len(shape)
    return pl.BlockSpec(shape, lambda *_: (0,) * nd)


def _pack_weights(w_in, w_gate_up, b_gate, w_uq, w_ukv, q_head_norm, k_head_norm, w_out):
    d_model = w_in.shape[0]
    o = 0
    parts = {}
    for name, width in (("q", GLA_QK_W), ("k", GLA_QK_W), ("v", GLA_V_W), ("gate", GLA_GATE_RANK),
                        ("g", GLA_V_W), ("cq", MLA_Q_RANK), ("ckv", MLA_KV_RANK), ("pe", MLA_ROPE)):
        parts[name] = w_in[:, o:o + width]
        o += width
    gap = lambda n: jnp.zeros((d_model, n), w_in.dtype)
    assert MISC_GATE == 0
    w_in_p = jnp.concatenate(
        [parts["q"], parts["k"], parts["v"], parts["g"], parts["cq"], parts["ckv"],
         parts["gate"], gap(MISC_PE - GLA_GATE_RANK), parts["pe"],
         gap(LANES - MISC_PE - MLA_ROPE)], axis=1).astype(BF16)

    w_gate_p = jnp.pad(w_gate_up.astype(F32), ((0, LANES - GLA_GATE_RANK), (0, 0)))
    b_gate_p = b_gate.reshape(1, GLA_QK_W).astype(F32)

    w_uq_h = w_uq.reshape(MLA_Q_RANK, MLA_HEADS, MLA_QK)
    w_uq_h = jnp.pad(w_uq_h, ((0, 0), (0, 0), (0, HEAD_PAD - MLA_QK)))
    w_uqt = w_uq_h.reshape(MLA_Q_RANK, MLA_HEADS * HEAD_PAD).T.astype(BF16)

    w_ukv_h = w_ukv.reshape(MLA_KV_RANK, MLA_HEADS, MLA_NOPE + MLA_V)
    w_k = jnp.pad(w_ukv_h[:, :, :MLA_NOPE], ((0, 0), (0, 0), (0, HEAD_PAD - MLA_NOPE)))
    w_k = w_k.reshape(MLA_KV_RANK, MLA_HEADS * HEAD_PAD).astype(BF16)
    w_vt = w_ukv_h[:, :, MLA_NOPE:].reshape(MLA_KV_RANK, MLA_HEADS * MLA_V).T.astype(BF16)

    qg_col = jnp.pad(q_head_norm, (0, HEAD_PAD - MLA_QK)).reshape(HEAD_PAD, 1).astype(F32)
    kg_nope = jnp.pad(k_head_norm[:MLA_NOPE], (0, LANES - MLA_NOPE)).reshape(1, LANES).astype(F32)
    kg_rope = jnp.pad(k_head_norm[MLA_NOPE:], (MISC_PE, LANES - MISC_PE - MLA_ROPE))
    kg_rope = kg_rope.reshape(1, LANES).astype(F32)

    w_og = w_out[:GLA_V_W].astype(BF16)
    w_om = w_out[GLA_V_W:].astype(BF16)
    return w_in_p, w_gate_p, b_gate_p, w_uqt, w_k, w_vt, qg_col, kg_nope, kg_rope, w_og, w_om


def _layer(x, posr, invf_col, rope_sel, attn_norm, w_in, w_gate_up, b_gate, gla_out_norm,
           q_a_norm, w_uq, kv_a_norm, w_ukv, q_head_norm, k_head_norm, w_out, mlp_norm, w_up,
           w_down):
    b_, s_, d_model = x.shape
    tm = min(TOKEN_BLOCK, s_)
    tg = min(GLA_BLOCK, s_)
    tq = min(ATTN_TQ, s_)
    tk = min(ATTN_TK, tq)
    assert s_ % tm == 0 and s_ % tg == 0 and s_ % tq == 0 and tk % CHUNK == 0
    assert tq % (4 * tk) == 0
    assert tk == QUERY_GROUP and QK_LOOKAHEAD <= tq // QUERY_GROUP
    d_ff = w_up.shape[1]
    ff_block = min(FF_BLOCK, d_ff)

    (w_in_p, w_gate_p, b_gate_p, w_uqt, w_k, w_vt, qg_col, kg_nope, kg_rope, w_og,
     w_om) = _pack_weights(w_in, w_gate_up, b_gate, w_uq, w_ukv, q_head_norm, k_head_norm, w_out)
    row = lambda v: v.reshape(1, -1).astype(F32)

    tok = lambda w: pl.BlockSpec((1, tm, w), lambda b, i: (b, i, 0))
    tok_t = lambda hgt: pl.BlockSpec((1, hgt, tm), lambda b, i: (b, 0, i))
    consts = (row(attn_norm), w_in_p, w_gate_p, b_gate_p, row(q_a_norm), w_uqt, row(kv_a_norm),
              w_k, w_vt, qg_col, kg_nope, kg_rope, invf_col, rope_sel)
    zqk, loga, zv, gate, qt, kk, vt = pl.pallas_call(
        _proj_kernel,
        grid=(b_, s_ // tm),
        in_specs=[tok(d_model), tok_t(1)] + [_const_spec(c.shape) for c in consts],
        out_specs=[tok(2 * GLA_QK_W), tok(GLA_QK_W), tok(GLA_V_W), tok(GLA_V_W),
                   tok_t(MLA_HEADS * HEAD_PAD), tok(MLA_HEADS * HEAD_PAD), tok_t(MLA_HEADS * V_AUG)],
        out_shape=[jax.ShapeDtypeStruct((b_, s_, 2 * GLA_QK_W), F32),
                   jax.ShapeDtypeStruct((b_, s_, GLA_QK_W), F32),
                   jax.ShapeDtypeStruct((b_, s_, GLA_V_W), BF16),
                   jax.ShapeDtypeStruct((b_, s_, GLA_V_W), BF16),
                   jax.ShapeDtypeStruct((b_, MLA_HEADS * HEAD_PAD, s_), BF16),
                   jax.ShapeDtypeStruct((b_, s_, MLA_HEADS * HEAD_PAD), BF16),
                   jax.ShapeDtypeStruct((b_, MLA_HEADS * V_AUG, s_), BF16)],
        compiler_params=pltpu.CompilerParams(dimension_semantics=("parallel", "parallel"),
                                             vmem_limit_bytes=VMEM_LIMIT),
        name="proj",
    )(x, posr, *consts)

    gtok = lambda w: pl.BlockSpec((1, tg, w), lambda b, i: (b, i, 0))
    tri = jnp.asarray(np.tril(np.ones((CHUNK, CHUNK), np.float32)), BF16)
    o_gla = pl.pallas_call(
        functools.partial(_gla_kernel, n_chunks=tg // CHUNK),
        grid=(b_, s_ // tg),
        in_specs=[gtok(2 * GLA_QK_W), gtok(GLA_QK_W), gtok(GLA_V_W), gtok(GLA_V_W),
                  _const_spec((1, GLA_DV)), _const_spec((CHUNK, CHUNK))],
        out_specs=gtok(GLA_V_W),
        out_shape=jax.ShapeDtypeStruct((b_, s_, GLA_V_W), BF16),
        scratch_shapes=[pltpu.VMEM((GLA_DV, GLA_QK_W), F32)],
        compiler_params=pltpu.CompilerParams(dimension_semantics=("parallel", "arbitrary"),
                                             vmem_limit_bytes=VMEM_LIMIT),
        name="gla",
    )(zqk, loga, zv, gate, row(gla_out_norm), tri)

    o_t = pl.pallas_call(
        functools.partial(_attn_kernel, tq=tq, tk=tk),
        grid=(b_, MLA_HEADS, s_ // tq),
        in_specs=[pl.BlockSpec((1, HEAD_PAD, tq), lambda b, h, i: (b, h, i)),
                  pl.BlockSpec((1, s_, HEAD_PAD), lambda b, h, i: (b, 0, h)),
                  pl.BlockSpec((1, V_AUG, s_), lambda b, h, i: (b, h, 0))],
        out_specs=pl.BlockSpec((1, MLA_V, tq), lambda b, h, i: (b, h, i)),
        out_shape=jax.ShapeDtypeStruct((b_, MLA_HEADS * MLA_V, s_), BF16),
        scratch_shapes=[pltpu.VMEM((4, tk, tq), F32), pltpu.VMEM((V_AUG, tq), F32),
                        pltpu.VMEM((1, tq), F32), pltpu.VMEM((4, 1, tq), F32)],
        compiler_params=pltpu.CompilerParams(
            dimension_semantics=("parallel", "parallel", "arbitrary"),
            vmem_limit_bytes=VMEM_LIMIT),
        name="attn",
    )(qt, kk, vt)

    y = pl.pallas_call(
        functools.partial(_mlp_kernel, ff_block=ff_block),
        grid=(b_, s_ // tm),
        in_specs=[tok(d_model), tok(GLA_V_W), tok_t(MLA_HEADS * MLA_V),
                  _const_spec(w_og.shape), _const_spec(w_om.shape), _const_spec((1, d_model)),
                  _const_spec(w_up.shape), _const_spec(w_down.shape)],
        out_specs=tok(d_model),
        out_shape=jax.ShapeDtypeStruct((b_, s_, d_model), x.dtype),
        compiler_params=pltpu.CompilerParams(dimension_semantics=("parallel", "parallel"),
                                             vmem_limit_bytes=VMEM_LIMIT),
        name="mlp",
    )(x, o_gla, o_t, w_og, w_om, row(mlp_norm), w_up.astype(BF16), w_down.astype(BF16))
    return y


def kernel(x, positions, attn_norm, w_in, w_gate_up, b_gate, gla_out_norm, q_a_norm, w_uq,
           kv_a_norm, w_ukv, q_head_norm, k_head_norm, w_out, mlp_norm, w_up, w_down):
    b_, s_, _ = x.shape
    posr = positions.reshape(b_, 1, s_)
    inv_freq = ROPE_BASE ** (-jnp.arange(0, MLA_ROPE, 2, dtype=F32) / MLA_ROPE)
    invf_col = inv_freq.reshape(MLA_HALF, 1)
    i = np.arange(MLA_HALF)
    sel = np.zeros((MLA_ROPE, 3 * LANES), np.float32)
    sel[i, MISC_PE + i] = 1.0
    sel[i, MISC_PE + MLA_HALF + i] = 1.0
    sel[MLA_HALF + i, LANES + MISC_PE + i] = -1.0
    sel[MLA_HALF + i, 2 * LANES + MISC_PE + MLA_HALF + i] = 1.0
    rope_sel = jnp.asarray(np.tile(sel, (3, 1)), BF16)
    for l in range(attn_norm.shape[0]):
        x = _layer(x, posr, invf_col, rope_sel, attn_norm[l], w_in[l], w_gate_up[l],
                   b_gate[l], gla_out_norm[l], q_a_norm[l], w_uq[l], kv_a_norm[l], w_ukv[l],
                   q_head_norm[l], k_head_norm[l], w_out[l], mlp_norm[l], w_up[l], w_down[l])
    return x
```

```python
import functools
import math

import jax
import jax.numpy as jnp
import numpy as np
from jax import lax
from jax.experimental import pallas as pl
from jax.experimental.pallas import tpu as pltpu

F32 = jnp.float32
BF16 = jnp.bfloat16
HIGHEST = lax.Precision.HIGHEST

CHUNK = 64
EPS = 1e-6

GLA_HEADS = 4
GLA_DK = 64
GLA_DV = 128
GLA_GATE_RANK = 16
GLA_GATE_NORMALIZER = 16.0
GLA_LOG_GATE_MIN = -1.0
GLA_QK_W = GLA_HEADS * GLA_DK
GLA_V_W = GLA_HEADS * GLA_DV

MLA_HEADS = 8
MLA_Q_RANK = 256
MLA_KV_RANK = 128
MLA_NOPE = 64
MLA_ROPE = 32
MLA_HALF = MLA_ROPE // 2
MLA_QK = MLA_NOPE + MLA_ROPE
MLA_V = 64
ROPE_BASE = 10000.0

LANES = 128
HEAD_PAD = LANES
BF16_SUBLANES = 16
V_AUG = MLA_V + BF16_SUBLANES
LOG2_E = math.log2(math.e)
NEG = -1e30

C_Q = 0
C_K = C_Q + GLA_QK_W
C_V = C_K + GLA_QK_W
C_G = C_V + GLA_V_W
C_CQ = C_G + GLA_V_W
C_CKV = C_CQ + MLA_Q_RANK
C_MISC = C_CKV + MLA_KV_RANK
D_PROJ_PAD = C_MISC + LANES
MISC_GATE = 0
MISC_PE = MLA_NOPE

PROJ_PIECE = 256
TOKEN_BLOCK = 512
GLA_BLOCK = 512
ATTN_TQ = 4096
ATTN_TK = 256
QUERY_GROUP = 256
QK_LOOKAHEAD = 3
FF_BLOCK = 1024
VMEM_LIMIT = 56 * 1024 * 1024


def _nt(a, b):
    return lax.dot_general(a, b, (((1,), (1,)), ((), ())), preferred_element_type=F32)


def _tn(a, b):
    return lax.dot_general(a, b, (((0,), (0,)), ((), ())), preferred_element_type=F32)


def _rms(v):
    return v * lax.rsqrt(jnp.mean(v * v, axis=-1, keepdims=True) + EPS)


def _proj_kernel(x_ref, posr_ref, g_attn_ref, w_in_ref, w_gate_ref, b_gate_ref,
                 qa_g_ref, w_uqt_ref, kva_g_ref, w_k_ref, w_vt_ref,
                 qg_col_ref, kg_nope_ref, kg_rope_ref, invf_col_ref, rope_sel_ref,
                 zqk_ref, loga_ref, zv_ref, gate_ref, qt_ref, k_ref, vt_ref):
    hb = (_rms(x_ref[0]) * g_attn_ref[...]).astype(BF16)

    def in_proj(c0, c1):
        return jnp.dot(hb, w_in_ref[:, c0:c1], preferred_element_type=F32)

    zg0 = in_proj(C_G, C_G + PROJ_PIECE)
    zg1 = in_proj(C_G + PROJ_PIECE, C_CQ)
    z_mla = in_proj(C_CQ, D_PROJ_PAD)
    gate_ref[0, :, :PROJ_PIECE] = (zg0 * jax.nn.sigmoid(zg0)).astype(BF16)
    gate_ref[0, :, PROJ_PIECE:] = (zg1 * jax.nn.sigmoid(zg1)).astype(BF16)
    zcq = z_mla[:, :C_CKV - C_CQ]
    zckv = z_mla[:, C_CKV - C_CQ:C_MISC - C_CQ]
    misc = z_mla[:, C_MISC - C_CQ:]

    cq = (_rms(zcq) * qa_g_ref[...]).astype(BF16)
    ckv = (_rms(zckv) * kva_g_ref[...]).astype(BF16)
    vt = _nt(w_vt_ref[...], ckv).astype(BF16)
    ang_t = invf_col_ref[...] * posr_ref[0].astype(F32)
    cos_t = jnp.cos(ang_t)
    sin_t = jnp.sin(ang_t)
    cs3 = jnp.concatenate(_split3(jnp.concatenate([cos_t, sin_t], axis=0)), axis=0)
    tabs = _tn(cs3, rope_sel_ref[...])

    logit = jnp.dot(misc, w_gate_ref[...], preferred_element_type=F32, precision=HIGHEST)
    logit = logit + b_gate_ref[...]
    log_sig = jnp.minimum(logit, 0.0) - jnp.log1p(jnp.exp(-jnp.abs(logit)))
    loga_ref[0] = jnp.maximum(log_sig / GLA_GATE_NORMALIZER, GLA_LOG_GATE_MIN)

    ones_rows = jnp.ones((V_AUG - MLA_V, vt.shape[1]), BF16)
    for hh in range(MLA_HEADS):
        vt_ref[0, hh * V_AUG:hh * V_AUG + MLA_V, :] = vt[hh * MLA_V:(hh + 1) * MLA_V, :]
        vt_ref[0, hh * V_AUG + MLA_V:(hh + 1) * V_AUG, :] = ones_rows

    qscale = MLA_QK ** -0.5 * LOG2_E
    r1 = MLA_NOPE
    r2 = MLA_NOPE + MLA_HALF
    r3 = MLA_QK

    def q_head(hh, blk):
        base = hh * HEAD_PAD
        ssq = jnp.sum(blk * blk, axis=0, keepdims=True)
        rs = lax.rsqrt(ssq * (1.0 / MLA_QK) + EPS) * qscale
        qn = blk * rs * qg_col_ref[...]
        x1 = qn[r1:r2, :]
        x2 = qn[r2:r3, :]
        qt_ref[0, base:base + r1, :] = qn[:r1, :].astype(BF16)
        qt_ref[0, base + r1:base + r2, :] = (x1 * cos_t - x2 * sin_t).astype(BF16)
        qt_ref[0, base + r2:base + r3, :] = (x1 * sin_t + x2 * cos_t).astype(BF16)
        qt_ref[0, base + r3:base + HEAD_PAD, :] = jnp.zeros((HEAD_PAD - r3, blk.shape[1]), BF16)

    lane = lax.broadcasted_iota(jnp.int32, (1, LANES), 1)
    is_pe = (lane >= MISC_PE) & (lane < MISC_PE + MLA_ROPE)
    kpe = jnp.where(is_pe, misc, 0.0)
    ssq_pe = jnp.sum(kpe * kpe, axis=-1, keepdims=True)
    xg = kpe * kg_rope_ref[...]
    rot = (xg * tabs[:, :LANES]
           + pltpu.roll(xg, LANES - MLA_HALF, 1) * tabs[:, LANES:2 * LANES]
           + pltpu.roll(xg, MLA_HALF, 1) * tabs[:, 2 * LANES:])

    def k_head(hh, kn_h):
        base = hh * HEAD_PAD
        ssq = jnp.sum(kn_h * kn_h, axis=-1, keepdims=True) + ssq_pe
        rs = lax.rsqrt(ssq * (1.0 / MLA_QK) + EPS)
        k_ref[0, :, base:base + HEAD_PAD] = ((kn_h * kg_nope_ref[...] + rot) * rs).astype(BF16)

    pieces = [(zqk_ref, 0, C_Q), (zqk_ref, PROJ_PIECE, C_K),
              (zv_ref, 0, C_V), (zv_ref, PROJ_PIECE, C_V + PROJ_PIECE)]
    group = MLA_HEADS // len(pieces)
    for i, (ref, lane0, c0) in enumerate(pieces):
        rows = slice(i * group * HEAD_PAD, (i + 1) * group * HEAD_PAD)
        ref[0, :, lane0:lane0 + PROJ_PIECE] = in_proj(c0, c0 + PROJ_PIECE).astype(ref.dtype)
        qt = _nt(w_uqt_ref[rows, :], cq)
        kn = jnp.dot(ckv, w_k_ref[:, rows], preferred_element_type=F32)
        for n in range(group):
            q_head(i * group + n, qt[n * HEAD_PAD:(n + 1) * HEAD_PAD, :])
            k_head(i * group + n, kn[:, n * HEAD_PAD:(n + 1) * HEAD_PAD])


def _split3(v):
    hi = v.astype(BF16)
    r1 = v - hi.astype(F32)
    mid = r1.astype(BF16)
    lo = (r1 - mid.astype(F32)).astype(BF16)
    return hi, mid, lo


def _gla_kernel(zqk_ref, loga_ref, zv_ref, gate_ref, gout_ref, tri_ref, o_ref, st_ref, *,
                n_chunks):
    @pl.when(pl.program_id(1) == 0)
    def _():
        st_ref[...] = jnp.zeros(st_ref.shape, F32)

    sr = lax.broadcasted_iota(jnp.int32, (GLA_HEADS * CHUNK, CHUNK), 0)
    sc_ = lax.broadcasted_iota(jnp.int32, (GLA_HEADS * CHUNK, CHUNK), 1)
    causal4 = (sr % CHUNK) >= sc_
    qr = lax.broadcasted_iota(jnp.int32, (GLA_HEADS * CHUNK, GLA_QK_W), 0)
    qc = lax.broadcasted_iota(jnp.int32, (GLA_HEADS * CHUNK, GLA_QK_W), 1)
    head_sel = (qr // CHUNK) == (qc // GLA_DK)
    lane_head = lax.broadcasted_iota(jnp.int32, (GLA_DV, GLA_QK_W), 1) // GLA_DK
    gout = gout_ref[...]
    chunks = [slice(ci * CHUNK, (ci + 1) * CHUNK) for ci in range(n_chunks)]

    la = loga_ref[0]
    parts = [p for r in chunks for p in _split3(la[r])]
    tri_out = jnp.dot(tri_ref[...], jnp.concatenate(parts, axis=1), preferred_element_type=F32)
    w = GLA_QK_W
    cum = [tri_out[:, (3 * ci) * w:(3 * ci + 1) * w] + tri_out[:, (3 * ci + 1) * w:(3 * ci + 2) * w]
           + tri_out[:, (3 * ci + 2) * w:(3 * ci + 3) * w] for ci in range(n_chunks)]
    cl = [c[CHUNK - 1:CHUNK, :] for c in cum]

    def prep(ci):
        r = chunks[ci]
        q = zqk_ref[0, r, 0:GLA_QK_W]
        k = zqk_ref[0, r, GLA_QK_W:2 * GLA_QK_W]
        qd = (q * (GLA_DK ** -0.5) * jnp.exp(cum[ci])).astype(BF16)
        k_inv = (k * jnp.exp(-cum[ci])).astype(BF16)
        k_end = (k * jnp.exp(cl[ci] - cum[ci])).astype(BF16)
        q_stack = jnp.where(head_sel, jnp.concatenate([qd] * GLA_HEADS, axis=0), 0.0)
        return q_stack, k_inv, k_end

    def intra_scores(ci):
        return jnp.where(causal4, _nt(pre[ci][0], pre[ci][1]), 0.0).astype(BF16)

    def intra_out_and_kv(ci):
        v = zv_ref[0, chunks[ci], :]
        full = [jnp.dot(scores[ci][hh * CHUNK:(hh + 1) * CHUNK, :],
                        v[:, hh * GLA_DV:(hh + 1) * GLA_DV], preferred_element_type=F32)
                for hh in range(GLA_HEADS)]
        kv_t = _tn(v, pre[ci][2])
        kv = kv_t[(GLA_HEADS - 1) * GLA_DV:, :]
        for hh in range(GLA_HEADS - 2, -1, -1):
            kv = jnp.where(lane_head == hh, kv_t[hh * GLA_DV:(hh + 1) * GLA_DV, :], kv)
        return full, kv

    pre, scores, intra = {}, {}, {}
    for t in range(n_chunks + 2):
        if t < n_chunks:
            pre[t] = prep(t)
        if 0 <= t - 1 < n_chunks:
            scores[t - 1] = intra_scores(t - 1)
        if 0 <= t - 2 < n_chunks:
            intra[t - 2] = intra_out_and_kv(t - 2)

    st = st_ref[...]
    st_in = []
    for ci in range(n_chunks):
        st_in.append(st.astype(BF16))
        st = st * jnp.exp(cl[ci]) + intra[ci][1]
    st_ref[...] = st

    def finish(ci, inter):
        r = chunks[ci]
        for hh in range(GLA_HEADS):
            cols = slice(hh * GLA_DV, (hh + 1) * GLA_DV)
            o_h = inter[hh * CHUNK:(hh + 1) * CHUNK, :] + intra[ci][0][hh]
            g_h = gate_ref[0, r, cols].astype(F32)
            o_ref[0, r, cols] = (_rms(o_h) * gout * g_h).astype(BF16)

    inter = {}
    for t in range(n_chunks + 1):
        if t < n_chunks:
            inter[t] = _nt(pre[t][0], st_in[t])
        if t >= 1:
            finish(t - 1, inter[t - 1])


def _attn_kernel(qt_ref, k_ref, vt_ref, o_ref, s_ref, acc_ref, m_ref, bm_ref, *, tq, tk):
    qi = pl.program_id(2)
    ndiag = tq // tk

    m_ref[...] = jnp.full(m_ref.shape, NEG, F32)
    acc_ref[...] = jnp.zeros(acc_ref.shape, F32)

    kr = lax.broadcasted_iota(jnp.int32, (tk, tk), 0)
    qc = lax.broadcasted_iota(jnp.int32, (tk, tk), 1)
    square_mask = (kr // CHUNK) <= (qc // CHUNK)

    def qk(j, slot, c0=0, c1=tq):
        start = pl.multiple_of(j * tk, tk)
        s = jnp.dot(k_ref[0, pl.ds(start, tk), :], qt_ref[0, :, c0:c1],
                    preferred_element_type=F32)
        s_ref[slot, :, c0:c1] = s
        bm_ref[slot, :, c0:c1] = jnp.max(s, axis=0, keepdims=True)

    def mask_square(slot, c0):
        cols = slice(c0, c0 + tk)
        s = jnp.where(square_mask, s_ref[slot, :, cols], NEG)
        s_ref[slot, :, cols] = s
        bm_ref[slot, :, cols] = jnp.max(s, axis=0, keepdims=True)

    def softmax_pv(j, slots, c0=0, c1=tq):
        m_old = m_ref[:, c0:c1]
        m_new = m_old
        for slot in slots:
            m_new = jnp.maximum(m_new, bm_ref[slot, :, c0:c1])
        m_ref[:, c0:c1] = m_new
        p = jnp.concatenate([jnp.exp2(s_ref[slot, :, c0:c1] - m_new).astype(BF16)
                             for slot in slots], axis=0)
        start = pl.multiple_of(j * tk, tk)
        pv = jnp.dot(vt_ref[0, :, pl.ds(start, len(slots) * tk)], p,
                     preferred_element_type=F32)
        acc_ref[:, c0:c1] = jnp.exp2(m_old - m_new) * acc_ref[:, c0:c1] + pv

    ngroups = tq // QUERY_GROUP
    per_iter = 2 * ngroups

    def cols(g):
        return g * QUERY_GROUP, (g + 1) * QUERY_GROUP

    def qk_item(j, ps, g, both=True):
        qk(j, 2 * ps, *cols(g))
        if both:
            qk(j + 1, 2 * ps + 1, *cols(g))

    for u in range(QK_LOOKAHEAD):
        qk_item(0, 0, u)

    def body(i, carry):
        for u in range(per_iter):
            v = u + QK_LOOKAHEAD
            if v < per_iter:
                qk_item(4 * i + 2 * (v // ngroups), v // ngroups, v % ngroups)
            else:
                qk_item(4 * i + 4, 0, v - per_iter)
            ps, g = divmod(u, ngroups)
            softmax_pv(4 * i + 2 * ps, (2 * ps, 2 * ps + 1), *cols(g))
        return carry

    lax.fori_loop(0, qi * (ndiag // 4), body, 0)

    nfull = qi * ndiag
    items = [(p, g) for p in range(ndiag // 2) for g in range(2 * p, ngroups)]
    for t, (p, g) in enumerate(items):
        if t + QK_LOOKAHEAD < len(items):
            p2, g2 = items[t + QK_LOOKAHEAD]
            qk_item(nfull + 2 * p2, p2 % 2, g2, both=(g2 != 2 * p2))
        d = 2 * p
        slots = (2 * (p % 2), 2 * (p % 2) + 1)
        if g == d:
            mask_square(slots[0], d * tk)
            softmax_pv(nfull + d, slots[:1], *cols(g))
        else:
            if g == d + 1:
                mask_square(slots[1], (d + 1) * tk)
            softmax_pv(nfull + d, slots, *cols(g))

    acc = acc_ref[...]
    o_ref[0] = (acc[:MLA_V, :] / acc[MLA_V:MLA_V + 1, :]).astype(BF16)


def _mlp_kernel(x_ref, og_ref, ot_ref, w_og_ref, w_om_ref, g_mlp_ref, w_up_ref, w_dn_ref,
                y_ref, *, ff_block):
    mix = jnp.dot(og_ref[0], w_og_ref[...], preferred_element_type=F32)
    mix = mix + _tn(ot_ref[0], w_om_ref[...])
    x1 = x_ref[0] + mix
    h = (_rms(x1) * g_mlp_ref[...]).astype(BF16)
    acc = x1
    d_ff = w_up_ref.shape[1]
    for f in range(d_ff // ff_block):
        cols = slice(f * ff_block, (f + 1) * ff_block)
        u = jnp.dot(h, w_up_ref[:, cols], preferred_element_type=F32)
        a = jnp.square(jnp.maximum(u, 0.0)).astype(BF16)
        acc = acc + jnp.dot(a, w_dn_ref[cols, :], preferred_element_type=F32)
    y_ref[0] = acc


def _const_spec(shape):
    nd = len(shape)
    return pl.BlockSpec(shape, lambda *_: (0,) * nd)


def _pack_weights(w_in, w_gate_up, b_gate, w_uq, w_ukv, q_head_norm, k_head_norm, w_out):
    d_model = w_in.shape[0]
    o = 0
    parts = {}
    for name, width in (("q", GLA_QK_W), ("k", GLA_QK_W), ("v", GLA_V_W), ("gate", GLA_GATE_RANK),
                        ("g", GLA_V_W), ("cq", MLA_Q_RANK), ("ckv", MLA_KV_RANK), ("pe", MLA_ROPE)):
        parts[name] = w_in[:, o:o + width]
        o += width
    gap = lambda n: jnp.zeros((d_model, n), w_in.dtype)
    assert MISC_GATE == 0
    w_in_p = jnp.concatenate(
        [parts["q"], parts["k"], parts["v"], parts["g"], parts["cq"], parts["ckv"],
         parts["gate"], gap(MISC_PE - GLA_GATE_RANK), parts["pe"],
         gap(LANES - MISC_PE - MLA_ROPE)], axis=1).astype(BF16)

    w_gate_p = jnp.pad(w_gate_up.astype(F32), ((0, LANES - GLA_GATE_RANK), (0, 0)))
    b_gate_p = b_gate.reshape(1, GLA_QK_W).astype(F32)

    w_uq_h = w_uq.reshape(MLA_Q_RANK, MLA_HEADS, MLA_QK)
    w_uq_h = jnp.pad(w_uq_h, ((0, 0), (0, 0), (0, HEAD_PAD - MLA_QK)))
    w_uqt = w_uq_h.reshape(MLA_Q_RANK, MLA_HEADS * HEAD_PAD).T.astype(BF16)

    w_ukv_h = w_ukv.reshape(MLA_KV_RANK, MLA_HEADS, MLA_NOPE + MLA_V)
    w_k = jnp.pad(w_ukv_h[:, :, :MLA_NOPE], ((0, 0), (0, 0), (0, HEAD_PAD - MLA_NOPE)))
    w_k = w_k.reshape(MLA_KV_RANK, MLA_HEADS * HEAD_PAD).astype(BF16)
    w_vt = w_ukv_h[:, :, MLA_NOPE:].reshape(MLA_KV_RANK, MLA_HEADS * MLA_V).T.astype(BF16)

    qg_col = jnp.pad(q_head_norm, (0, HEAD_PAD - MLA_QK)).reshape(HEAD_PAD, 1).astype(F32)
    kg_nope = jnp.pad(k_head_norm[:MLA_NOPE], (0, LANES - MLA_NOPE)).reshape(1, LANES).astype(F32)
    kg_rope = jnp.pad(k_head_norm[MLA_NOPE:], (MISC_PE, LANES - MISC_PE - MLA_ROPE))
    kg_rope = kg_rope.reshape(1, LANES).astype(F32)

    w_og = w_out[:GLA_V_W].astype(BF16)
    w_om = w_out[GLA_V_W:].astype(BF16)
    return w_in_p, w_gate_p, b_gate_p, w_uqt, w_k, w_vt, qg_col, kg_nope, kg_rope, w_og, w_om


def _layer(x, posr, invf_col, rope_sel, attn_norm, w_in, w_gate_up, b_gate, gla_out_norm,
           q_a_norm, w_uq, kv_a_norm, w_ukv, q_head_norm, k_head_norm, w_out, mlp_norm, w_up,
           w_down):
    b_, s_, d_model = x.shape
    tm = min(TOKEN_BLOCK, s_)
    tg = min(GLA_BLOCK, s_)
    tq = min(ATTN_TQ, s_)
    tk = min(ATTN_TK, tq)
    assert s_ % tm == 0 and s_ % tg == 0 and s_ % tq == 0 and tk % CHUNK == 0
    assert tq % (4 * tk) == 0
    assert tk == QUERY_GROUP and QK_LOOKAHEAD <= tq // QUERY_GROUP
    d_ff = w_up.shape[1]
    ff_block = min(FF_BLOCK, d_ff)

    (w_in_p, w_gate_p, b_gate_p, w_uqt, w_k, w_vt, qg_col, kg_nope, kg_rope, w_og,
     w_om) = _pack_weights(w_in, w_gate_up, b_gate, w_uq, w_ukv, q_head_norm, k_head_norm, w_out)
    row = lambda v: v.reshape(1, -1).astype(F32)

    tok = lambda w: pl.BlockSpec((1, tm, w), lambda b, i: (b, i, 0))
    tok_t = lambda hgt: pl.BlockSpec((1, hgt, tm), lambda b, i: (b, 0, i))
    consts = (row(attn_norm), w_in_p, w_gate_p, b_gate_p, row(q_a_norm), w_uqt, row(kv_a_norm),
              w_k, w_vt, qg_col, kg_nope, kg_rope, invf_col, rope_sel)
    zqk, loga, zv, gate, qt, kk, vt = pl.pallas_call(
        _proj_kernel,
        grid=(b_, s_ // tm),
        in_specs=[tok(d_model), tok_t(1)] + [_const_spec(c.shape) for c in consts],
        out_specs=[tok(2 * GLA_QK_W), tok(GLA_QK_W), tok(GLA_V_W), tok(GLA_V_W),
                   tok_t(MLA_HEADS * HEAD_PAD), tok(MLA_HEADS * HEAD_PAD), tok_t(MLA_HEADS * V_AUG)],
        out_shape=[jax.ShapeDtypeStruct((b_, s_, 2 * GLA_QK_W), F32),
                   jax.ShapeDtypeStruct((b_, s_, GLA_QK_W), F32),
                   jax.ShapeDtypeStruct((b_, s_, GLA_V_W), BF16),
                   jax.ShapeDtypeStruct((b_, s_, GLA_V_W), BF16),
                   jax.ShapeDtypeStruct((b_, MLA_HEADS * HEAD_PAD, s_), BF16),
                   jax.ShapeDtypeStruct((b_, s_, MLA_HEADS * HEAD_PAD), BF16),
                   jax.ShapeDtypeStruct((b_, MLA_HEADS * V_AUG, s_), BF16)],
        compiler_params=pltpu.CompilerParams(dimension_semantics=("parallel", "parallel"),
                                             vmem_limit_bytes=VMEM_LIMIT),
        name="proj",
    )(x, posr, *consts)

    gtok = lambda w: pl.BlockSpec((1, tg, w), lambda b, i: (b, i, 0))
    tri = jnp.asarray(np.tril(np.ones((CHUNK, CHUNK), np.float32)), BF16)
    o_gla = pl.pallas_call(
        functools.partial(_gla_kernel, n_chunks=tg // CHUNK),
        grid=(b_, s_ // tg),
        in_specs=[gtok(2 * GLA_QK_W), gtok(GLA_QK_W), gtok(GLA_V_W), gtok(GLA_V_W),
                  _const_spec((1, GLA_DV)), _const_spec((CHUNK, CHUNK))],
        out_specs=gtok(GLA_V_W),
        out_shape=jax.ShapeDtypeStruct((b_, s_, GLA_V_W), BF16),
        scratch_shapes=[pltpu.VMEM((GLA_DV, GLA_QK_W), F32)],
        compiler_params=pltpu.CompilerParams(dimension_semantics=("parallel", "arbitrary"),
                                             vmem_limit_bytes=VMEM_LIMIT),
        name="gla",
    )(zqk, loga, zv, gate, row(gla_out_norm), tri)

    o_t = pl.pallas_call(
        functools.partial(_attn_kernel, tq=tq, tk=tk),
        grid=(b_, MLA_HEADS, s_ // tq),
        in_specs=[pl.BlockSpec((1, HEAD_PAD, tq), lambda b, h, i: (b, h, i)),
                  pl.BlockSpec((1, s_, HEAD_PAD), lambda b, h, i: (b, 0, h)),
                  pl.BlockSpec((1, V_AUG, s_), lambda b, h, i: (b, h, 0))],
        out_specs=pl.BlockSpec((1, MLA_V, tq), lambda b, h, i: (b, h, i)),
        out_shape=jax.ShapeDtypeStruct((b_, MLA_HEADS * MLA_V, s_), BF16),
        scratch_shapes=[pltpu.VMEM((4, tk, tq), F32), pltpu.VMEM((V_AUG, tq), F32),
                        pltpu.VMEM((1, tq), F32), pltpu.VMEM((4, 1, tq), F32)],
        compiler_params=pltpu.CompilerParams(
            dimension_semantics=("parallel", "parallel", "arbitrary"),
            vmem_limit_bytes=VMEM_LIMIT),
        name="attn",
    )(qt, kk, vt)

    y = pl.pallas_call(
        functools.partial(_mlp_kernel, ff_block=ff_block),
        grid=(b_, s_ // tm),
        in_specs=[tok(d_model), tok(GLA_V_W), tok_t(MLA_HEADS * MLA_V),
                  _const_spec(w_og.shape), _const_spec(w_om.shape), _const_spec((1, d_model)),
                  _const_spec(w_up.shape), _const_spec(w_down.shape)],
        out_specs=tok(d_model),
        out_shape=jax.ShapeDtypeStruct((b_, s_, d_model), x.dtype),
        compiler_params=pltpu.CompilerParams(dimension_semantics=("parallel", "parallel"),
                                             vmem_limit_bytes=VMEM_LIMIT),
        name="mlp",
    )(x, o_gla, o_t, w_og, w_om, row(mlp_norm), w_up.astype(BF16), w_down.astype(BF16))
    return y


def kernel(x, positions, attn_norm, w_in, w_gate_up, b_gate, gla_out_norm, q_a_norm, w_uq,
           kv_a_norm, w_ukv, q_head_norm, k_head_norm, w_out, mlp_norm, w_up, w_down):
    b_, s_, _ = x.shape
    posr = positions.reshape(b_, 1, s_)
    inv_freq = ROPE_BASE ** (-jnp.arange(0, MLA_ROPE, 2, dtype=F32) / MLA_ROPE)
    invf_col = inv_freq.reshape(MLA_HALF, 1)
    i = np.arange(MLA_HALF)
    sel = np.zeros((MLA_ROPE, 3 * LANES), np.float32)
    sel[i, MISC_PE + i] = 1.0
    sel[i, MISC_PE + MLA_HALF + i] = 1.0
    sel[MLA_HALF + i, LANES + MISC_PE + i] = -1.0
    sel[MLA_HALF + i, 2 * LANES + MISC_PE + MLA_HALF + i] = 1.0
    rope_sel = jnp.asarray(np.tile(sel, (3, 1)), BF16)
    for l in range(attn_norm.shape[0]):
        x = _layer(x, posr, invf_col, rope_sel, attn_norm[l], w_in[l], w_gate_up[l],
                   b_gate[l], gla_out_norm[l], q_a_norm[l], w_uq[l], kv_a_norm[l], w_ukv[l],
                   q_head_norm[l], k_head_norm[l], w_out[l], mlp_norm[l], w_up[l], w_down[l])
    return x
```

```python
import functools
import math

import jax
import jax.numpy as jnp
import numpy as np
from jax import lax
from jax.experimental import pallas as pl
from jax.experimental.pallas import tpu as pltpu

F32 = jnp.float32
BF16 = jnp.bfloat16
HIGHEST = lax.Precision.HIGHEST

CHUNK = 64
EPS = 1e-6

GLA_HEADS = 4
GLA_DK = 64
GLA_DV = 128
GLA_GATE_RANK = 16
GLA_GATE_NORMALIZER = 16.0
GLA_LOG_GATE_MIN = -1.0
GLA_QK_W = GLA_HEADS * GLA_DK
GLA_V_W = GLA_HEADS * GLA_DV

MLA_HEADS = 8
MLA_Q_RANK = 256
MLA_KV_RANK = 128
MLA_NOPE = 64
MLA_ROPE = 32
MLA_HALF = MLA_ROPE // 2
MLA_QK = MLA_NOPE + MLA_ROPE
MLA_V = 64
ROPE_BASE = 10000.0

LANES = 128
HEAD_PAD = LANES
BF16_SUBLANES = 16
V_AUG = MLA_V + BF16_SUBLANES
LOG2_E = math.log2(math.e)
NEG = -1e30

H_NOPE = MLA_ROPE
A_Q = 0
A_K = A_Q + GLA_QK_W
A_V = A_K + GLA_QK_W
A_W = A_V + GLA_V_W
B_G = 0
B_CQ = B_G + GLA_V_W
B_CKV = B_CQ + MLA_Q_RANK
B_MISC = B_CKV + MLA_KV_RANK
B_W = B_MISC + LANES
MISC_PE = 0
MISC_GATE = MLA_ROPE

PROJ_PIECE = 256
TOKEN_BLOCK = 512
GLA_BLOCK = 512
ATTN_TQ = 4096
ATTN_TK = 256
QUERY_GROUP = 256
QK_LOOKAHEAD = 3
FF_BLOCK = 1024
VMEM_LIMIT = 56 * 1024 * 1024


def _nt(a, b):
    return lax.dot_general(a, b, (((1,), (1,)), ((), ())), preferred_element_type=F32)


def _tn(a, b):
    return lax.dot_general(a, b, (((0,), (0,)), ((), ())), preferred_element_type=F32)


def _rms(v):
    return v * lax.rsqrt(jnp.mean(v * v, axis=-1, keepdims=True) + EPS)


def _proj_kernel(x_ref, posr_ref, g_attn_ref, w_a_ref, w_b_ref, w_gate_ref, b_gate_ref,
                 qa_g_ref, w_uqt_ref, kva_g_ref, w_k_ref, w_vt_ref,
                 qg_col_ref, kg_nope_ref, kg_rope_ref, invf_col_ref, rope_sel_ref,
                 zqk_ref, loga_ref, zv_ref, gate_ref, qt_ref, k_ref, vt_ref):
    hb = (_rms(x_ref[0]) * g_attn_ref[...]).astype(BF16)

    def in_proj(w_ref, c0, c1):
        return jnp.dot(hb, w_ref[:, c0:c1], preferred_element_type=F32)

    zg0 = in_proj(w_b_ref, B_G, B_G + PROJ_PIECE)
    zg1 = in_proj(w_b_ref, B_G + PROJ_PIECE, B_CQ)
    z_mla = in_proj(w_b_ref, B_CQ, B_W)
    gate_ref[0, :, :PROJ_PIECE] = (zg0 * jax.nn.sigmoid(zg0)).astype(BF16)
    gate_ref[0, :, PROJ_PIECE:] = (zg1 * jax.nn.sigmoid(zg1)).astype(BF16)
    zcq = z_mla[:, :B_CKV - B_CQ]
    zckv = z_mla[:, B_CKV - B_CQ:B_MISC - B_CQ]
    misc = z_mla[:, B_MISC - B_CQ:]

    cq = (_rms(zcq) * qa_g_ref[...]).astype(BF16)
    ckv = (_rms(zckv) * kva_g_ref[...]).astype(BF16)
    vt = _nt(w_vt_ref[...], ckv).astype(BF16)
    ang_t = invf_col_ref[...] * posr_ref[0].astype(F32)
    cos_t = jnp.cos(ang_t)
    sin_t = jnp.sin(ang_t)
    cs3 = jnp.concatenate(_split3(jnp.concatenate([cos_t, sin_t], axis=0)), axis=0)
    tabs = _tn(cs3, rope_sel_ref[...])

    logit = jnp.dot(misc, w_gate_ref[...], preferred_element_type=F32, precision=HIGHEST)
    logit = logit + b_gate_ref[...]
    log_sig = jnp.minimum(logit, 0.0) - jnp.log1p(jnp.exp(-jnp.abs(logit)))
    loga_ref[0] = jnp.maximum(log_sig / GLA_GATE_NORMALIZER, GLA_LOG_GATE_MIN)

    ones_rows = jnp.ones((V_AUG - MLA_V, vt.shape[1]), BF16)
    for hh in range(MLA_HEADS):
        vt_ref[0, hh * V_AUG:hh * V_AUG + MLA_V, :] = vt[hh * MLA_V:(hh + 1) * MLA_V, :]
        vt_ref[0, hh * V_AUG + MLA_V:(hh + 1) * V_AUG, :] = ones_rows

    qscale = MLA_QK ** -0.5 * LOG2_E
    r1 = MLA_HALF
    r2 = H_NOPE
    r3 = H_NOPE + MLA_NOPE

    def q_head(hh, blk):
        base = hh * HEAD_PAD
        ssq = jnp.sum(blk * blk, axis=0, keepdims=True)
        rs = lax.rsqrt(ssq * (1.0 / MLA_QK) + EPS) * qscale
        qn = blk * rs * qg_col_ref[...]
        x1 = qn[:r1, :]
        x2 = qn[r1:r2, :]
        qt_ref[0, base:base + r1, :] = (x1 * cos_t - x2 * sin_t).astype(BF16)
        qt_ref[0, base + r1:base + r2, :] = (x1 * sin_t + x2 * cos_t).astype(BF16)
        qt_ref[0, base + r2:base + r3, :] = qn[r2:r3, :].astype(BF16)
        qt_ref[0, base + r3:base + HEAD_PAD, :] = jnp.zeros((HEAD_PAD - r3, blk.shape[1]), BF16)

    lane = lax.broadcasted_iota(jnp.int32, (1, LANES), 1)
    is_pe = (lane >= MISC_PE) & (lane < MISC_PE + MLA_ROPE)
    kpe = jnp.where(is_pe, misc, 0.0)
    ssq_pe = jnp.sum(kpe * kpe, axis=-1, keepdims=True)
    xg = kpe * kg_rope_ref[...]
    rot = (xg * tabs[:, :LANES]
           + pltpu.roll(xg, LANES - MLA_HALF, 1) * tabs[:, LANES:2 * LANES]
           + pltpu.roll(xg, MLA_HALF, 1) * tabs[:, 2 * LANES:])

    def k_head(hh, kn_h):
        base = hh * HEAD_PAD
        ssq = jnp.sum(kn_h * kn_h, axis=-1, keepdims=True) + ssq_pe
        rs = lax.rsqrt(ssq * (1.0 / MLA_QK) + EPS)
        k_ref[0, :, base:base + HEAD_PAD] = ((kn_h * kg_nope_ref[...] + rot) * rs).astype(BF16)

    pieces = [(zqk_ref, 0, A_Q), (zqk_ref, PROJ_PIECE, A_K),
              (zv_ref, 0, A_V), (zv_ref, PROJ_PIECE, A_V + PROJ_PIECE)]
    group = MLA_HEADS // len(pieces)
    for i, (ref, lane0, c0) in enumerate(pieces):
        rows = slice(i * group * HEAD_PAD, (i + 1) * group * HEAD_PAD)
        ref[0, :, lane0:lane0 + PROJ_PIECE] = in_proj(w_a_ref, c0, c0 + PROJ_PIECE).astype(ref.dtype)
        qt = _nt(w_uqt_ref[rows, :], cq)
        kn = jnp.dot(ckv, w_k_ref[:, rows], preferred_element_type=F32)
        for n in range(group):
            q_head(i * group + n, qt[n * HEAD_PAD:(n + 1) * HEAD_PAD, :])
            k_head(i * group + n, kn[:, n * HEAD_PAD:(n + 1) * HEAD_PAD])


def _split3(v):
    hi = v.astype(BF16)
    r1 = v - hi.astype(F32)
    mid = r1.astype(BF16)
    lo = (r1 - mid.astype(F32)).astype(BF16)
    return hi, mid, lo


def _gla_kernel(zqk_ref, loga_ref, zv_ref, gate_ref, gout_ref, tri_ref, o_ref, st_ref, *,
                n_chunks):
    @pl.when(pl.program_id(1) == 0)
    def _():
        st_ref[...] = jnp.zeros(st_ref.shape, F32)

    sr = lax.broadcasted_iota(jnp.int32, (GLA_HEADS * CHUNK, CHUNK), 0)
    sc_ = lax.broadcasted_iota(jnp.int32, (GLA_HEADS * CHUNK, CHUNK), 1)
    causal4 = (sr % CHUNK) >= sc_
    qr = lax.broadcasted_iota(jnp.int32, (GLA_HEADS * CHUNK, GLA_QK_W), 0)
    qc = lax.broadcasted_iota(jnp.int32, (GLA_HEADS * CHUNK, GLA_QK_W), 1)
    head_sel = (qr // CHUNK) == (qc // GLA_DK)
    lane_head = lax.broadcasted_iota(jnp.int32, (GLA_DV, GLA_QK_W), 1) // GLA_DK
    gout = gout_ref[...]
    chunks = [slice(ci * CHUNK, (ci + 1) * CHUNK) for ci in range(n_chunks)]

    la = loga_ref[0]
    parts = [p for r in chunks for p in _split3(la[r])]
    tri_out = jnp.dot(tri_ref[...], jnp.concatenate(parts, axis=1), preferred_element_type=F32)
    w = GLA_QK_W
    cum = [tri_out[:, (3 * ci) * w:(3 * ci + 1) * w] + tri_out[:, (3 * ci + 1) * w:(3 * ci + 2) * w]
           + tri_out[:, (3 * ci + 2) * w:(3 * ci + 3) * w] for ci in range(n_chunks)]
    cl = [c[CHUNK - 1:CHUNK, :] for c in cum]

    def prep(ci):
        r = chunks[ci]
        q = zqk_ref[0, r, 0:GLA_QK_W]
        k = zqk_ref[0, r, GLA_QK_W:2 * GLA_QK_W]
        qd = (q * (GLA_DK ** -0.5) * jnp.exp(cum[ci])).astype(BF16)
        k_inv = (k * jnp.exp(-cum[ci])).astype(BF16)
        k_end = (k * jnp.exp(cl[ci] - cum[ci])).astype(BF16)
        q_stack = jnp.where(head_sel, jnp.concatenate([qd] * GLA_HEADS, axis=0), 0.0)
        return q_stack, k_inv, k_end

    def intra_scores(ci):
        return jnp.where(causal4, _nt(pre[ci][0], pre[ci][1]), 0.0).astype(BF16)

    def intra_out_and_kv(ci):
        v = zv_ref[0, chunks[ci], :]
        full = [jnp.dot(scores[ci][hh * CHUNK:(hh + 1) * CHUNK, :],
                        v[:, hh * GLA_DV:(hh + 1) * GLA_DV], preferred_element_type=F32)
                for hh in range(GLA_HEADS)]
        kv_t = _tn(v, pre[ci][2])
        kv = kv_t[(GLA_HEADS - 1) * GLA_DV:, :]
        for hh in range(GLA_HEADS - 2, -1, -1):
            kv = jnp.where(lane_head == hh, kv_t[hh * GLA_DV:(hh + 1) * GLA_DV, :], kv)
        return full, kv

    pre, scores, intra = {}, {}, {}
    for t in range(n_chunks + 2):
        if t < n_chunks:
            pre[t] = prep(t)
        if 0 <= t - 1 < n_chunks:
            scores[t - 1] = intra_scores(t - 1)
        if 0 <= t - 2 < n_chunks:
            intra[t - 2] = intra_out_and_kv(t - 2)

    st = st_ref[...]
    st_in = []
    for ci in range(n_chunks):
        st_in.append(st.astype(BF16))
        st = st * jnp.exp(cl[ci]) + intra[ci][1]
    st_ref[...] = st

    def finish(ci, inter):
        r = chunks[ci]
        for hh in range(GLA_HEADS):
            cols = slice(hh * GLA_DV, (hh + 1) * GLA_DV)
            o_h = inter[hh * CHUNK:(hh + 1) * CHUNK, :] + intra[ci][0][hh]
            g_h = gate_ref[0, r, cols].astype(F32)
            o_ref[0, r, cols] = (_rms(o_h) * gout * g_h).astype(BF16)

    inter = {}
    for t in range(n_chunks + 1):
        if t < n_chunks:
            inter[t] = _nt(pre[t][0], st_in[t])
        if t >= 1:
            finish(t - 1, inter[t - 1])


def _attn_kernel(qt_ref, k_ref, vt_ref, o_ref, s_ref, acc_ref, m_ref, bm_ref, *, tq, tk):
    qi = pl.program_id(2)
    ndiag = tq // tk

    m_ref[...] = jnp.full(m_ref.shape, NEG, F32)
    acc_ref[...] = jnp.zeros(acc_ref.shape, F32)

    kr = lax.broadcasted_iota(jnp.int32, (tk, tk), 0)
    qc = lax.broadcasted_iota(jnp.int32, (tk, tk), 1)
    square_mask = (kr // CHUNK) <= (qc // CHUNK)

    def qk(j, slot, c0=0, c1=tq):
        start = pl.multiple_of(j * tk, tk)
        s = jnp.dot(k_ref[0, pl.ds(start, tk), :], qt_ref[0, :, c0:c1],
                    preferred_element_type=F32)
        s_ref[slot, :, c0:c1] = s
        bm_ref[slot, :, c0:c1] = jnp.max(s, axis=0, keepdims=True)

    def mask_square(slot, c0):
        cols = slice(c0, c0 + tk)
        s = jnp.where(square_mask, s_ref[slot, :, cols], NEG)
        s_ref[slot, :, cols] = s
        bm_ref[slot, :, cols] = jnp.max(s, axis=0, keepdims=True)

    def softmax_pv(j, slots, c0=0, c1=tq):
        m_old = m_ref[:, c0:c1]
        m_new = m_old
        for slot in slots:
            m_new = jnp.maximum(m_new, bm_ref[slot, :, c0:c1])
        m_ref[:, c0:c1] = m_new
        p = jnp.concatenate([jnp.exp2(s_ref[slot, :, c0:c1] - m_new).astype(BF16)
                             for slot in slots], axis=0)
        start = pl.multiple_of(j * tk, tk)
        pv = jnp.dot(vt_ref[0, :, pl.ds(start, len(slots) * tk)], p,
                     preferred_element_type=F32)
        acc_ref[:, c0:c1] = jnp.exp2(m_old - m_new) * acc_ref[:, c0:c1] + pv

    ngroups = tq // QUERY_GROUP
    per_iter = 2 * ngroups

    def cols(g):
        return g * QUERY_GROUP, (g + 1) * QUERY_GROUP

    def qk_item(j, ps, g, both=True):
        qk(j, 2 * ps, *cols(g))
        if both:
            qk(j + 1, 2 * ps + 1, *cols(g))

    for u in range(QK_LOOKAHEAD):
        qk_item(0, 0, u)

    def body(i, carry):
        for u in range(per_iter):
            v = u + QK_LOOKAHEAD
            if v < per_iter:
                qk_item(4 * i + 2 * (v // ngroups), v // ngroups, v % ngroups)
            else:
                qk_item(4 * i + 4, 0, v - per_iter)
            ps, g = divmod(u, ngroups)
            softmax_pv(4 * i + 2 * ps, (2 * ps, 2 * ps + 1), *cols(g))
        return carry

    lax.fori_loop(0, qi * (ndiag // 4), body, 0)

    nfull = qi * ndiag
    items = [(p, g) for p in range(ndiag // 2) for g in range(2 * p, ngroups)]
    for t, (p, g) in enumerate(items):
        if t + QK_LOOKAHEAD < len(items):
            p2, g2 = items[t + QK_LOOKAHEAD]
            qk_item(nfull + 2 * p2, p2 % 2, g2, both=(g2 != 2 * p2))
        d = 2 * p
        slots = (2 * (p % 2), 2 * (p % 2) + 1)
        if g == d:
            mask_square(slots[0], d * tk)
            softmax_pv(nfull + d, slots[:1], *cols(g))
        else:
            if g == d + 1:
                mask_square(slots[1], (d + 1) * tk)
            softmax_pv(nfull + d, slots, *cols(g))

    acc = acc_ref[...]
    o_ref[0] = (acc[:MLA_V, :] / acc[MLA_V:MLA_V + 1, :]).astype(BF16)


def _mlp_kernel(x_ref, og_ref, ot_ref, w_out_ref, g_mlp_ref, w_up_ref, w_dn_ref,
                y_ref, *, ff_block):
    mix = jnp.dot(og_ref[0], w_out_ref[:GLA_V_W, :], preferred_element_type=F32)
    mix = mix + _tn(ot_ref[0], w_out_ref[GLA_V_W:, :])
    x1 = x_ref[0] + mix
    h = (_rms(x1) * g_mlp_ref[...]).astype(BF16)
    acc = x1
    d_ff = w_up_ref.shape[1]
    for f in range(d_ff // ff_block):
        cols = slice(f * ff_block, (f + 1) * ff_block)
        u = jnp.dot(h, w_up_ref[:, cols], preferred_element_type=F32)
        a = jnp.square(jnp.maximum(u, 0.0)).astype(BF16)
        acc = acc + jnp.dot(a, w_dn_ref[cols, :], preferred_element_type=F32)
    y_ref[0] = acc


def _const_spec(shape):
    nd = len(shape)
    return pl.BlockSpec(shape, lambda *_: (0,) * nd)


def _pack_weights(w_in, w_gate_up, b_gate, w_uq, w_ukv, q_head_norm, k_head_norm):
    o_gate = 2 * GLA_QK_W + GLA_V_W
    o_rest = o_gate + GLA_GATE_RANK
    rest = w_in[:, o_rest:]
    assert A_W == o_gate and rest.shape[1] == B_MISC + MLA_ROPE and MISC_PE == 0
    w_a = w_in[:, :o_gate].astype(BF16)
    gate_at = B_MISC + MISC_GATE
    w_b = (jnp.pad(rest, ((0, 0), (0, B_W - rest.shape[1])))
           + jnp.pad(w_in[:, o_gate:o_rest], ((0, 0), (gate_at, B_W - gate_at - GLA_GATE_RANK)))
           ).astype(BF16)

    w_gate_p = jnp.pad(w_gate_up.astype(F32),
                       ((MISC_GATE, LANES - MISC_GATE - GLA_GATE_RANK), (0, 0)))
    b_gate_p = b_gate.reshape(1, GLA_QK_W).astype(F32)

    w_uq_h = w_uq.reshape(MLA_Q_RANK, MLA_HEADS, MLA_QK)
    w_uq_h = jnp.concatenate([w_uq_h[:, :, MLA_NOPE:], w_uq_h[:, :, :MLA_NOPE]], axis=2)
    w_uq_h = jnp.pad(w_uq_h, ((0, 0), (0, 0), (0, HEAD_PAD - MLA_QK)))
    w_uqt = w_uq_h.reshape(MLA_Q_RANK, MLA_HEADS * HEAD_PAD).T.astype(BF16)

    nope_pad = (H_NOPE, HEAD_PAD - H_NOPE - MLA_NOPE)
    w_ukv_h = w_ukv.reshape(MLA_KV_RANK, MLA_HEADS, MLA_NOPE + MLA_V)
    w_k = jnp.pad(w_ukv_h[:, :, :MLA_NOPE], ((0, 0), (0, 0), nope_pad))
    w_k = w_k.reshape(MLA_KV_RANK, MLA_HEADS * HEAD_PAD).astype(BF16)
    w_vt = w_ukv_h[:, :, MLA_NOPE:].reshape(MLA_KV_RANK, MLA_HEADS * MLA_V).T.astype(BF16)

    qg = jnp.concatenate([q_head_norm[MLA_NOPE:], q_head_norm[:MLA_NOPE]])
    qg_col = jnp.pad(qg, (0, HEAD_PAD - MLA_QK)).reshape(HEAD_PAD, 1).astype(F32)
    kg_nope = jnp.pad(k_head_norm[:MLA_NOPE], nope_pad).reshape(1, LANES).astype(F32)
    kg_rope = jnp.pad(k_head_norm[MLA_NOPE:], (MISC_PE, LANES - MISC_PE - MLA_ROPE))
    kg_rope = kg_rope.reshape(1, LANES).astype(F32)
    return w_a, w_b, w_gate_p, b_gate_p, w_uqt, w_k, w_vt, qg_col, kg_nope, kg_rope


def _layer(x, posr, invf_col, rope_sel, attn_norm, w_in, w_gate_up, b_gate, gla_out_norm,
           q_a_norm, w_uq, kv_a_norm, w_ukv, q_head_norm, k_head_norm, w_out, mlp_norm, w_up,
           w_down):
    b_, s_, d_model = x.shape
    tm = min(TOKEN_BLOCK, s_)
    tg = min(GLA_BLOCK, s_)
    tq = min(ATTN_TQ, s_)
    tk = min(ATTN_TK, tq)
    assert s_ % tm == 0 and s_ % tg == 0 and s_ % tq == 0 and tk % CHUNK == 0
    assert tq % (4 * tk) == 0
    assert tk == QUERY_GROUP and QK_LOOKAHEAD <= tq // QUERY_GROUP
    d_ff = w_up.shape[1]
    ff_block = min(FF_BLOCK, d_ff)

    (w_a, w_b, w_gate_p, b_gate_p, w_uqt, w_k, w_vt, qg_col, kg_nope,
     kg_rope) = _pack_weights(w_in, w_gate_up, b_gate, w_uq, w_ukv, q_head_norm, k_head_norm)
    row = lambda v: v.reshape(1, -1).astype(F32)

    tok = lambda w: pl.BlockSpec((1, tm, w), lambda b, i: (b, i, 0))
    tok_t = lambda hgt: pl.BlockSpec((1, hgt, tm), lambda b, i: (b, 0, i))
    consts = (row(attn_norm), w_a, w_b, w_gate_p, b_gate_p, row(q_a_norm), w_uqt, row(kv_a_norm),
              w_k, w_vt, qg_col, kg_nope, kg_rope, invf_col, rope_sel)
    zqk, loga, zv, gate, qt, kk, vt = pl.pallas_call(
        _proj_kernel,
        grid=(b_, s_ // tm),
        in_specs=[tok(d_model), tok_t(1)] + [_const_spec(c.shape) for c in consts],
        out_specs=[tok(2 * GLA_QK_W), tok(GLA_QK_W), tok(GLA_V_W), tok(GLA_V_W),
                   tok_t(MLA_HEADS * HEAD_PAD), tok(MLA_HEADS * HEAD_PAD), tok_t(MLA_HEADS * V_AUG)],
        out_shape=[jax.ShapeDtypeStruct((b_, s_, 2 * GLA_QK_W), F32),
                   jax.ShapeDtypeStruct((b_, s_, GLA_QK_W), F32),
                   jax.ShapeDtypeStruct((b_, s_, GLA_V_W), BF16),
                   jax.ShapeDtypeStruct((b_, s_, GLA_V_W), BF16),
                   jax.ShapeDtypeStruct((b_, MLA_HEADS * HEAD_PAD, s_), BF16),
                   jax.ShapeDtypeStruct((b_, s_, MLA_HEADS * HEAD_PAD), BF16),
                   jax.ShapeDtypeStruct((b_, MLA_HEADS * V_AUG, s_), BF16)],
        compiler_params=pltpu.CompilerParams(dimension_semantics=("parallel", "parallel"),
                                             vmem_limit_bytes=VMEM_LIMIT),
        name="proj",
    )(x, posr, *consts)

    gtok = lambda w: pl.BlockSpec((1, tg, w), lambda b, i: (b, i, 0))
    tri = jnp.asarray(np.tril(np.ones((CHUNK, CHUNK), np.float32)), BF16)
    o_gla = pl.pallas_call(
        functools.partial(_gla_kernel, n_chunks=tg // CHUNK),
        grid=(b_, s_ // tg),
        in_specs=[gtok(2 * GLA_QK_W), gtok(GLA_QK_W), gtok(GLA_V_W), gtok(GLA_V_W),
                  _const_spec((1, GLA_DV)), _const_spec((CHUNK, CHUNK))],
        out_specs=gtok(GLA_V_W),
        out_shape=jax.ShapeDtypeStruct((b_, s_, GLA_V_W), BF16),
        scratch_shapes=[pltpu.VMEM((GLA_DV, GLA_QK_W), F32)],
        compiler_params=pltpu.CompilerParams(dimension_semantics=("parallel", "arbitrary"),
                                             vmem_limit_bytes=VMEM_LIMIT),
        name="gla",
    )(zqk, loga, zv, gate, row(gla_out_norm), tri)

    o_t = pl.pallas_call(
        functools.partial(_attn_kernel, tq=tq, tk=tk),
        grid=(b_, MLA_HEADS, s_ // tq),
        in_specs=[pl.BlockSpec((1, HEAD_PAD, tq), lambda b, h, i: (b, h, i)),
                  pl.BlockSpec((1, s_, HEAD_PAD), lambda b, h, i: (b, 0, h)),
                  pl.BlockSpec((1, V_AUG, s_), lambda b, h, i: (b, h, 0))],
        out_specs=pl.BlockSpec((1, MLA_V, tq), lambda b, h, i: (b, h, i)),
        out_shape=jax.ShapeDtypeStruct((b_, MLA_HEADS * MLA_V, s_), BF16),
        scratch_shapes=[pltpu.VMEM((4, tk, tq), F32), pltpu.VMEM((V_AUG, tq), F32),
                        pltpu.VMEM((1, tq), F32), pltpu.VMEM((4, 1, tq), F32)],
        compiler_params=pltpu.CompilerParams(
            dimension_semantics=("parallel", "parallel", "arbitrary"),
            vmem_limit_bytes=VMEM_LIMIT),
        name="attn",
    )(qt, kk, vt)

    y = pl.pallas_call(
        functools.partial(_mlp_kernel, ff_block=ff_block),
        grid=(b_, s_ // tm),
        in_specs=[tok(d_model), tok(GLA_V_W), tok_t(MLA_HEADS * MLA_V),
                  _const_spec(w_out.shape), _const_spec((1, d_model)),
                  _const_spec(w_up.shape), _const_spec(w_down.shape)],
        out_specs=tok(d_model),
        out_shape=jax.ShapeDtypeStruct((b_, s_, d_model), x.dtype),
        compiler_params=pltpu.CompilerParams(dimension_semantics=("parallel", "parallel"),
                                             vmem_limit_bytes=VMEM_LIMIT),
        name="mlp",
    )(x, o_gla, o_t, w_out.astype(BF16), row(mlp_norm), w_up.astype(BF16), w_down.astype(BF16))
    return y


def kernel(x, positions, attn_norm, w_in, w_gate_up, b_gate, gla_out_norm, q_a_norm, w_uq,
           kv_a_norm, w_ukv, q_head_norm, k_head_norm, w_out, mlp_norm, w_up, w_down):
    b_, s_, _ = x.shape
    posr = positions.reshape(b_, 1, s_)
    inv_freq = ROPE_BASE ** (-jnp.arange(0, MLA_ROPE, 2, dtype=F32) / MLA_ROPE)
    invf_col = inv_freq.reshape(MLA_HALF, 1)
    i = np.arange(MLA_HALF)
    sel = np.zeros((MLA_ROPE, 3 * LANES), np.float32)
    sel[i, MISC_PE + i] = 1.0
    sel[i, MISC_PE + MLA_HALF + i] = 1.0
    sel[MLA_HALF + i, LANES + MISC_PE + i] = -1.0
    sel[MLA_HALF + i, 2 * LANES + MISC_PE + MLA_HALF + i] = 1.0
    rope_sel = jnp.asarray(np.tile(sel, (3, 1)), BF16)
    for l in range(attn_norm.shape[0]):
        x = _layer(x, posr, invf_col, rope_sel, attn_norm[l], w_in[l], w_gate_up[l],
                   b_gate[l], gla_out_norm[l], q_a_norm[l], w_uq[l], kv_a_norm[l], w_ukv[l],
                   q_head_norm[l], k_head_norm[l], w_out[l], mlp_norm[l], w_up[l], w_down[l])
    return x
```

```python
import functools
import math

import jax
import jax.numpy as jnp
import numpy as np
from jax import lax
from jax.experimental import pallas as pl
from jax.experimental.pallas import tpu as pltpu

F32 = jnp.float32
BF16 = jnp.bfloat16
HIGHEST = lax.Precision.HIGHEST

CHUNK = 64
EPS = 1e-6

GLA_HEADS = 4
GLA_DK = 64
GLA_DV = 128
GLA_GATE_RANK = 16
GLA_GATE_NORMALIZER = 16.0
GLA_LOG_GATE_MIN = -1.0
GLA_QK_W = GLA_HEADS * GLA_DK
GLA_V_W = GLA_HEADS * GLA_DV

MLA_HEADS = 8
MLA_Q_RANK = 256
MLA_KV_RANK = 128
MLA_NOPE = 64
MLA_ROPE = 32
MLA_HALF = MLA_ROPE // 2
MLA_QK = MLA_NOPE + MLA_ROPE
MLA_V = 64
ROPE_BASE = 10000.0

LANES = 128
HEAD_PAD = LANES
BF16_SUBLANES = 16
V_AUG = MLA_V + BF16_SUBLANES
LOG2_E = math.log2(math.e)
NEG = -1e30

H_NOPE = MLA_ROPE
A_Q = 0
A_K = A_Q + GLA_QK_W
A_V = A_K + GLA_QK_W
A_W = A_V + GLA_V_W
B_G = 0
B_CQ = B_G + GLA_V_W
B_CKV = B_CQ + MLA_Q_RANK
B_MISC = B_CKV + MLA_KV_RANK
B_W = B_MISC + LANES
MISC_PE = 0
MISC_GATE = MLA_ROPE

PROJ_PIECE = 256
TOKEN_BLOCK = 512
GLA_BLOCK = 512
ATTN_TQ = 4096
ATTN_TK = 256
QUERY_GROUP = 256
QK_LOOKAHEAD = 3
FF_BLOCK = 1024
VMEM_LIMIT = 56 * 1024 * 1024


def _nt(a, b):
    return lax.dot_general(a, b, (((1,), (1,)), ((), ())), preferred_element_type=F32)


def _tn(a, b):
    return lax.dot_general(a, b, (((0,), (0,)), ((), ())), preferred_element_type=F32)


def _rms(v):
    return v * lax.rsqrt(jnp.mean(v * v, axis=-1, keepdims=True) + EPS)


def _proj_kernel(x_ref, posr_ref, g_attn_ref, w_a_ref, w_b_ref, w_gate_ref, b_gate_ref,
                 qa_g_ref, w_uqt_ref, kva_g_ref, w_k_ref, w_vt_ref,
                 qg_col_ref, kg_nope_ref, kg_rope_ref, invf_col_ref, rope_sel_ref,
                 zqk_ref, loga_ref, zv_ref, gate_ref, qt_ref, k_ref, vt_ref):
    hb = (_rms(x_ref[0]) * g_attn_ref[...]).astype(BF16)

    def in_proj(w_ref, c0, c1):
        return jnp.dot(hb, w_ref[:, c0:c1], preferred_element_type=F32)

    zg0 = in_proj(w_b_ref, B_G, B_G + PROJ_PIECE)
    zg1 = in_proj(w_b_ref, B_G + PROJ_PIECE, B_CQ)
    z_mla = in_proj(w_b_ref, B_CQ, B_W)
    gate_ref[0, :, :PROJ_PIECE] = (zg0 * jax.nn.sigmoid(zg0)).astype(BF16)
    gate_ref[0, :, PROJ_PIECE:] = (zg1 * jax.nn.sigmoid(zg1)).astype(BF16)
    zcq = z_mla[:, :B_CKV - B_CQ]
    zckv = z_mla[:, B_CKV - B_CQ:B_MISC - B_CQ]
    misc = z_mla[:, B_MISC - B_CQ:]

    cq = (_rms(zcq) * qa_g_ref[...]).astype(BF16)
    ckv = (_rms(zckv) * kva_g_ref[...]).astype(BF16)
    vt = _nt(w_vt_ref[...], ckv).astype(BF16)
    ang_t = invf_col_ref[...] * posr_ref[0].astype(F32)
    cos_t = jnp.cos(ang_t)
    sin_t = jnp.sin(ang_t)
    cs3 = jnp.concatenate(_split3(jnp.concatenate([cos_t, sin_t], axis=0)), axis=0)
    tabs = _tn(cs3, rope_sel_ref[...])

    logit = jnp.dot(misc, w_gate_ref[...], preferred_element_type=F32, precision=HIGHEST)
    logit = logit + b_gate_ref[...]
    log_sig = jnp.minimum(logit, 0.0) - jnp.log1p(jnp.exp(-jnp.abs(logit)))
    loga_ref[0] = jnp.maximum(log_sig / GLA_GATE_NORMALIZER, GLA_LOG_GATE_MIN)

    ones_rows = jnp.ones((V_AUG - MLA_V, vt.shape[1]), BF16)
    for hh in range(MLA_HEADS):
        vt_ref[0, hh * V_AUG:hh * V_AUG + MLA_V, :] = vt[hh * MLA_V:(hh + 1) * MLA_V, :]
        vt_ref[0, hh * V_AUG + MLA_V:(hh + 1) * V_AUG, :] = ones_rows

    qscale = MLA_QK ** -0.5 * LOG2_E
    r1 = MLA_HALF
    r2 = H_NOPE
    r3 = H_NOPE + MLA_NOPE

    def q_head(hh, blk):
        base = hh * HEAD_PAD
        ssq = jnp.sum(blk * blk, axis=0, keepdims=True)
        rs = lax.rsqrt(ssq * (1.0 / MLA_QK) + EPS) * qscale
        qn = blk * rs * qg_col_ref[...]
        x1 = qn[:r1, :]
        x2 = qn[r1:r2, :]
        qt_ref[0, base:base + r1, :] = (x1 * cos_t - x2 * sin_t).astype(BF16)
        qt_ref[0, base + r1:base + r2, :] = (x1 * sin_t + x2 * cos_t).astype(BF16)
        qt_ref[0, base + r2:base + r3, :] = qn[r2:r3, :].astype(BF16)
        qt_ref[0, base + r3:base + HEAD_PAD, :] = jnp.zeros((HEAD_PAD - r3, blk.shape[1]), BF16)

    lane = lax.broadcasted_iota(jnp.int32, (1, LANES), 1)
    is_pe = (lane >= MISC_PE) & (lane < MISC_PE + MLA_ROPE)
    kpe = jnp.where(is_pe, misc, 0.0)
    ssq_pe = jnp.sum(kpe * kpe, axis=-1, keepdims=True)
    xg = kpe * kg_rope_ref[...]
    rot = (xg * tabs[:, :LANES]
           + pltpu.roll(xg, LANES - MLA_HALF, 1) * tabs[:, LANES:2 * LANES]
           + pltpu.roll(xg, MLA_HALF, 1) * tabs[:, 2 * LANES:])

    def k_head(hh, kn_h):
        base = hh * HEAD_PAD
        ssq = jnp.sum(kn_h * kn_h, axis=-1, keepdims=True) + ssq_pe
        rs = lax.rsqrt(ssq * (1.0 / MLA_QK) + EPS)
        k_ref[0, :, base:base + HEAD_PAD] = ((kn_h * kg_nope_ref[...] + rot) * rs).astype(BF16)

    pieces = [(zqk_ref, 0, A_Q), (zqk_ref, PROJ_PIECE, A_K),
              (zv_ref, 0, A_V), (zv_ref, PROJ_PIECE, A_V + PROJ_PIECE)]
    group = MLA_HEADS // len(pieces)
    for i, (ref, lane0, c0) in enumerate(pieces):
        rows = slice(i * group * HEAD_PAD, (i + 1) * group * HEAD_PAD)
        ref[0, :, lane0:lane0 + PROJ_PIECE] = in_proj(w_a_ref, c0, c0 + PROJ_PIECE).astype(ref.dtype)
        qt = _nt(w_uqt_ref[rows, :], cq)
        kn = jnp.dot(ckv, w_k_ref[:, rows], preferred_element_type=F32)
        for n in range(group):
            q_head(i * group + n, qt[n * HEAD_PAD:(n + 1) * HEAD_PAD, :])
            k_head(i * group + n, kn[:, n * HEAD_PAD:(n + 1) * HEAD_PAD])


def _split3(v):
    hi = v.astype(BF16)
    r1 = v - hi.astype(F32)
    mid = r1.astype(BF16)
    lo = (r1 - mid.astype(F32)).astype(BF16)
    return hi, mid, lo


def _gla_kernel(zqk_ref, loga_ref, zv_ref, gate_ref, gout_ref, tri_ref, w_up_ref, w_dn_ref,
                o_ref, w_up_bf_ref, w_dn_bf_ref, st_ref, *, n_chunks):
    w_up_bf_ref[...] = w_up_ref[...].astype(BF16)
    w_dn_bf_ref[...] = w_dn_ref[...].astype(BF16)

    @pl.when(pl.program_id(1) == 0)
    def _():
        st_ref[...] = jnp.zeros(st_ref.shape, F32)

    sr = lax.broadcasted_iota(jnp.int32, (GLA_HEADS * CHUNK, CHUNK), 0)
    sc_ = lax.broadcasted_iota(jnp.int32, (GLA_HEADS * CHUNK, CHUNK), 1)
    causal4 = (sr % CHUNK) >= sc_
    qr = lax.broadcasted_iota(jnp.int32, (GLA_HEADS * CHUNK, GLA_QK_W), 0)
    qc = lax.broadcasted_iota(jnp.int32, (GLA_HEADS * CHUNK, GLA_QK_W), 1)
    head_sel = (qr // CHUNK) == (qc // GLA_DK)
    lane_head = lax.broadcasted_iota(jnp.int32, (GLA_DV, GLA_QK_W), 1) // GLA_DK
    gout = gout_ref[...]
    chunks = [slice(ci * CHUNK, (ci + 1) * CHUNK) for ci in range(n_chunks)]

    la = loga_ref[0]
    parts = [p for r in chunks for p in _split3(la[r])]
    tri_out = jnp.dot(tri_ref[...], jnp.concatenate(parts, axis=1), preferred_element_type=F32)
    w = GLA_QK_W
    cum = [tri_out[:, (3 * ci) * w:(3 * ci + 1) * w] + tri_out[:, (3 * ci + 1) * w:(3 * ci + 2) * w]
           + tri_out[:, (3 * ci + 2) * w:(3 * ci + 3) * w] for ci in range(n_chunks)]
    cl = [c[CHUNK - 1:CHUNK, :] for c in cum]

    def prep(ci):
        r = chunks[ci]
        q = zqk_ref[0, r, 0:GLA_QK_W]
        k = zqk_ref[0, r, GLA_QK_W:2 * GLA_QK_W]
        qd = (q * (GLA_DK ** -0.5) * jnp.exp(cum[ci])).astype(BF16)
        k_inv = (k * jnp.exp(-cum[ci])).astype(BF16)
        k_end = (k * jnp.exp(cl[ci] - cum[ci])).astype(BF16)
        q_stack = jnp.where(head_sel, jnp.concatenate([qd] * GLA_HEADS, axis=0), 0.0)
        return q_stack, k_inv, k_end

    def intra_scores(ci):
        return jnp.where(causal4, _nt(pre[ci][0], pre[ci][1]), 0.0).astype(BF16)

    def intra_out_and_kv(ci):
        v = zv_ref[0, chunks[ci], :]
        full = [jnp.dot(scores[ci][hh * CHUNK:(hh + 1) * CHUNK, :],
                        v[:, hh * GLA_DV:(hh + 1) * GLA_DV], preferred_element_type=F32)
                for hh in range(GLA_HEADS)]
        kv_t = _tn(v, pre[ci][2])
        kv = kv_t[(GLA_HEADS - 1) * GLA_DV:, :]
        for hh in range(GLA_HEADS - 2, -1, -1):
            kv = jnp.where(lane_head == hh, kv_t[hh * GLA_DV:(hh + 1) * GLA_DV, :], kv)
        return full, kv

    pre, scores, intra = {}, {}, {}
    for t in range(n_chunks + 2):
        if t < n_chunks:
            pre[t] = prep(t)
        if 0 <= t - 1 < n_chunks:
            scores[t - 1] = intra_scores(t - 1)
        if 0 <= t - 2 < n_chunks:
            intra[t - 2] = intra_out_and_kv(t - 2)

    st = st_ref[...]
    st_in = []
    for ci in range(n_chunks):
        st_in.append(st.astype(BF16))
        st = st * jnp.exp(cl[ci]) + intra[ci][1]
    st_ref[...] = st

    def finish(ci, inter):
        r = chunks[ci]
        for hh in range(GLA_HEADS):
            cols = slice(hh * GLA_DV, (hh + 1) * GLA_DV)
            o_h = inter[hh * CHUNK:(hh + 1) * CHUNK, :] + intra[ci][0][hh]
            g_h = gate_ref[0, r, cols].astype(F32)
            o_ref[0, r, cols] = (_rms(o_h) * gout * g_h).astype(BF16)

    inter = {}
    for t in range(n_chunks + 1):
        if t < n_chunks:
            inter[t] = _nt(pre[t][0], st_in[t])
        if t >= 1:
            finish(t - 1, inter[t - 1])


def _attn_kernel(qt_ref, k_ref, vt_ref, o_ref, s_ref, acc_ref, m_ref, bm_ref, *, tq, tk):
    qi = pl.program_id(2)
    ndiag = tq // tk

    m_ref[...] = jnp.full(m_ref.shape, NEG, F32)
    acc_ref[...] = jnp.zeros(acc_ref.shape, F32)

    kr = lax.broadcasted_iota(jnp.int32, (tk, tk), 0)
    qc = lax.broadcasted_iota(jnp.int32, (tk, tk), 1)
    square_mask = (kr // CHUNK) <= (qc // CHUNK)

    def qk(j, slot, c0=0, c1=tq):
        start = pl.multiple_of(j * tk, tk)
        s = jnp.dot(k_ref[0, pl.ds(start, tk), :], qt_ref[0, :, c0:c1],
                    preferred_element_type=F32)
        s_ref[slot, :, c0:c1] = s
        bm_ref[slot, :, c0:c1] = jnp.max(s, axis=0, keepdims=True)

    def mask_square(slot, c0):
        cols = slice(c0, c0 + tk)
        s = jnp.where(square_mask, s_ref[slot, :, cols], NEG)
        s_ref[slot, :, cols] = s
        bm_ref[slot, :, cols] = jnp.max(s, axis=0, keepdims=True)

    def softmax_pv(j, slots, c0=0, c1=tq):
        m_old = m_ref[:, c0:c1]
        m_new = m_old
        for slot in slots:
            m_new = jnp.maximum(m_new, bm_ref[slot, :, c0:c1])
        m_ref[:, c0:c1] = m_new
        p = jnp.concatenate([jnp.exp2(s_ref[slot, :, c0:c1] - m_new).astype(BF16)
                             for slot in slots], axis=0)
        start = pl.multiple_of(j * tk, tk)
        pv = jnp.dot(vt_ref[0, :, pl.ds(start, len(slots) * tk)], p,
                     preferred_element_type=F32)
        acc_ref[:, c0:c1] = jnp.exp2(m_old - m_new) * acc_ref[:, c0:c1] + pv

    ngroups = tq // QUERY_GROUP
    per_iter = 2 * ngroups

    def cols(g):
        return g * QUERY_GROUP, (g + 1) * QUERY_GROUP

    def qk_item(j, ps, g, both=True):
        qk(j, 2 * ps, *cols(g))
        if both:
            qk(j + 1, 2 * ps + 1, *cols(g))

    for u in range(QK_LOOKAHEAD):
        qk_item(0, 0, u)

    def body(i, carry):
        for u in range(per_iter):
            v = u + QK_LOOKAHEAD
            if v < per_iter:
                qk_item(4 * i + 2 * (v // ngroups), v // ngroups, v % ngroups)
            else:
                qk_item(4 * i + 4, 0, v - per_iter)
            ps, g = divmod(u, ngroups)
            softmax_pv(4 * i + 2 * ps, (2 * ps, 2 * ps + 1), *cols(g))
        return carry

    lax.fori_loop(0, qi * (ndiag // 4), body, 0)

    nfull = qi * ndiag
    items = [(p, g) for p in range(ndiag // 2) for g in range(2 * p, ngroups)]
    for t, (p, g) in enumerate(items):
        if t + QK_LOOKAHEAD < len(items):
            p2, g2 = items[t + QK_LOOKAHEAD]
            qk_item(nfull + 2 * p2, p2 % 2, g2, both=(g2 != 2 * p2))
        d = 2 * p
        slots = (2 * (p % 2), 2 * (p % 2) + 1)
        if g == d:
            mask_square(slots[0], d * tk)
            softmax_pv(nfull + d, slots[:1], *cols(g))
        else:
            if g == d + 1:
                mask_square(slots[1], (d + 1) * tk)
            softmax_pv(nfull + d, slots, *cols(g))

    acc = acc_ref[...]
    o_ref[0] = (acc[:MLA_V, :] / acc[MLA_V:MLA_V + 1, :]).astype(BF16)


def _mlp_kernel(x_ref, og_ref, ot_ref, w_out_ref, g_mlp_ref, w_up_ref, w_dn_ref,
                y_ref, *, ff_block):
    mix = jnp.dot(og_ref[0], w_out_ref[:GLA_V_W, :], preferred_element_type=F32)
    mix = mix + _tn(ot_ref[0], w_out_ref[GLA_V_W:, :])
    x1 = x_ref[0] + mix
    h = (_rms(x1) * g_mlp_ref[...]).astype(BF16)
    acc = x1
    d_ff = w_up_ref.shape[1]
    for f in range(d_ff // ff_block):
        cols = slice(f * ff_block, (f + 1) * ff_block)
        u = jnp.dot(h, w_up_ref[:, cols], preferred_element_type=F32)
        a = jnp.square(jnp.maximum(u, 0.0)).astype(BF16)
        acc = acc + jnp.dot(a, w_dn_ref[cols, :], preferred_element_type=F32)
    y_ref[0] = acc


def _const_spec(shape):
    nd = len(shape)
    return pl.BlockSpec(shape, lambda *_: (0,) * nd)


def _pack_weights(w_in, w_gate_up, b_gate, w_uq, w_ukv, q_head_norm, k_head_norm):
    o_gate = 2 * GLA_QK_W + GLA_V_W
    o_rest = o_gate + GLA_GATE_RANK
    rest = w_in[:, o_rest:]
    assert A_W == o_gate and rest.shape[1] == B_MISC + MLA_ROPE and MISC_PE == 0
    w_a = w_in[:, :o_gate].astype(BF16)
    gate_at = B_MISC + MISC_GATE
    w_b = (jnp.pad(rest, ((0, 0), (0, B_W - rest.shape[1])))
           + jnp.pad(w_in[:, o_gate:o_rest], ((0, 0), (gate_at, B_W - gate_at - GLA_GATE_RANK)))
           ).astype(BF16)

    w_gate_p = jnp.pad(w_gate_up.astype(F32),
                       ((MISC_GATE, LANES - MISC_GATE - GLA_GATE_RANK), (0, 0)))
    b_gate_p = b_gate.reshape(1, GLA_QK_W).astype(F32)

    w_uq_h = w_uq.reshape(MLA_Q_RANK, MLA_HEADS, MLA_QK)
    w_uq_h = jnp.concatenate([w_uq_h[:, :, MLA_NOPE:], w_uq_h[:, :, :MLA_NOPE]], axis=2)
    w_uq_h = jnp.pad(w_uq_h, ((0, 0), (0, 0), (0, HEAD_PAD - MLA_QK)))
    w_uqt = w_uq_h.reshape(MLA_Q_RANK, MLA_HEADS * HEAD_PAD).T.astype(BF16)

    nope_pad = (H_NOPE, HEAD_PAD - H_NOPE - MLA_NOPE)
    w_ukv_h = w_ukv.reshape(MLA_KV_RANK, MLA_HEADS, MLA_NOPE + MLA_V)
    w_k = jnp.pad(w_ukv_h[:, :, :MLA_NOPE], ((0, 0), (0, 0), nope_pad))
    w_k = w_k.reshape(MLA_KV_RANK, MLA_HEADS * HEAD_PAD).astype(BF16)
    w_vt = w_ukv_h[:, :, MLA_NOPE:].reshape(MLA_KV_RANK, MLA_HEADS * MLA_V).T.astype(BF16)

    qg = jnp.concatenate([q_head_norm[MLA_NOPE:], q_head_norm[:MLA_NOPE]])
    qg_col = jnp.pad(qg, (0, HEAD_PAD - MLA_QK)).reshape(HEAD_PAD, 1).astype(F32)
    kg_nope = jnp.pad(k_head_norm[:MLA_NOPE], nope_pad).reshape(1, LANES).astype(F32)
    kg_rope = jnp.pad(k_head_norm[MLA_NOPE:], (MISC_PE, LANES - MISC_PE - MLA_ROPE))
    kg_rope = kg_rope.reshape(1, LANES).astype(F32)
    return w_a, w_b, w_gate_p, b_gate_p, w_uqt, w_k, w_vt, qg_col, kg_nope, kg_rope


def _layer(x, posr, invf_col, rope_sel, attn_norm, w_in, w_gate_up, b_gate, gla_out_norm,
           q_a_norm, w_uq, kv_a_norm, w_ukv, q_head_norm, k_head_norm, w_out, mlp_norm, w_up,
           w_down):
    b_, s_, d_model = x.shape
    tm = min(TOKEN_BLOCK, s_)
    tg = min(GLA_BLOCK, s_)
    tq = min(ATTN_TQ, s_)
    tk = min(ATTN_TK, tq)
    assert s_ % tm == 0 and s_ % tg == 0 and s_ % tq == 0 and tk % CHUNK == 0
    assert tq % (4 * tk) == 0
    assert tk == QUERY_GROUP and QK_LOOKAHEAD <= tq // QUERY_GROUP
    d_ff = w_up.shape[1]
    ff_block = min(FF_BLOCK, d_ff)

    (w_a, w_b, w_gate_p, b_gate_p, w_uqt, w_k, w_vt, qg_col, kg_nope,
     kg_rope) = _pack_weights(w_in, w_gate_up, b_gate, w_uq, w_ukv, q_head_norm, k_head_norm)
    row = lambda v: v.reshape(1, -1).astype(F32)

    tok = lambda w: pl.BlockSpec((1, tm, w), lambda b, i: (b, i, 0))
    tok_t = lambda hgt: pl.BlockSpec((1, hgt, tm), lambda b, i: (b, 0, i))
    consts = (row(attn_norm), w_a, w_b, w_gate_p, b_gate_p, row(q_a_norm), w_uqt, row(kv_a_norm),
              w_k, w_vt, qg_col, kg_nope, kg_rope, invf_col, rope_sel)
    zqk, loga, zv, gate, qt, kk, vt = pl.pallas_call(
        _proj_kernel,
        grid=(b_, s_ // tm),
        in_specs=[tok(d_model), tok_t(1)] + [_const_spec(c.shape) for c in consts],
        out_specs=[tok(2 * GLA_QK_W), tok(GLA_QK_W), tok(GLA_V_W), tok(GLA_V_W),
                   tok_t(MLA_HEADS * HEAD_PAD), tok(MLA_HEADS * HEAD_PAD), tok_t(MLA_HEADS * V_AUG)],
        out_shape=[jax.ShapeDtypeStruct((b_, s_, 2 * GLA_QK_W), F32),
                   jax.ShapeDtypeStruct((b_, s_, GLA_QK_W), F32),
                   jax.ShapeDtypeStruct((b_, s_, GLA_V_W), BF16),
                   jax.ShapeDtypeStruct((b_, s_, GLA_V_W), BF16),
                   jax.ShapeDtypeStruct((b_, MLA_HEADS * HEAD_PAD, s_), BF16),
                   jax.ShapeDtypeStruct((b_, s_, MLA_HEADS * HEAD_PAD), BF16),
                   jax.ShapeDtypeStruct((b_, MLA_HEADS * V_AUG, s_), BF16)],
        compiler_params=pltpu.CompilerParams(dimension_semantics=("parallel", "parallel"),
                                             vmem_limit_bytes=VMEM_LIMIT),
        name="proj",
    )(x, posr, *consts)

    gtok = lambda w: pl.BlockSpec((1, tg, w), lambda b, i: (b, i, 0))
    tri = jnp.asarray(np.tril(np.ones((CHUNK, CHUNK), np.float32)), BF16)
    n_gla = s_ // tg
    n_steps = b_ * n_gla
    assert d_model % (n_steps * BF16_SUBLANES) == 0 and d_ff % (n_steps * BF16_SUBLANES) == 0
    slab = lambda rows, w: pl.BlockSpec((rows // n_steps, w), lambda b, i: (b * n_gla + i, 0))
    o_gla, w_up_bf, w_dn_bf = pl.pallas_call(
        functools.partial(_gla_kernel, n_chunks=tg // CHUNK),
        grid=(b_, n_gla),
        in_specs=[gtok(2 * GLA_QK_W), gtok(GLA_QK_W), gtok(GLA_V_W), gtok(GLA_V_W),
                  _const_spec((1, GLA_DV)), _const_spec((CHUNK, CHUNK)),
                  slab(d_model, d_ff), slab(d_ff, d_model)],
        out_specs=[gtok(GLA_V_W), slab(d_model, d_ff), slab(d_ff, d_model)],
        out_shape=[jax.ShapeDtypeStruct((b_, s_, GLA_V_W), BF16),
                   jax.ShapeDtypeStruct((d_model, d_ff), BF16),
                   jax.ShapeDtypeStruct((d_ff, d_model), BF16)],
        scratch_shapes=[pltpu.VMEM((GLA_DV, GLA_QK_W), F32)],
        compiler_params=pltpu.CompilerParams(dimension_semantics=("parallel", "arbitrary"),
                                             vmem_limit_bytes=VMEM_LIMIT),
        name="gla",
    )(zqk, loga, zv, gate, row(gla_out_norm), tri, w_up, w_down)

    o_t = pl.pallas_call(
        functools.partial(_attn_kernel, tq=tq, tk=tk),
        grid=(b_, MLA_HEADS, s_ // tq),
        in_specs=[pl.BlockSpec((1, HEAD_PAD, tq), lambda b, h, i: (b, h, i)),
                  pl.BlockSpec((1, s_, HEAD_PAD), lambda b, h, i: (b, 0, h)),
                  pl.BlockSpec((1, V_AUG, s_), lambda b, h, i: (b, h, 0))],
        out_specs=pl.BlockSpec((1, MLA_V, tq), lambda b, h, i: (b, h, i)),
        out_shape=jax.ShapeDtypeStruct((b_, MLA_HEADS * MLA_V, s_), BF16),
        scratch_shapes=[pltpu.VMEM((4, tk, tq), F32), pltpu.VMEM((V_AUG, tq), F32),
                        pltpu.VMEM((1, tq), F32), pltpu.VMEM((4, 1, tq), F32)],
        compiler_params=pltpu.CompilerParams(
            dimension_semantics=("parallel", "parallel", "arbitrary"),
            vmem_limit_bytes=VMEM_LIMIT),
        name="attn",
    )(qt, kk, vt)

    y = pl.pallas_call(
        functools.partial(_mlp_kernel, ff_block=ff_block),
        grid=(b_, s_ // tm),
        in_specs=[tok(d_model), tok(GLA_V_W), tok_t(MLA_HEADS * MLA_V),
                  _const_spec(w_out.shape), _const_spec((1, d_model)),
                  _const_spec(w_up.shape), _const_spec(w_down.shape)],
        out_specs=tok(d_model),
        out_shape=jax.ShapeDtypeStruct((b_, s_, d_model), x.dtype),
        compiler_params=pltpu.CompilerParams(dimension_semantics=("parallel", "parallel"),
                                             vmem_limit_bytes=VMEM_LIMIT),
        name="mlp",
    )(x, o_gla, o_t, w_out.astype(BF16), row(mlp_norm), w_up_bf, w_dn_bf)
    return y


def kernel(x, positions, attn_norm, w_in, w_gate_up, b_gate, gla_out_norm, q_a_norm, w_uq,
           kv_a_norm, w_ukv, q_head_norm, k_head_norm, w_out, mlp_norm, w_up, w_down):
    b_, s_, _ = x.shape
    posr = positions.reshape(b_, 1, s_)
    inv_freq = ROPE_BASE ** (-jnp.arange(0, MLA_ROPE, 2, dtype=F32) / MLA_ROPE)
    invf_col = inv_freq.reshape(MLA_HALF, 1)
    i = np.arange(MLA_HALF)
    sel = np.zeros((MLA_ROPE, 3 * LANES), np.float32)
    sel[i, MISC_PE + i] = 1.0
    sel[i, MISC_PE + MLA_HALF + i] = 1.0
    sel[MLA_HALF + i, LANES + MISC_PE + i] = -1.0
    sel[MLA_HALF + i, 2 * LANES + MISC_PE + MLA_HALF + i] = 1.0
    rope_sel = jnp.asarray(np.tile(sel, (3, 1)), BF16)
    for l in range(attn_norm.shape[0]):
        x = _layer(x, posr, invf_col, rope_sel, attn_norm[l], w_in[l], w_gate_up[l],
                   b_gate[l], gla_out_norm[l], q_a_norm[l], w_uq[l], kv_a_norm[l], w_ukv[l],
                   q_head_norm[l], k_head_norm[l], w_out[l], mlp_norm[l], w_up[l], w_down[l])
    return x
```

```python
import functools
import math

import jax
import jax.numpy as jnp
import numpy as np
from jax import lax
from jax.experimental import pallas as pl
from jax.experimental.pallas import tpu as pltpu

F32 = jnp.float32
BF16 = jnp.bfloat16
HIGHEST = lax.Precision.HIGHEST

CHUNK = 64
EPS = 1e-6

GLA_HEADS = 4
GLA_DK = 64
GLA_DV = 128
GLA_GATE_RANK = 16
GLA_GATE_NORMALIZER = 16.0
GLA_LOG_GATE_MIN = -1.0
GLA_QK_W = GLA_HEADS * GLA_DK
GLA_V_W = GLA_HEADS * GLA_DV

MLA_HEADS = 8
MLA_Q_RANK = 256
MLA_KV_RANK = 128
MLA_NOPE = 64
MLA_ROPE = 32
MLA_HALF = MLA_ROPE // 2
MLA_QK = MLA_NOPE + MLA_ROPE
MLA_V = 64
ROPE_BASE = 10000.0

LANES = 128
HEAD_PAD = LANES
BF16_SUBLANES = 16
V_AUG = MLA_V + BF16_SUBLANES
LOG2_E = math.log2(math.e)
NEG = -1e30

H_NOPE = MLA_ROPE
A_Q = 0
A_K = A_Q + GLA_QK_W
A_V = A_K + GLA_QK_W
A_W = A_V + GLA_V_W
B_G = 0
B_CQ = B_G + GLA_V_W
B_CKV = B_CQ + MLA_Q_RANK
B_MISC = B_CKV + MLA_KV_RANK
B_W = B_MISC + LANES
MISC_PE = 0
MISC_GATE = MLA_ROPE

PROJ_PIECE = 256
TOKEN_BLOCK = 512
GLA_BLOCK = 512
ATTN_TQ = 4096
ATTN_TK = 256
QUERY_GROUP = 256
QK_LOOKAHEAD = 3
FF_BLOCK = 1024
VMEM_LIMIT = 56 * 1024 * 1024


def _nt(a, b):
    return lax.dot_general(a, b, (((1,), (1,)), ((), ())), preferred_element_type=F32)


def _tn(a, b):
    return lax.dot_general(a, b, (((0,), (0,)), ((), ())), preferred_element_type=F32)


def _rms(v):
    return v * lax.rsqrt(jnp.mean(v * v, axis=-1, keepdims=True) + EPS)


def _proj_kernel(x_ref, posr_ref, g_attn_ref, w_a_ref, w_b_ref, w_gate_ref, b_gate_ref,
                 qa_g_ref, w_uqt_ref, kva_g_ref, w_k_ref, w_vt_ref,
                 qg_col_ref, kg_nope_ref, kg_rope_ref, invf_col_ref, rope_sel_ref,
                 zqk_ref, loga_ref, zv_ref, gate_ref, qt_ref, k_ref, vt_ref):
    hb = (_rms(x_ref[0]) * g_attn_ref[...]).astype(BF16)

    def in_proj(w_ref, c0, c1):
        return jnp.dot(hb, w_ref[:, c0:c1].astype(BF16), preferred_element_type=F32)

    zg0 = in_proj(w_b_ref, B_G, B_G + PROJ_PIECE)
    zg1 = in_proj(w_b_ref, B_G + PROJ_PIECE, B_CQ)
    z_mla = in_proj(w_b_ref, B_CQ, B_W)
    gate_ref[0, :, :PROJ_PIECE] = (zg0 * jax.nn.sigmoid(zg0)).astype(BF16)
    gate_ref[0, :, PROJ_PIECE:] = (zg1 * jax.nn.sigmoid(zg1)).astype(BF16)
    zcq = z_mla[:, :B_CKV - B_CQ]
    zckv = z_mla[:, B_CKV - B_CQ:B_MISC - B_CQ]
    misc = z_mla[:, B_MISC - B_CQ:]

    cq = (_rms(zcq) * qa_g_ref[...]).astype(BF16)
    ckv = (_rms(zckv) * kva_g_ref[...]).astype(BF16)
    vt = _nt(w_vt_ref[...], ckv).astype(BF16)
    ang_t = invf_col_ref[...] * posr_ref[0].astype(F32)
    cos_t = jnp.cos(ang_t)
    sin_t = jnp.sin(ang_t)
    cs3 = jnp.concatenate(_split3(jnp.concatenate([cos_t, sin_t], axis=0)), axis=0)
    tabs = _tn(cs3, rope_sel_ref[...])

    logit = jnp.dot(misc, w_gate_ref[...], preferred_element_type=F32, precision=HIGHEST)
    logit = logit + b_gate_ref[...]
    log_sig = jnp.minimum(logit, 0.0) - jnp.log1p(jnp.exp(-jnp.abs(logit)))
    loga_ref[0] = jnp.maximum(log_sig / GLA_GATE_NORMALIZER, GLA_LOG_GATE_MIN)

    ones_rows = jnp.ones((V_AUG - MLA_V, vt.shape[1]), BF16)
    for hh in range(MLA_HEADS):
        vt_ref[0, hh * V_AUG:hh * V_AUG + MLA_V, :] = vt[hh * MLA_V:(hh + 1) * MLA_V, :]
        vt_ref[0, hh * V_AUG + MLA_V:(hh + 1) * V_AUG, :] = ones_rows

    qscale = MLA_QK ** -0.5 * LOG2_E
    r1 = MLA_HALF
    r2 = H_NOPE
    r3 = H_NOPE + MLA_NOPE

    def q_head(hh, blk):
        base = hh * HEAD_PAD
        ssq = jnp.sum(blk * blk, axis=0, keepdims=True)
        rs = lax.rsqrt(ssq * (1.0 / MLA_QK) + EPS) * qscale
        qn = blk * rs * qg_col_ref[...]
        x1 = qn[:r1, :]
        x2 = qn[r1:r2, :]
        qt_ref[0, base:base + r1, :] = (x1 * cos_t - x2 * sin_t).astype(BF16)
        qt_ref[0, base + r1:base + r2, :] = (x1 * sin_t + x2 * cos_t).astype(BF16)
        qt_ref[0, base + r2:base + r3, :] = qn[r2:r3, :].astype(BF16)
        qt_ref[0, base + r3:base + HEAD_PAD, :] = jnp.zeros((HEAD_PAD - r3, blk.shape[1]), BF16)

    lane = lax.broadcasted_iota(jnp.int32, (1, LANES), 1)
    is_pe = (lane >= MISC_PE) & (lane < MISC_PE + MLA_ROPE)
    kpe = jnp.where(is_pe, misc, 0.0)
    ssq_pe = jnp.sum(kpe * kpe, axis=-1, keepdims=True)
    xg = kpe * kg_rope_ref[...]
    rot = (xg * tabs[:, :LANES]
           + pltpu.roll(xg, LANES - MLA_HALF, 1) * tabs[:, LANES:2 * LANES]
           + pltpu.roll(xg, MLA_HALF, 1) * tabs[:, 2 * LANES:])

    def k_head(hh, kn_h):
        base = hh * HEAD_PAD
        ssq = jnp.sum(kn_h * kn_h, axis=-1, keepdims=True) + ssq_pe
        rs = lax.rsqrt(ssq * (1.0 / MLA_QK) + EPS)
        k_ref[0, :, base:base + HEAD_PAD] = ((kn_h * kg_nope_ref[...] + rot) * rs).astype(BF16)

    pieces = [(zqk_ref, 0, A_Q), (zqk_ref, PROJ_PIECE, A_K),
              (zv_ref, 0, A_V), (zv_ref, PROJ_PIECE, A_V + PROJ_PIECE)]
    group = MLA_HEADS // len(pieces)
    for i, (ref, lane0, c0) in enumerate(pieces):
        rows = slice(i * group * HEAD_PAD, (i + 1) * group * HEAD_PAD)
        ref[0, :, lane0:lane0 + PROJ_PIECE] = in_proj(w_a_ref, c0, c0 + PROJ_PIECE).astype(ref.dtype)
        qt = _nt(w_uqt_ref[rows, :], cq)
        kn = jnp.dot(ckv, w_k_ref[:, rows], preferred_element_type=F32)
        for n in range(group):
            q_head(i * group + n, qt[n * HEAD_PAD:(n + 1) * HEAD_PAD, :])
            k_head(i * group + n, kn[:, n * HEAD_PAD:(n + 1) * HEAD_PAD])


def _split3(v):
    hi = v.astype(BF16)
    r1 = v - hi.astype(F32)
    mid = r1.astype(BF16)
    lo = (r1 - mid.astype(F32)).astype(BF16)
    return hi, mid, lo


def _gla_kernel(zqk_ref, loga_ref, zv_ref, gate_ref, gout_ref, tri_ref, w_up_ref, w_dn_ref,
                o_ref, w_up_bf_ref, w_dn_bf_ref, st_ref, *, n_chunks):
    w_up_bf_ref[...] = w_up_ref[...].astype(BF16)
    w_dn_bf_ref[...] = w_dn_ref[...].astype(BF16)

    @pl.when(pl.program_id(1) == 0)
    def _():
        st_ref[...] = jnp.zeros(st_ref.shape, F32)

    sr = lax.broadcasted_iota(jnp.int32, (GLA_HEADS * CHUNK, CHUNK), 0)
    sc_ = lax.broadcasted_iota(jnp.int32, (GLA_HEADS * CHUNK, CHUNK), 1)
    causal4 = (sr % CHUNK) >= sc_
    qr = lax.broadcasted_iota(jnp.int32, (GLA_HEADS * CHUNK, GLA_QK_W), 0)
    qc = lax.broadcasted_iota(jnp.int32, (GLA_HEADS * CHUNK, GLA_QK_W), 1)
    head_sel = (qr // CHUNK) == (qc // GLA_DK)
    lane_head = lax.broadcasted_iota(jnp.int32, (GLA_DV, GLA_QK_W), 1) // GLA_DK
    gout = gout_ref[...]
    chunks = [slice(ci * CHUNK, (ci + 1) * CHUNK) for ci in range(n_chunks)]

    la = loga_ref[0]
    parts = [p for r in chunks for p in _split3(la[r])]
    tri_out = jnp.dot(tri_ref[...], jnp.concatenate(parts, axis=1), preferred_element_type=F32)
    w = GLA_QK_W
    cum = [tri_out[:, (3 * ci) * w:(3 * ci + 1) * w] + tri_out[:, (3 * ci + 1) * w:(3 * ci + 2) * w]
           + tri_out[:, (3 * ci + 2) * w:(3 * ci + 3) * w] for ci in range(n_chunks)]
    cl = [c[CHUNK - 1:CHUNK, :] for c in cum]

    def prep(ci):
        r = chunks[ci]
        q = zqk_ref[0, r, 0:GLA_QK_W]
        k = zqk_ref[0, r, GLA_QK_W:2 * GLA_QK_W]
        qd = (q * (GLA_DK ** -0.5) * jnp.exp(cum[ci])).astype(BF16)
        k_inv = (k * jnp.exp(-cum[ci])).astype(BF16)
        k_end = (k * jnp.exp(cl[ci] - cum[ci])).astype(BF16)
        q_stack = jnp.where(head_sel, jnp.concatenate([qd] * GLA_HEADS, axis=0), 0.0)
        return q_stack, k_inv, k_end

    def intra_scores(ci):
        return jnp.where(causal4, _nt(pre[ci][0], pre[ci][1]), 0.0).astype(BF16)

    def intra_out_and_kv(ci):
        v = zv_ref[0, chunks[ci], :]
        full = [jnp.dot(scores[ci][hh * CHUNK:(hh + 1) * CHUNK, :],
                        v[:, hh * GLA_DV:(hh + 1) * GLA_DV], preferred_element_type=F32)
                for hh in range(GLA_HEADS)]
        kv_t = _tn(v, pre[ci][2])
        kv = kv_t[(GLA_HEADS - 1) * GLA_DV:, :]
        for hh in range(GLA_HEADS - 2, -1, -1):
            kv = jnp.where(lane_head == hh, kv_t[hh * GLA_DV:(hh + 1) * GLA_DV, :], kv)
        return full, kv

    pre, scores, intra = {}, {}, {}
    for t in range(n_chunks + 2):
        if t < n_chunks:
            pre[t] = prep(t)
        if 0 <= t - 1 < n_chunks:
            scores[t - 1] = intra_scores(t - 1)
        if 0 <= t - 2 < n_chunks:
            intra[t - 2] = intra_out_and_kv(t - 2)

    st = st_ref[...]
    st_in = []
    for ci in range(n_chunks):
        st_in.append(st.astype(BF16))
        st = st * jnp.exp(cl[ci]) + intra[ci][1]
    st_ref[...] = st

    def finish(ci, inter):
        r = chunks[ci]
        for hh in range(GLA_HEADS):
            cols = slice(hh * GLA_DV, (hh + 1) * GLA_DV)
            o_h = inter[hh * CHUNK:(hh + 1) * CHUNK, :] + intra[ci][0][hh]
            g_h = gate_ref[0, r, cols].astype(F32)
            o_ref[0, r, cols] = (_rms(o_h) * gout * g_h).astype(BF16)

    inter = {}
    for t in range(n_chunks + 1):
        if t < n_chunks:
            inter[t] = _nt(pre[t][0], st_in[t])
        if t >= 1:
            finish(t - 1, inter[t - 1])


def _attn_kernel(qt_ref, k_ref, vt_ref, o_ref, s_ref, acc_ref, m_ref, bm_ref, *, tq, tk):
    qi = pl.program_id(2)
    ndiag = tq // tk

    m_ref[...] = jnp.full(m_ref.shape, NEG, F32)
    acc_ref[...] = jnp.zeros(acc_ref.shape, F32)

    kr = lax.broadcasted_iota(jnp.int32, (tk, tk), 0)
    qc = lax.broadcasted_iota(jnp.int32, (tk, tk), 1)
    square_mask = (kr // CHUNK) <= (qc // CHUNK)

    def qk(j, slot, c0=0, c1=tq):
        start = pl.multiple_of(j * tk, tk)
        s = jnp.dot(k_ref[0, pl.ds(start, tk), :], qt_ref[0, :, c0:c1],
                    preferred_element_type=F32)
        s_ref[slot, :, c0:c1] = s
        bm_ref[slot, :, c0:c1] = jnp.max(s, axis=0, keepdims=True)

    def mask_square(slot, c0):
        cols = slice(c0, c0 + tk)
        s = jnp.where(square_mask, s_ref[slot, :, cols], NEG)
        s_ref[slot, :, cols] = s
        bm_ref[slot, :, cols] = jnp.max(s, axis=0, keepdims=True)

    def softmax_pv(j, slots, c0=0, c1=tq):
        m_old = m_ref[:, c0:c1]
        m_new = m_old
        for slot in slots:
            m_new = jnp.maximum(m_new, bm_ref[slot, :, c0:c1])
        m_ref[:, c0:c1] = m_new
        p = jnp.concatenate([jnp.exp2(s_ref[slot, :, c0:c1] - m_new).astype(BF16)
                             for slot in slots], axis=0)
        start = pl.multiple_of(j * tk, tk)
        pv = jnp.dot(vt_ref[0, :, pl.ds(start, len(slots) * tk)], p,
                     preferred_element_type=F32)
        acc_ref[:, c0:c1] = jnp.exp2(m_old - m_new) * acc_ref[:, c0:c1] + pv

    ngroups = tq // QUERY_GROUP
    per_iter = 2 * ngroups

    def cols(g):
        return g * QUERY_GROUP, (g + 1) * QUERY_GROUP

    def qk_item(j, ps, g, both=True):
        qk(j, 2 * ps, *cols(g))
        if both:
            qk(j + 1, 2 * ps + 1, *cols(g))

    for u in range(QK_LOOKAHEAD):
        qk_item(0, 0, u)

    def body(i, carry):
        for u in range(per_iter):
            v = u + QK_LOOKAHEAD
            if v < per_iter:
                qk_item(4 * i + 2 * (v // ngroups), v // ngroups, v % ngroups)
            else:
                qk_item(4 * i + 4, 0, v - per_iter)
            ps, g = divmod(u, ngroups)
            softmax_pv(4 * i + 2 * ps, (2 * ps, 2 * ps + 1), *cols(g))
        return carry

    lax.fori_loop(0, qi * (ndiag // 4), body, 0)

    nfull = qi * ndiag
    items = [(p, g) for p in range(ndiag // 2) for g in range(2 * p, ngroups)]
    for t, (p, g) in enumerate(items):
        if t + QK_LOOKAHEAD < len(items):
            p2, g2 = items[t + QK_LOOKAHEAD]
            qk_item(nfull + 2 * p2, p2 % 2, g2, both=(g2 != 2 * p2))
        d = 2 * p
        slots = (2 * (p % 2), 2 * (p % 2) + 1)
        if g == d:
            mask_square(slots[0], d * tk)
            softmax_pv(nfull + d, slots[:1], *cols(g))
        else:
            if g == d + 1:
                mask_square(slots[1], (d + 1) * tk)
            softmax_pv(nfull + d, slots, *cols(g))

    acc = acc_ref[...]
    o_ref[0] = (acc[:MLA_V, :] / acc[MLA_V:MLA_V + 1, :]).astype(BF16)


def _mlp_kernel(x_ref, og_ref, ot_ref, w_out_ref, g_mlp_ref, w_up_ref, w_dn_ref,
                y_ref, *, ff_block):
    mix = jnp.dot(og_ref[0], w_out_ref[:GLA_V_W, :], preferred_element_type=F32)
    mix = mix + _tn(ot_ref[0], w_out_ref[GLA_V_W:, :])
    x1 = x_ref[0] + mix
    h = (_rms(x1) * g_mlp_ref[...]).astype(BF16)
    acc = x1
    d_ff = w_up_ref.shape[1]
    for f in range(d_ff // ff_block):
        cols = slice(f * ff_block, (f + 1) * ff_block)
        u = jnp.dot(h, w_up_ref[:, cols], preferred_element_type=F32)
        a = jnp.square(jnp.maximum(u, 0.0)).astype(BF16)
        acc = acc + jnp.dot(a, w_dn_ref[cols, :], preferred_element_type=F32)
    y_ref[0] = acc


def _const_spec(shape):
    nd = len(shape)
    return pl.BlockSpec(shape, lambda *_: (0,) * nd)


def _pack_weights(w_in, w_gate_up, b_gate, w_uq, w_ukv, q_head_norm, k_head_norm):
    o_gate = 2 * GLA_QK_W + GLA_V_W
    o_rest = o_gate + GLA_GATE_RANK
    rest = w_in[:, o_rest:]
    assert A_W == o_gate and rest.shape[1] == B_MISC + MLA_ROPE and MISC_PE == 0
    gate_at = B_MISC + MISC_GATE
    w_b = (jnp.pad(rest, ((0, 0), (0, B_W - rest.shape[1])))
           + jnp.pad(w_in[:, o_gate:o_rest], ((0, 0), (gate_at, B_W - gate_at - GLA_GATE_RANK)))
           ).astype(BF16)

    w_gate_p = jnp.pad(w_gate_up.astype(F32),
                       ((MISC_GATE, LANES - MISC_GATE - GLA_GATE_RANK), (0, 0)))
    b_gate_p = b_gate.reshape(1, GLA_QK_W).astype(F32)

    w_uq_h = w_uq.reshape(MLA_Q_RANK, MLA_HEADS, MLA_QK)
    w_uq_h = jnp.concatenate([w_uq_h[:, :, MLA_NOPE:], w_uq_h[:, :, :MLA_NOPE]], axis=2)
    w_uq_h = jnp.pad(w_uq_h, ((0, 0), (0, 0), (0, HEAD_PAD - MLA_QK)))
    w_uqt = w_uq_h.reshape(MLA_Q_RANK, MLA_HEADS * HEAD_PAD).T.astype(BF16)

    nope_pad = (H_NOPE, HEAD_PAD - H_NOPE - MLA_NOPE)
    w_ukv_h = w_ukv.reshape(MLA_KV_RANK, MLA_HEADS, MLA_NOPE + MLA_V)
    w_k = jnp.pad(w_ukv_h[:, :, :MLA_NOPE], ((0, 0), (0, 0), nope_pad))
    w_k = w_k.reshape(MLA_KV_RANK, MLA_HEADS * HEAD_PAD).astype(BF16)
    w_vt = w_ukv_h[:, :, MLA_NOPE:].reshape(MLA_KV_RANK, MLA_HEADS * MLA_V).T.astype(BF16)

    qg = jnp.concatenate([q_head_norm[MLA_NOPE:], q_head_norm[:MLA_NOPE]])
    qg_col = jnp.pad(qg, (0, HEAD_PAD - MLA_QK)).reshape(HEAD_PAD, 1).astype(F32)
    kg_nope = jnp.pad(k_head_norm[:MLA_NOPE], nope_pad).reshape(1, LANES).astype(F32)
    kg_rope = jnp.pad(k_head_norm[MLA_NOPE:], (MISC_PE, LANES - MISC_PE - MLA_ROPE))
    kg_rope = kg_rope.reshape(1, LANES).astype(F32)
    return w_b, w_gate_p, b_gate_p, w_uqt, w_k, w_vt, qg_col, kg_nope, kg_rope


def _layer(x, posr, invf_col, rope_sel, attn_norm, w_in, w_gate_up, b_gate, gla_out_norm,
           q_a_norm, w_uq, kv_a_norm, w_ukv, q_head_norm, k_head_norm, w_out, mlp_norm, w_up,
           w_down):
    b_, s_, d_model = x.shape
    tm = min(TOKEN_BLOCK, s_)
    tg = min(GLA_BLOCK, s_)
    tq = min(ATTN_TQ, s_)
    tk = min(ATTN_TK, tq)
    assert s_ % tm == 0 and s_ % tg == 0 and s_ % tq == 0 and tk % CHUNK == 0
    assert tq % (4 * tk) == 0
    assert tk == QUERY_GROUP and QK_LOOKAHEAD <= tq // QUERY_GROUP
    d_ff = w_up.shape[1]
    ff_block = min(FF_BLOCK, d_ff)

    (w_b, w_gate_p, b_gate_p, w_uqt, w_k, w_vt, qg_col, kg_nope,
     kg_rope) = _pack_weights(w_in, w_gate_up, b_gate, w_uq, w_ukv, q_head_norm, k_head_norm)
    row = lambda v: v.reshape(1, -1).astype(F32)

    tok = lambda w: pl.BlockSpec((1, tm, w), lambda b, i: (b, i, 0))
    tok_t = lambda hgt: pl.BlockSpec((1, hgt, tm), lambda b, i: (b, 0, i))
    consts = (w_b, w_gate_p, b_gate_p, row(q_a_norm), w_uqt, row(kv_a_norm),
              w_k, w_vt, qg_col, kg_nope, kg_rope, invf_col, rope_sel)
    w_a_spec = pl.BlockSpec((d_model, A_W), lambda b, i: (0, 0))
    zqk, loga, zv, gate, qt, kk, vt = pl.pallas_call(
        _proj_kernel,
        grid=(b_, s_ // tm),
        in_specs=([tok(d_model), tok_t(1), _const_spec((1, d_model)), w_a_spec]
                  + [_const_spec(c.shape) for c in consts]),
        out_specs=[tok(2 * GLA_QK_W), tok(GLA_QK_W), tok(GLA_V_W), tok(GLA_V_W),
                   tok_t(MLA_HEADS * HEAD_PAD), tok(MLA_HEADS * HEAD_PAD), tok_t(MLA_HEADS * V_AUG)],
        out_shape=[jax.ShapeDtypeStruct((b_, s_, 2 * GLA_QK_W), F32),
                   jax.ShapeDtypeStruct((b_, s_, GLA_QK_W), F32),
                   jax.ShapeDtypeStruct((b_, s_, GLA_V_W), BF16),
                   jax.ShapeDtypeStruct((b_, s_, GLA_V_W), BF16),
                   jax.ShapeDtypeStruct((b_, MLA_HEADS * HEAD_PAD, s_), BF16),
                   jax.ShapeDtypeStruct((b_, s_, MLA_HEADS * HEAD_PAD), BF16),
                   jax.ShapeDtypeStruct((b_, MLA_HEADS * V_AUG, s_), BF16)],
        compiler_params=pltpu.CompilerParams(dimension_semantics=("parallel", "parallel"),
                                             vmem_limit_bytes=VMEM_LIMIT),
        name="proj",
    )(x, posr, row(attn_norm), w_in, *consts)

    gtok = lambda w: pl.BlockSpec((1, tg, w), lambda b, i: (b, i, 0))
    tri = jnp.asarray(np.tril(np.ones((CHUNK, CHUNK), np.float32)), BF16)
    n_gla = s_ // tg
    n_steps = b_ * n_gla
    assert d_model % (n_steps * BF16_SUBLANES) == 0 and d_ff % (n_steps * BF16_SUBLANES) == 0
    slab = lambda rows, w: pl.BlockSpec((rows // n_steps, w), lambda b, i: (b * n_gla + i, 0))
    o_gla, w_up_bf, w_dn_bf = pl.pallas_call(
        functools.partial(_gla_kernel, n_chunks=tg // CHUNK),
        grid=(b_, n_gla),
        in_specs=[gtok(2 * GLA_QK_W), gtok(GLA_QK_W), gtok(GLA_V_W), gtok(GLA_V_W),
                  _const_spec((1, GLA_DV)), _const_spec((CHUNK, CHUNK)),
                  slab(d_model, d_ff), slab(d_ff, d_model)],
        out_specs=[gtok(GLA_V_W), slab(d_model, d_ff), slab(d_ff, d_model)],
        out_shape=[jax.ShapeDtypeStruct((b_, s_, GLA_V_W), BF16),
                   jax.ShapeDtypeStruct((d_model, d_ff), BF16),
                   jax.ShapeDtypeStruct((d_ff, d_model), BF16)],
        scratch_shapes=[pltpu.VMEM((GLA_DV, GLA_QK_W), F32)],
        compiler_params=pltpu.CompilerParams(dimension_semantics=("parallel", "arbitrary"),
                                             vmem_limit_bytes=VMEM_LIMIT),
        name="gla",
    )(zqk, loga, zv, gate, row(gla_out_norm), tri, w_up, w_down)

    o_t = pl.pallas_call(
        functools.partial(_attn_kernel, tq=tq, tk=tk),
        grid=(b_, MLA_HEADS, s_ // tq),
        in_specs=[pl.BlockSpec((1, HEAD_PAD, tq), lambda b, h, i: (b, h, i)),
                  pl.BlockSpec((1, s_, HEAD_PAD), lambda b, h, i: (b, 0, h)),
                  pl.BlockSpec((1, V_AUG, s_), lambda b, h, i: (b, h, 0))],
        out_specs=pl.BlockSpec((1, MLA_V, tq), lambda b, h, i: (b, h, i)),
        out_shape=jax.ShapeDtypeStruct((b_, MLA_HEADS * MLA_V, s_), BF16),
        scratch_shapes=[pltpu.VMEM((4, tk, tq), F32), pltpu.VMEM((V_AUG, tq), F32),
                        pltpu.VMEM((1, tq), F32), pltpu.VMEM((4, 1, tq), F32)],
        compiler_params=pltpu.CompilerParams(
            dimension_semantics=("parallel", "parallel", "arbitrary"),
            vmem_limit_bytes=VMEM_LIMIT),
        name="attn",
    )(qt, kk, vt)

    y = pl.pallas_call(
        functools.partial(_mlp_kernel, ff_block=ff_block),
        grid=(b_, s_ // tm),
        in_specs=[tok(d_model), tok(GLA_V_W), tok_t(MLA_HEADS * MLA_V),
                  _const_spec(w_out.shape), _const_spec((1, d_model)),
                  _const_spec(w_up.shape), _const_spec(w_down.shape)],
        out_specs=tok(d_model),
        out_shape=jax.ShapeDtypeStruct((b_, s_, d_model), x.dtype),
        compiler_params=pltpu.CompilerParams(dimension_semantics=("parallel", "parallel"),
                                             vmem_limit_bytes=VMEM_LIMIT),
        name="mlp",
    )(x, o_gla, o_t, w_out.astype(BF16), row(mlp_norm), w_up_bf, w_dn_bf)
    return y


def kernel(x, positions, attn_norm, w_in, w_gate_up, b_gate, gla_out_norm, q_a_norm, w_uq,
           kv_a_norm, w_ukv, q_head_norm, k_head_norm, w_out, mlp_norm, w_up, w_down):
    b_, s_, _ = x.shape
    posr = positions.reshape(b_, 1, s_)
    inv_freq = ROPE_BASE ** (-jnp.arange(0, MLA_ROPE, 2, dtype=F32) / MLA_ROPE)
    invf_col = inv_freq.reshape(MLA_HALF, 1)
    i = np.arange(MLA_HALF)
    sel = np.zeros((MLA_ROPE, 3 * LANES), np.float32)
    sel[i, MISC_PE + i] = 1.0
    sel[i, MISC_PE + MLA_HALF + i] = 1.0
    sel[MLA_HALF + i, LANES + MISC_PE + i] = -1.0
    sel[MLA_HALF + i, 2 * LANES + MISC_PE + MLA_HALF + i] = 1.0
    rope_sel = jnp.asarray(np.tile(sel, (3, 1)), BF16)
    for l in range(attn_norm.shape[0]):
        x = _layer(x, posr, invf_col, rope_sel, attn_norm[l], w_in[l], w_gate_up[l],
                   b_gate[l], gla_out_norm[l], q_a_norm[l], w_uq[l], kv_a_norm[l], w_ukv[l],
                   q_head_norm[l], k_head_norm[l], w_out[l], mlp_norm[l], w_up[l], w_down[l])
    return x
```

```python
import functools
import math

import jax
import jax.numpy as jnp
import numpy as np
from jax import lax
from jax.experimental import pallas as pl
from jax.experimental.pallas import tpu as pltpu

F32 = jnp.float32
BF16 = jnp.bfloat16
HIGHEST = lax.Precision.HIGHEST

CHUNK = 64
EPS = 1e-6

GLA_HEADS = 4
GLA_DK = 64
GLA_DV = 128
GLA_GATE_RANK = 16
GLA_GATE_NORMALIZER = 16.0
GLA_LOG_GATE_MIN = -1.0
GLA_QK_W = GLA_HEADS * GLA_DK
GLA_V_W = GLA_HEADS * GLA_DV

MLA_HEADS = 8
MLA_Q_RANK = 256
MLA_KV_RANK = 128
MLA_NOPE = 64
MLA_ROPE = 32
MLA_HALF = MLA_ROPE // 2
MLA_QK = MLA_NOPE + MLA_ROPE
MLA_V = 64
ROPE_BASE = 10000.0

LANES = 128
HEAD_PAD = LANES
BF16_SUBLANES = 16
V_AUG = MLA_V + BF16_SUBLANES
LOG2_E = math.log2(math.e)
NEG = -1e30

H_NOPE = MLA_ROPE
A_Q = 0
A_K = A_Q + GLA_QK_W
A_V = A_K + GLA_QK_W
A_W = A_V + GLA_V_W
B_G = 0
B_CQ = B_G + GLA_V_W
B_CKV = B_CQ + MLA_Q_RANK
B_MISC = B_CKV + MLA_KV_RANK
B_W = B_MISC + LANES
MISC_PE = 0
MISC_GATE = MLA_ROPE

PROJ_PIECE = 256
TOKEN_BLOCK = 1024
MLP_BLOCK = 1024
GLA_BLOCK = 1024
ATTN_TQ = 4096
ATTN_TK = 256
QUERY_GROUP = 256
QK_LOOKAHEAD = 3
FF_BLOCK = 1024
VMEM_LIMIT = 56 * 1024 * 1024


def _nt(a, b):
    return lax.dot_general(a, b, (((1,), (1,)), ((), ())), preferred_element_type=F32)


def _tn(a, b):
    return lax.dot_general(a, b, (((0,), (0,)), ((), ())), preferred_element_type=F32)


def _rms(v):
    return v * lax.rsqrt(jnp.mean(v * v, axis=-1, keepdims=True) + EPS)


def _proj_kernel(x_ref, posr_ref, g_attn_ref, w_a_ref, w_b_ref, w_gate_ref, b_gate_ref,
                 qa_g_ref, w_uqt_ref, kva_g_ref, w_k_ref, w_vt_ref,
                 qg_col_ref, kg_nope_ref, kg_rope_ref, invf_col_ref, rope_sel_ref,
                 zqk_ref, loga_ref, zv_ref, gate_ref, qt_ref, k_ref, vt_ref):
    hb = (_rms(x_ref[0]) * g_attn_ref[...]).astype(BF16)

    def in_proj(w_ref, c0, c1):
        return jnp.dot(hb, w_ref[:, c0:c1].astype(BF16), preferred_element_type=F32)

    zg0 = in_proj(w_b_ref, B_G, B_G + PROJ_PIECE)
    zg1 = in_proj(w_b_ref, B_G + PROJ_PIECE, B_CQ)
    z_mla = in_proj(w_b_ref, B_CQ, B_W)
    gate_ref[0, :, :PROJ_PIECE] = (zg0 * jax.nn.sigmoid(zg0)).astype(BF16)
    gate_ref[0, :, PROJ_PIECE:] = (zg1 * jax.nn.sigmoid(zg1)).astype(BF16)
    zcq = z_mla[:, :B_CKV - B_CQ]
    zckv = z_mla[:, B_CKV - B_CQ:B_MISC - B_CQ]
    misc = z_mla[:, B_MISC - B_CQ:]

    cq = (_rms(zcq) * qa_g_ref[...]).astype(BF16)
    ckv = (_rms(zckv) * kva_g_ref[...]).astype(BF16)
    vt = _nt(w_vt_ref[...], ckv).astype(BF16)
    ang_t = invf_col_ref[...] * posr_ref[0].astype(F32)
    cos_t = jnp.cos(ang_t)
    sin_t = jnp.sin(ang_t)
    cs3 = jnp.concatenate(_split3(jnp.concatenate([cos_t, sin_t], axis=0)), axis=0)
    tabs = _tn(cs3, rope_sel_ref[...])

    logit = jnp.dot(misc, w_gate_ref[...], preferred_element_type=F32, precision=HIGHEST)
    logit = logit + b_gate_ref[...]
    log_sig = jnp.minimum(logit, 0.0) - jnp.log1p(jnp.exp(-jnp.abs(logit)))
    loga_ref[0] = jnp.maximum(log_sig / GLA_GATE_NORMALIZER, GLA_LOG_GATE_MIN)

    ones_rows = jnp.ones((V_AUG - MLA_V, vt.shape[1]), BF16)
    for hh in range(MLA_HEADS):
        vt_ref[0, hh * V_AUG:hh * V_AUG + MLA_V, :] = vt[hh * MLA_V:(hh + 1) * MLA_V, :]
        vt_ref[0, hh * V_AUG + MLA_V:(hh + 1) * V_AUG, :] = ones_rows

    qscale = MLA_QK ** -0.5 * LOG2_E
    r1 = MLA_HALF
    r2 = H_NOPE
    r3 = H_NOPE + MLA_NOPE

    def q_head(hh, blk):
        base = hh * HEAD_PAD
        ssq = jnp.sum(blk * blk, axis=0, keepdims=True)
        rs = lax.rsqrt(ssq * (1.0 / MLA_QK) + EPS) * qscale
        qn = blk * rs * qg_col_ref[...]
        x1 = qn[:r1, :]
        x2 = qn[r1:r2, :]
        qt_ref[0, base:base + r1, :] = (x1 * cos_t - x2 * sin_t).astype(BF16)
        qt_ref[0, base + r1:base + r2, :] = (x1 * sin_t + x2 * cos_t).astype(BF16)
        qt_ref[0, base + r2:base + r3, :] = qn[r2:r3, :].astype(BF16)
        qt_ref[0, base + r3:base + HEAD_PAD, :] = jnp.zeros((HEAD_PAD - r3, blk.shape[1]), BF16)

    lane = lax.broadcasted_iota(jnp.int32, (1, LANES), 1)
    is_pe = (lane >= MISC_PE) & (lane < MISC_PE + MLA_ROPE)
    kpe = jnp.where(is_pe, misc, 0.0)
    ssq_pe = jnp.sum(kpe * kpe, axis=-1, keepdims=True)
    xg = kpe * kg_rope_ref[...]
    rot = (xg * tabs[:, :LANES]
           + pltpu.roll(xg, LANES - MLA_HALF, 1) * tabs[:, LANES:2 * LANES]
           + pltpu.roll(xg, MLA_HALF, 1) * tabs[:, 2 * LANES:])

    def k_head(hh, kn_h):
        base = hh * HEAD_PAD
        ssq = jnp.sum(kn_h * kn_h, axis=-1, keepdims=True) + ssq_pe
        rs = lax.rsqrt(ssq * (1.0 / MLA_QK) + EPS)
        k_ref[0, :, base:base + HEAD_PAD] = ((kn_h * kg_nope_ref[...] + rot) * rs).astype(BF16)

    pieces = [(zqk_ref, 0, A_Q), (zqk_ref, PROJ_PIECE, A_K),
              (zv_ref, 0, A_V), (zv_ref, PROJ_PIECE, A_V + PROJ_PIECE)]
    group = MLA_HEADS // len(pieces)
    for i, (ref, lane0, c0) in enumerate(pieces):
        rows = slice(i * group * HEAD_PAD, (i + 1) * group * HEAD_PAD)
        ref[0, :, lane0:lane0 + PROJ_PIECE] = in_proj(w_a_ref, c0, c0 + PROJ_PIECE).astype(ref.dtype)
        qt = _nt(w_uqt_ref[rows, :], cq)
        kn = jnp.dot(ckv, w_k_ref[:, rows], preferred_element_type=F32)
        for n in range(group):
            q_head(i * group + n, qt[n * HEAD_PAD:(n + 1) * HEAD_PAD, :])
            k_head(i * group + n, kn[:, n * HEAD_PAD:(n + 1) * HEAD_PAD])


def _split3(v):
    hi = v.astype(BF16)
    r1 = v - hi.astype(F32)
    mid = r1.astype(BF16)
    lo = (r1 - mid.astype(F32)).astype(BF16)
    return hi, mid, lo


def _gla_kernel(zqk_ref, loga_ref, zv_ref, gate_ref, gout_ref, tri_ref, w_up_ref, w_dn_ref,
                o_ref, w_up_bf_ref, w_dn_bf_ref, st_ref, *, n_chunks):
    w_up_bf_ref[...] = w_up_ref[...].astype(BF16)
    w_dn_bf_ref[...] = w_dn_ref[...].astype(BF16)

    @pl.when(pl.program_id(1) == 0)
    def _():
        st_ref[...] = jnp.zeros(st_ref.shape, F32)

    sr = lax.broadcasted_iota(jnp.int32, (GLA_HEADS * CHUNK, CHUNK), 0)
    sc_ = lax.broadcasted_iota(jnp.int32, (GLA_HEADS * CHUNK, CHUNK), 1)
    causal4 = (sr % CHUNK) >= sc_
    qr = lax.broadcasted_iota(jnp.int32, (GLA_HEADS * CHUNK, GLA_QK_W), 0)
    qc = lax.broadcasted_iota(jnp.int32, (GLA_HEADS * CHUNK, GLA_QK_W), 1)
    head_sel = (qr // CHUNK) == (qc // GLA_DK)
    lane_head = lax.broadcasted_iota(jnp.int32, (GLA_DV, GLA_QK_W), 1) // GLA_DK
    gout = gout_ref[...]
    chunks = [slice(ci * CHUNK, (ci + 1) * CHUNK) for ci in range(n_chunks)]

    la = loga_ref[0]
    parts = [p for r in chunks for p in _split3(la[r])]
    tri_out = jnp.dot(tri_ref[...], jnp.concatenate(parts, axis=1), preferred_element_type=F32)
    w = GLA_QK_W
    cum = [tri_out[:, (3 * ci) * w:(3 * ci + 1) * w] + tri_out[:, (3 * ci + 1) * w:(3 * ci + 2) * w]
           + tri_out[:, (3 * ci + 2) * w:(3 * ci + 3) * w] for ci in range(n_chunks)]
    cl = [c[CHUNK - 1:CHUNK, :] for c in cum]

    def prep(ci):
        r = chunks[ci]
        q = zqk_ref[0, r, 0:GLA_QK_W]
        k = zqk_ref[0, r, GLA_QK_W:2 * GLA_QK_W]
        qd = (q * (GLA_DK ** -0.5) * jnp.exp(cum[ci])).astype(BF16)
        k_inv = (k * jnp.exp(-cum[ci])).astype(BF16)
        k_end = (k * jnp.exp(cl[ci] - cum[ci])).astype(BF16)
        q_stack = jnp.where(head_sel, jnp.concatenate([qd] * GLA_HEADS, axis=0), 0.0)
        return q_stack, k_inv, k_end

    def intra_scores(ci):
        return jnp.where(causal4, _nt(pre[ci][0], pre[ci][1]), 0.0).astype(BF16)

    def intra_out_and_kv(ci):
        v = zv_ref[0, chunks[ci], :]
        full = [jnp.dot(scores[ci][hh * CHUNK:(hh + 1) * CHUNK, :],
                        v[:, hh * GLA_DV:(hh + 1) * GLA_DV], preferred_element_type=F32)
                for hh in range(GLA_HEADS)]
        kv_t = _tn(v, pre[ci][2])
        kv = kv_t[(GLA_HEADS - 1) * GLA_DV:, :]
        for hh in range(GLA_HEADS - 2, -1, -1):
            kv = jnp.where(lane_head == hh, kv_t[hh * GLA_DV:(hh + 1) * GLA_DV, :], kv)
        return full, kv

    pre, scores, intra = {}, {}, {}
    for t in range(n_chunks + 2):
        if t < n_chunks:
            pre[t] = prep(t)
        if 0 <= t - 1 < n_chunks:
            scores[t - 1] = intra_scores(t - 1)
        if 0 <= t - 2 < n_chunks:
            intra[t - 2] = intra_out_and_kv(t - 2)

    st = st_ref[...]
    st_in = []
    for ci in range(n_chunks):
        st_in.append(st.astype(BF16))
        st = st * jnp.exp(cl[ci]) + intra[ci][1]
    st_ref[...] = st

    def finish(ci, inter):
        r = chunks[ci]
        for hh in range(GLA_HEADS):
            cols = slice(hh * GLA_DV, (hh + 1) * GLA_DV)
            o_h = inter[hh * CHUNK:(hh + 1) * CHUNK, :] + intra[ci][0][hh]
            g_h = gate_ref[0, r, cols].astype(F32)
            o_ref[0, r, cols] = (_rms(o_h) * gout * g_h).astype(BF16)

    inter = {}
    for t in range(n_chunks + 1):
        if t < n_chunks:
            inter[t] = _nt(pre[t][0], st_in[t])
        if t >= 1:
            finish(t - 1, inter[t - 1])


def _attn_kernel(qt_ref, k_ref, vt_ref, o_ref, s_ref, acc_ref, m_ref, bm_ref, *, tq, tk):
    qi = pl.program_id(2)
    ndiag = tq // tk

    m_ref[...] = jnp.full(m_ref.shape, NEG, F32)
    acc_ref[...] = jnp.zeros(acc_ref.shape, F32)

    kr = lax.broadcasted_iota(jnp.int32, (tk, tk), 0)
    qc = lax.broadcasted_iota(jnp.int32, (tk, tk), 1)
    square_mask = (kr // CHUNK) <= (qc // CHUNK)

    def qk(j, slot, c0=0, c1=tq):
        start = pl.multiple_of(j * tk, tk)
        s = jnp.dot(k_ref[0, pl.ds(start, tk), :], qt_ref[0, :, c0:c1],
                    preferred_element_type=F32)
        s_ref[slot, :, c0:c1] = s
        bm_ref[slot, :, c0:c1] = jnp.max(s, axis=0, keepdims=True)

    def mask_square(slot, c0):
        cols = slice(c0, c0 + tk)
        s = jnp.where(square_mask, s_ref[slot, :, cols], NEG)
        s_ref[slot, :, cols] = s
        bm_ref[slot, :, cols] = jnp.max(s, axis=0, keepdims=True)

    def softmax_pv(j, slots, c0=0, c1=tq):
        m_old = m_ref[:, c0:c1]
        m_new = m_old
        for slot in slots:
            m_new = jnp.maximum(m_new, bm_ref[slot, :, c0:c1])
        m_ref[:, c0:c1] = m_new
        p = jnp.concatenate([jnp.exp2(s_ref[slot, :, c0:c1] - m_new).astype(BF16)
                             for slot in slots], axis=0)
        start = pl.multiple_of(j * tk, tk)
        pv = jnp.dot(vt_ref[0, :, pl.ds(start, len(slots) * tk)], p,
                     preferred_element_type=F32)
        acc_ref[:, c0:c1] = jnp.exp2(m_old - m_new) * acc_ref[:, c0:c1] + pv

    ngroups = tq // QUERY_GROUP
    per_iter = 2 * ngroups

    def cols(g):
        return g * QUERY_GROUP, (g + 1) * QUERY_GROUP

    def qk_item(j, ps, g, both=True):
        qk(j, 2 * ps, *cols(g))
        if both:
            qk(j + 1, 2 * ps + 1, *cols(g))

    for u in range(QK_LOOKAHEAD):
        qk_item(0, 0, u)

    def body(i, carry):
        for u in range(per_iter):
            v = u + QK_LOOKAHEAD
            if v < per_iter:
                qk_item(4 * i + 2 * (v // ngroups), v // ngroups, v % ngroups)
            else:
                qk_item(4 * i + 4, 0, v - per_iter)
            ps, g = divmod(u, ngroups)
            softmax_pv(4 * i + 2 * ps, (2 * ps, 2 * ps + 1), *cols(g))
        return carry

    lax.fori_loop(0, qi * (ndiag // 4), body, 0)

    nfull = qi * ndiag
    items = [(p, g) for p in range(ndiag // 2) for g in range(2 * p, ngroups)]
    for t, (p, g) in enumerate(items):
        if t + QK_LOOKAHEAD < len(items):
            p2, g2 = items[t + QK_LOOKAHEAD]
            qk_item(nfull + 2 * p2, p2 % 2, g2, both=(g2 != 2 * p2))
        d = 2 * p
        slots = (2 * (p % 2), 2 * (p % 2) + 1)
        if g == d:
            mask_square(slots[0], d * tk)
            softmax_pv(nfull + d, slots[:1], *cols(g))
        else:
            if g == d + 1:
                mask_square(slots[1], (d + 1) * tk)
            softmax_pv(nfull + d, slots, *cols(g))

    acc = acc_ref[...]
    o_ref[0] = (acc[:MLA_V, :] / acc[MLA_V:MLA_V + 1, :]).astype(BF16)


def _mlp_kernel(x_ref, og_ref, ot_ref, w_out_ref, g_mlp_ref, w_up_ref, w_dn_ref,
                y_ref, *, ff_block):
    mix = jnp.dot(og_ref[0], w_out_ref[:GLA_V_W, :], preferred_element_type=F32)
    mix = mix + _tn(ot_ref[0], w_out_ref[GLA_V_W:, :])
    x1 = x_ref[0] + mix
    h = (_rms(x1) * g_mlp_ref[...]).astype(BF16)
    acc = x1
    d_ff = w_up_ref.shape[1]
    for f in range(d_ff // ff_block):
        cols = slice(f * ff_block, (f + 1) * ff_block)
        u = jnp.dot(h, w_up_ref[:, cols], preferred_element_type=F32)
        a = jnp.square(jnp.maximum(u, 0.0)).astype(BF16)
        acc = acc + jnp.dot(a, w_dn_ref[cols, :], preferred_element_type=F32)
    y_ref[0] = acc


def _const_spec(shape, single_buffer=False):
    nd = len(shape)
    mode = pl.Buffered(1) if single_buffer else None
    return pl.BlockSpec(shape, lambda *_: (0,) * nd, pipeline_mode=mode)


def _pack_weights(w_in, w_gate_up, b_gate, w_uq, w_ukv, q_head_norm, k_head_norm):
    o_gate = 2 * GLA_QK_W + GLA_V_W
    o_rest = o_gate + GLA_GATE_RANK
    rest = w_in[:, o_rest:]
    assert A_W == o_gate and rest.shape[1] == B_MISC + MLA_ROPE and MISC_PE == 0
    gate_at = B_MISC + MISC_GATE
    w_b = (jnp.pad(rest, ((0, 0), (0, B_W - rest.shape[1])))
           + jnp.pad(w_in[:, o_gate:o_rest], ((0, 0), (gate_at, B_W - gate_at - GLA_GATE_RANK)))
           ).astype(BF16)

    w_gate_p = jnp.pad(w_gate_up.astype(F32),
                       ((MISC_GATE, LANES - MISC_GATE - GLA_GATE_RANK), (0, 0)))
    b_gate_p = b_gate.reshape(1, GLA_QK_W).astype(F32)

    w_uq_h = w_uq.reshape(MLA_Q_RANK, MLA_HEADS, MLA_QK)
    w_uq_h = jnp.concatenate([w_uq_h[:, :, MLA_NOPE:], w_uq_h[:, :, :MLA_NOPE]], axis=2)
    w_uq_h = jnp.pad(w_uq_h, ((0, 0), (0, 0), (0, HEAD_PAD - MLA_QK)))
    w_uqt = w_uq_h.reshape(MLA_Q_RANK, MLA_HEADS * HEAD_PAD).T.astype(BF16)

    nope_pad = (H_NOPE, HEAD_PAD - H_NOPE - MLA_NOPE)
    w_ukv_h = w_ukv.reshape(MLA_KV_RANK, MLA_HEADS, MLA_NOPE + MLA_V)
    w_k = jnp.pad(w_ukv_h[:, :, :MLA_NOPE], ((0, 0), (0, 0), nope_pad))
    w_k = w_k.reshape(MLA_KV_RANK, MLA_HEADS * HEAD_PAD).astype(BF16)
    w_vt = w_ukv_h[:, :, MLA_NOPE:].reshape(MLA_KV_RANK, MLA_HEADS * MLA_V).T.astype(BF16)

    qg = jnp.concatenate([q_head_norm[MLA_NOPE:], q_head_norm[:MLA_NOPE]])
    qg_col = jnp.pad(qg, (0, HEAD_PAD - MLA_QK)).reshape(HEAD_PAD, 1).astype(F32)
    kg_nope = jnp.pad(k_head_norm[:MLA_NOPE], nope_pad).reshape(1, LANES).astype(F32)
    kg_rope = jnp.pad(k_head_norm[MLA_NOPE:], (MISC_PE, LANES - MISC_PE - MLA_ROPE))
    kg_rope = kg_rope.reshape(1, LANES).astype(F32)
    return w_b, w_gate_p, b_gate_p, w_uqt, w_k, w_vt, qg_col, kg_nope, kg_rope


def _layer(x, posr, invf_col, rope_sel, attn_norm, w_in, w_gate_up, b_gate, gla_out_norm,
           q_a_norm, w_uq, kv_a_norm, w_ukv, q_head_norm, k_head_norm, w_out, mlp_norm, w_up,
           w_down):
    b_, s_, d_model = x.shape
    tm = min(TOKEN_BLOCK, s_)
    tg = min(GLA_BLOCK, s_)
    tq = min(ATTN_TQ, s_)
    tk = min(ATTN_TK, tq)
    assert s_ % tm == 0 and s_ % tg == 0 and s_ % tq == 0 and tk % CHUNK == 0
    assert tq % (4 * tk) == 0
    assert tk == QUERY_GROUP and QK_LOOKAHEAD <= tq // QUERY_GROUP
    d_ff = w_up.shape[1]
    ff_block = min(FF_BLOCK, d_ff)

    (w_b, w_gate_p, b_gate_p, w_uqt, w_k, w_vt, qg_col, kg_nope,
     kg_rope) = _pack_weights(w_in, w_gate_up, b_gate, w_uq, w_ukv, q_head_norm, k_head_norm)
    row = lambda v: v.reshape(1, -1).astype(F32)

    tok = lambda w: pl.BlockSpec((1, tm, w), lambda b, i: (b, i, 0))
    tok_t = lambda hgt: pl.BlockSpec((1, hgt, tm), lambda b, i: (b, 0, i))
    consts = (w_b, w_gate_p, b_gate_p, row(q_a_norm), w_uqt, row(kv_a_norm),
              w_k, w_vt, qg_col, kg_nope, kg_rope, invf_col, rope_sel)
    w_a_spec = pl.BlockSpec((d_model, A_W), lambda b, i: (0, 0))
    zqk, loga, zv, gate, qt, kk, vt = pl.pallas_call(
        _proj_kernel,
        grid=(b_, s_ // tm),
        in_specs=([tok(d_model), tok_t(1), _const_spec((1, d_model)), w_a_spec]
                  + [_const_spec(c.shape) for c in consts]),
        out_specs=[tok(2 * GLA_QK_W), tok(GLA_QK_W), tok(GLA_V_W), tok(GLA_V_W),
                   tok_t(MLA_HEADS * HEAD_PAD), tok(MLA_HEADS * HEAD_PAD), tok_t(MLA_HEADS * V_AUG)],
        out_shape=[jax.ShapeDtypeStruct((b_, s_, 2 * GLA_QK_W), F32),
                   jax.ShapeDtypeStruct((b_, s_, GLA_QK_W), F32),
                   jax.ShapeDtypeStruct((b_, s_, GLA_V_W), BF16),
                   jax.ShapeDtypeStruct((b_, s_, GLA_V_W), BF16),
                   jax.ShapeDtypeStruct((b_, MLA_HEADS * HEAD_PAD, s_), BF16),
                   jax.ShapeDtypeStruct((b_, s_, MLA_HEADS * HEAD_PAD), BF16),
                   jax.ShapeDtypeStruct((b_, MLA_HEADS * V_AUG, s_), BF16)],
        compiler_params=pltpu.CompilerParams(dimension_semantics=("parallel", "parallel"),
                                             vmem_limit_bytes=VMEM_LIMIT),
        name="proj",
    )(x, posr, row(attn_norm), w_in, *consts)

    gtok = lambda w: pl.BlockSpec((1, tg, w), lambda b, i: (b, i, 0))
    tri = jnp.asarray(np.tril(np.ones((CHUNK, CHUNK), np.float32)), BF16)
    n_gla = s_ // tg
    n_steps = b_ * n_gla
    assert d_model % (n_steps * BF16_SUBLANES) == 0 and d_ff % (n_steps * BF16_SUBLANES) == 0
    slab = lambda rows, w: pl.BlockSpec((rows // n_steps, w), lambda b, i: (b * n_gla + i, 0))
    o_gla, w_up_bf, w_dn_bf = pl.pallas_call(
        functools.partial(_gla_kernel, n_chunks=tg // CHUNK),
        grid=(b_, n_gla),
        in_specs=[gtok(2 * GLA_QK_W), gtok(GLA_QK_W), gtok(GLA_V_W), gtok(GLA_V_W),
                  _const_spec((1, GLA_DV)), _const_spec((CHUNK, CHUNK)),
                  slab(d_model, d_ff), slab(d_ff, d_model)],
        out_specs=[gtok(GLA_V_W), slab(d_model, d_ff), slab(d_ff, d_model)],
        out_shape=[jax.ShapeDtypeStruct((b_, s_, GLA_V_W), BF16),
                   jax.ShapeDtypeStruct((d_model, d_ff), BF16),
                   jax.ShapeDtypeStruct((d_ff, d_model), BF16)],
        scratch_shapes=[pltpu.VMEM((GLA_DV, GLA_QK_W), F32)],
        compiler_params=pltpu.CompilerParams(dimension_semantics=("parallel", "arbitrary"),
                                             vmem_limit_bytes=VMEM_LIMIT),
        name="gla",
    )(zqk, loga, zv, gate, row(gla_out_norm), tri, w_up, w_down)

    o_t = pl.pallas_call(
        functools.partial(_attn_kernel, tq=tq, tk=tk),
        grid=(b_, MLA_HEADS, s_ // tq),
        in_specs=[pl.BlockSpec((1, HEAD_PAD, tq), lambda b, h, i: (b, h, i)),
                  pl.BlockSpec((1, s_, HEAD_PAD), lambda b, h, i: (b, 0, h)),
                  pl.BlockSpec((1, V_AUG, s_), lambda b, h, i: (b, h, 0))],
        out_specs=pl.BlockSpec((1, MLA_V, tq), lambda b, h, i: (b, h, i)),
        out_shape=jax.ShapeDtypeStruct((b_, MLA_HEADS * MLA_V, s_), BF16),
        scratch_shapes=[pltpu.VMEM((4, tk, tq), F32), pltpu.VMEM((V_AUG, tq), F32),
                        pltpu.VMEM((1, tq), F32), pltpu.VMEM((4, 1, tq), F32)],
        compiler_params=pltpu.CompilerParams(
            dimension_semantics=("parallel", "parallel", "arbitrary"),
            vmem_limit_bytes=VMEM_LIMIT),
        name="attn",
    )(qt, kk, vt)

    tmm = min(MLP_BLOCK, s_)
    assert s_ % tmm == 0
    mtok = lambda w: pl.BlockSpec((1, tmm, w), lambda b, i: (b, i, 0))
    y = pl.pallas_call(
        functools.partial(_mlp_kernel, ff_block=ff_block),
        grid=(b_, s_ // tmm),
        in_specs=[mtok(d_model), mtok(GLA_V_W),
                  pl.BlockSpec((1, MLA_HEADS * MLA_V, tmm), lambda b, i: (b, 0, i)),
                  _const_spec(w_out.shape, True), _const_spec((1, d_model)),
                  _const_spec(w_up.shape, True), _const_spec(w_down.shape, True)],
        out_specs=mtok(d_model),
        out_shape=jax.ShapeDtypeStruct((b_, s_, d_model), x.dtype),
        compiler_params=pltpu.CompilerParams(dimension_semantics=("parallel", "parallel"),
                                             vmem_limit_bytes=VMEM_LIMIT),
        name="mlp",
    )(x, o_gla, o_t, w_out.astype(BF16), row(mlp_norm), w_up_bf, w_dn_bf)
    return y


def kernel(x, positions, attn_norm, w_in, w_gate_up, b_gate, gla_out_norm, q_a_norm, w_uq,
           kv_a_norm, w_ukv, q_head_norm, k_head_norm, w_out, mlp_norm, w_up, w_down):
    b_, s_, _ = x.shape
    posr = positions.reshape(b_, 1, s_)
    inv_freq = ROPE_BASE ** (-jnp.arange(0, MLA_ROPE, 2, dtype=F32) / MLA_ROPE)
    invf_col = inv_freq.reshape(MLA_HALF, 1)
    i = np.arange(MLA_HALF)
    sel = np.zeros((MLA_ROPE, 3 * LANES), np.float32)
    sel[i, MISC_PE + i] = 1.0
    sel[i, MISC_PE + MLA_HALF + i] = 1.0
    sel[MLA_HALF + i, LANES + MISC_PE + i] = -1.0
    sel[MLA_HALF + i, 2 * LANES + MISC_PE + MLA_HALF + i] = 1.0
    rope_sel = jnp.asarray(np.tile(sel, (3, 1)), BF16)
    for l in range(attn_norm.shape[0]):
        x = _layer(x, posr, invf_col, rope_sel, attn_norm[l], w_in[l], w_gate_up[l],
                   b_gate[l], gla_out_norm[l], q_a_norm[l], w_uq[l], kv_a_norm[l], w_ukv[l],
                   q_head_norm[l], k_head_norm[l], w_out[l], mlp_norm[l], w_up[l], w_down[l])
    return x
```

```python
import functools
import math

import jax
import jax.numpy as jnp
import numpy as np
from jax import lax
from jax.experimental import pallas as pl
from jax.experimental.pallas import tpu as pltpu

F32 = jnp.float32
BF16 = jnp.bfloat16

CHUNK = 64
EPS = 1e-6

GLA_HEADS = 4
GLA_DK = 64
GLA_DV = 128
GLA_GATE_RANK = 16
GLA_GATE_NORMALIZER = 16.0
GLA_LOG_GATE_MIN = -1.0
GLA_QK_W = GLA_HEADS * GLA_DK
GLA_V_W = GLA_HEADS * GLA_DV

MLA_HEADS = 8
MLA_Q_RANK = 256
MLA_KV_RANK = 128
MLA_NOPE = 64
MLA_ROPE = 32
MLA_HALF = MLA_ROPE // 2
MLA_QK = MLA_NOPE + MLA_ROPE
MLA_V = 64
ROPE_BASE = 10000.0

LANES = 128
HEAD_PAD = LANES
BF16_SUBLANES = 16
V_AUG = MLA_V + BF16_SUBLANES
LOG2_E = math.log2(math.e)
NEG = -1e30

H_NOPE = MLA_ROPE
A_Q = 0
A_K = A_Q + GLA_QK_W
A_V = A_K + GLA_QK_W
A_W = A_V + GLA_V_W
B_G = 0
B_CQ = B_G + GLA_V_W
B_CKV = B_CQ + MLA_Q_RANK
B_MISC = B_CKV + MLA_KV_RANK
B_W = B_MISC + LANES
MISC_PE = 0
MISC_GATE = MLA_ROPE

PROJ_PIECE = 256
TOKEN_BLOCK = 1024
MLP_BLOCK = 1024
GLA_BLOCK = 1024
ATTN_TQ = 4096
ATTN_TK = 256
QUERY_GROUP = 256
QK_LOOKAHEAD = 3
FF_BLOCK = 1024
VMEM_LIMIT = 56 * 1024 * 1024


def _nt(a, b):
    return lax.dot_general(a, b, (((1,), (1,)), ((), ())), preferred_element_type=F32)


def _tn(a, b):
    return lax.dot_general(a, b, (((0,), (0,)), ((), ())), preferred_element_type=F32)


def _rms(v):
    return v * lax.rsqrt(jnp.mean(v * v, axis=-1, keepdims=True) + EPS)


def _proj_kernel(x_ref, posr_ref, g_attn_ref, w_a_ref, w_b_ref, w_gate_ref, b_gate_ref,
                 qa_g_ref, w_uqt_ref, kva_g_ref, w_k_ref, w_vt_ref,
                 qg_col_ref, kg_nope_ref, kg_rope_ref, invf_col_ref, rope_sel_ref,
                 zqk_ref, loga_ref, zv_ref, gate_ref, qt_ref, k_ref, vt_ref):
    hb = (_rms(x_ref[0]) * g_attn_ref[...]).astype(BF16)

    def in_proj(w_ref, c0, c1):
        return jnp.dot(hb, w_ref[:, c0:c1].astype(BF16), preferred_element_type=F32)

    zg0 = in_proj(w_b_ref, B_G, B_G + PROJ_PIECE)
    zg1 = in_proj(w_b_ref, B_G + PROJ_PIECE, B_CQ)
    z_mla = in_proj(w_b_ref, B_CQ, B_W)
    gate_ref[0, :, :PROJ_PIECE] = (zg0 * jax.nn.sigmoid(zg0)).astype(BF16)
    gate_ref[0, :, PROJ_PIECE:] = (zg1 * jax.nn.sigmoid(zg1)).astype(BF16)
    zcq = z_mla[:, :B_CKV - B_CQ]
    zckv = z_mla[:, B_CKV - B_CQ:B_MISC - B_CQ]
    misc = z_mla[:, B_MISC - B_CQ:]

    cq = (_rms(zcq) * qa_g_ref[...]).astype(BF16)
    ckv = (_rms(zckv) * kva_g_ref[...]).astype(BF16)
    vt = _nt(w_vt_ref[...], ckv).astype(BF16)
    ang_t = invf_col_ref[...] * posr_ref[0].astype(F32)
    cos_t = jnp.cos(ang_t)
    sin_t = jnp.sin(ang_t)
    cs3 = jnp.concatenate(_split3(jnp.concatenate([cos_t, sin_t], axis=0)), axis=0)
    tabs = _tn(cs3, rope_sel_ref[...])

    m_hi = misc.astype(BF16)
    m_lo = (misc - m_hi.astype(F32)).astype(BF16)
    logit = jnp.dot(jnp.concatenate([m_hi, m_hi, m_lo], axis=1), w_gate_ref[...],
                    preferred_element_type=F32)
    logit = logit + b_gate_ref[...]
    log_sig = jnp.minimum(logit, 0.0) - jnp.log1p(jnp.exp(-jnp.abs(logit)))
    loga_ref[0] = jnp.maximum(log_sig / GLA_GATE_NORMALIZER, GLA_LOG_GATE_MIN)

    ones_rows = jnp.ones((V_AUG - MLA_V, vt.shape[1]), BF16)
    for hh in range(MLA_HEADS):
        vt_ref[0, hh * V_AUG:hh * V_AUG + MLA_V, :] = vt[hh * MLA_V:(hh + 1) * MLA_V, :]
        vt_ref[0, hh * V_AUG + MLA_V:(hh + 1) * V_AUG, :] = ones_rows

    qscale = MLA_QK ** -0.5 * LOG2_E
    r1 = MLA_HALF
    r2 = H_NOPE
    r3 = H_NOPE + MLA_NOPE

    def q_head(hh, blk):
        base = hh * HEAD_PAD
        ssq = jnp.sum(blk * blk, axis=0, keepdims=True)
        rs = lax.rsqrt(ssq * (1.0 / MLA_QK) + EPS) * qscale
        qn = blk * rs * qg_col_ref[...]
        x1 = qn[:r1, :]
        x2 = qn[r1:r2, :]
        qt_ref[0, base:base + r1, :] = (x1 * cos_t - x2 * sin_t).astype(BF16)
        qt_ref[0, base + r1:base + r2, :] = (x1 * sin_t + x2 * cos_t).astype(BF16)
        qt_ref[0, base + r2:base + r3, :] = qn[r2:r3, :].astype(BF16)
        qt_ref[0, base + r3:base + HEAD_PAD, :] = jnp.zeros((HEAD_PAD - r3, blk.shape[1]), BF16)

    lane = lax.broadcasted_iota(jnp.int32, (1, LANES), 1)
    is_pe = (lane >= MISC_PE) & (lane < MISC_PE + MLA_ROPE)
    kpe = jnp.where(is_pe, misc, 0.0)
    ssq_pe = jnp.sum(kpe * kpe, axis=-1, keepdims=True)
    xg = kpe * kg_rope_ref[...]
    rot = (xg * tabs[:, :LANES]
           + pltpu.roll(xg, LANES - MLA_HALF, 1) * tabs[:, LANES:2 * LANES]
           + pltpu.roll(xg, MLA_HALF, 1) * tabs[:, 2 * LANES:])

    def k_head(hh, kn_h):
        base = hh * HEAD_PAD
        ssq = jnp.sum(kn_h * kn_h, axis=-1, keepdims=True) + ssq_pe
        rs = lax.rsqrt(ssq * (1.0 / MLA_QK) + EPS)
        k_ref[0, :, base:base + HEAD_PAD] = ((kn_h * kg_nope_ref[...] + rot) * rs).astype(BF16)

    pieces = [(zqk_ref, 0, A_Q), (zqk_ref, PROJ_PIECE, A_K),
              (zv_ref, 0, A_V), (zv_ref, PROJ_PIECE, A_V + PROJ_PIECE)]
    group = MLA_HEADS // len(pieces)
    for i, (ref, lane0, c0) in enumerate(pieces):
        rows = slice(i * group * HEAD_PAD, (i + 1) * group * HEAD_PAD)
        ref[0, :, lane0:lane0 + PROJ_PIECE] = in_proj(w_a_ref, c0, c0 + PROJ_PIECE).astype(ref.dtype)
        qt = _nt(w_uqt_ref[rows, :], cq)
        kn = jnp.dot(ckv, w_k_ref[:, rows], preferred_element_type=F32)
        for n in range(group):
            q_head(i * group + n, qt[n * HEAD_PAD:(n + 1) * HEAD_PAD, :])
            k_head(i * group + n, kn[:, n * HEAD_PAD:(n + 1) * HEAD_PAD])


def _split3(v):
    hi = v.astype(BF16)
    r1 = v - hi.astype(F32)
    mid = r1.astype(BF16)
    lo = (r1 - mid.astype(F32)).astype(BF16)
    return hi, mid, lo


def _gla_kernel(zqk_ref, loga_ref, zv_ref, gate_ref, gout_ref, tri_ref, w_up_ref, w_dn_ref,
                o_ref, w_up_bf_ref, w_dn_bf_ref, st_ref, *, n_chunks):
    w_up_bf_ref[...] = w_up_ref[...].astype(BF16)
    w_dn_bf_ref[...] = w_dn_ref[...].astype(BF16)

    @pl.when(pl.program_id(1) == 0)
    def _():
        st_ref[...] = jnp.zeros(st_ref.shape, F32)

    sr = lax.broadcasted_iota(jnp.int32, (GLA_HEADS * CHUNK, CHUNK), 0)
    sc_ = lax.broadcasted_iota(jnp.int32, (GLA_HEADS * CHUNK, CHUNK), 1)
    causal4 = (sr % CHUNK) >= sc_
    qr = lax.broadcasted_iota(jnp.int32, (GLA_HEADS * CHUNK, GLA_QK_W), 0)
    qc = lax.broadcasted_iota(jnp.int32, (GLA_HEADS * CHUNK, GLA_QK_W), 1)
    head_sel = (qr // CHUNK) == (qc // GLA_DK)
    lane_head = lax.broadcasted_iota(jnp.int32, (GLA_DV, GLA_QK_W), 1) // GLA_DK
    gout = gout_ref[...]
    chunks = [slice(ci * CHUNK, (ci + 1) * CHUNK) for ci in range(n_chunks)]

    la = loga_ref[0]
    parts = [p for r in chunks for p in _split3(la[r])]
    tri_out = jnp.dot(tri_ref[...], jnp.concatenate(parts, axis=1), preferred_element_type=F32)
    w = GLA_QK_W
    cum = [tri_out[:, (3 * ci) * w:(3 * ci + 1) * w] + tri_out[:, (3 * ci + 1) * w:(3 * ci + 2) * w]
           + tri_out[:, (3 * ci + 2) * w:(3 * ci + 3) * w] for ci in range(n_chunks)]
    cl = [c[CHUNK - 1:CHUNK, :] for c in cum]

    def prep(ci):
        r = chunks[ci]
        q = zqk_ref[0, r, 0:GLA_QK_W]
        k = zqk_ref[0, r, GLA_QK_W:2 * GLA_QK_W]
        qd = (q * (GLA_DK ** -0.5) * jnp.exp(cum[ci])).astype(BF16)
        k_inv = (k * jnp.exp(-cum[ci])).astype(BF16)
        k_end = (k * jnp.exp(cl[ci] - cum[ci])).astype(BF16)
        q_stack = jnp.where(head_sel, jnp.concatenate([qd] * GLA_HEADS, axis=0), 0.0)
        return q_stack, k_inv, k_end

    def intra_scores(ci):
        return jnp.where(causal4, _nt(pre[ci][0], pre[ci][1]), 0.0).astype(BF16)

    def intra_out_and_kv(ci):
        v = zv_ref[0, chunks[ci], :]
        full = [jnp.dot(scores[ci][hh * CHUNK:(hh + 1) * CHUNK, :],
                        v[:, hh * GLA_DV:(hh + 1) * GLA_DV], preferred_element_type=F32)
                for hh in range(GLA_HEADS)]
        kv_t = _tn(v, pre[ci][2])
        kv = kv_t[(GLA_HEADS - 1) * GLA_DV:, :]
        for hh in range(GLA_HEADS - 2, -1, -1):
            kv = jnp.where(lane_head == hh, kv_t[hh * GLA_DV:(hh + 1) * GLA_DV, :], kv)
        return full, kv

    pre, scores, intra = {}, {}, {}
    for t in range(n_chunks + 2):
        if t < n_chunks:
            pre[t] = prep(t)
        if 0 <= t - 1 < n_chunks:
            scores[t - 1] = intra_scores(t - 1)
        if 0 <= t - 2 < n_chunks:
            intra[t - 2] = intra_out_and_kv(t - 2)

    st = st_ref[...]
    st_in = []
    for ci in range(n_chunks):
        st_in.append(st.astype(BF16))
        st = st * jnp.exp(cl[ci]) + intra[ci][1]
    st_ref[...] = st

    def finish(ci, inter):
        r = chunks[ci]
        for hh in range(GLA_HEADS):
            cols = slice(hh * GLA_DV, (hh + 1) * GLA_DV)
            o_h = inter[hh * CHUNK:(hh + 1) * CHUNK, :] + intra[ci][0][hh]
            g_h = gate_ref[0, r, cols].astype(F32)
            o_ref[0, r, cols] = (_rms(o_h) * gout * g_h).astype(BF16)

    inter = {}
    for t in range(n_chunks + 1):
        if t < n_chunks:
            inter[t] = _nt(pre[t][0], st_in[t])
        if t >= 1:
            finish(t - 1, inter[t - 1])


def _attn_kernel(qt_ref, k_ref, vt_ref, o_ref, s_ref, acc_ref, m_ref, bm_ref, *, tq, tk):
    qi = pl.program_id(2)
    ndiag = tq // tk

    m_ref[...] = jnp.full(m_ref.shape, NEG, F32)
    acc_ref[...] = jnp.zeros(acc_ref.shape, F32)

    kr = lax.broadcasted_iota(jnp.int32, (tk, tk), 0)
    qc = lax.broadcasted_iota(jnp.int32, (tk, tk), 1)
    square_mask = (kr // CHUNK) <= (qc // CHUNK)

    def qk(j, slot, c0=0, c1=tq):
        start = pl.multiple_of(j * tk, tk)
        s = jnp.dot(k_ref[0, pl.ds(start, tk), :], qt_ref[0, :, c0:c1],
                    preferred_element_type=F32)
        s_ref[slot, :, c0:c1] = s
        bm_ref[slot, :, c0:c1] = jnp.max(s, axis=0, keepdims=True)

    def mask_square(slot, c0):
        cols = slice(c0, c0 + tk)
        s = jnp.where(square_mask, s_ref[slot, :, cols], NEG)
        s_ref[slot, :, cols] = s
        bm_ref[slot, :, cols] = jnp.max(s, axis=0, keepdims=True)

    def softmax_pv(j, slots, c0=0, c1=tq):
        m_old = m_ref[:, c0:c1]
        m_new = m_old
        for slot in slots:
            m_new = jnp.maximum(m_new, bm_ref[slot, :, c0:c1])
        m_ref[:, c0:c1] = m_new
        p = jnp.concatenate([jnp.exp2(s_ref[slot, :, c0:c1] - m_new).astype(BF16)
                             for slot in slots], axis=0)
        start = pl.multiple_of(j * tk, tk)
        pv = jnp.dot(vt_ref[0, :, pl.ds(start, len(slots) * tk)], p,
                     preferred_element_type=F32)
        acc_ref[:, c0:c1] = jnp.exp2(m_old - m_new) * acc_ref[:, c0:c1] + pv

    ngroups = tq // QUERY_GROUP
    per_iter = 2 * ngroups

    def cols(g):
        return g * QUERY_GROUP, (g + 1) * QUERY_GROUP

    def qk_item(j, ps, g, both=True):
        qk(j, 2 * ps, *cols(g))
        if both:
            qk(j + 1, 2 * ps + 1, *cols(g))

    for u in range(QK_LOOKAHEAD):
        qk_item(0, 0, u)

    def body(i, carry):
        for u in range(per_iter):
            v = u + QK_LOOKAHEAD
            if v < per_iter:
                qk_item(4 * i + 2 * (v // ngroups), v // ngroups, v % ngroups)
            else:
                qk_item(4 * i + 4, 0, v - per_iter)
            ps, g = divmod(u, ngroups)
            softmax_pv(4 * i + 2 * ps, (2 * ps, 2 * ps + 1), *cols(g))
        return carry

    lax.fori_loop(0, qi * (ndiag // 4), body, 0)

    nfull = qi * ndiag
    items = [(p, g) for p in range(ndiag // 2) for g in range(2 * p, ngroups)]
    for t, (p, g) in enumerate(items):
        if t + QK_LOOKAHEAD < len(items):
            p2, g2 = items[t + QK_LOOKAHEAD]
            qk_item(nfull + 2 * p2, p2 % 2, g2, both=(g2 != 2 * p2))
        d = 2 * p
        slots = (2 * (p % 2), 2 * (p % 2) + 1)
        if g == d:
            mask_square(slots[0], d * tk)
            softmax_pv(nfull + d, slots[:1], *cols(g))
        else:
            if g == d + 1:
                mask_square(slots[1], (d + 1) * tk)
            softmax_pv(nfull + d, slots, *cols(g))

    acc = acc_ref[...]
    o_ref[0] = (acc[:MLA_V, :] / acc[MLA_V:MLA_V + 1, :]).astype(BF16)


def _mlp_kernel(x_ref, og_ref, ot_ref, w_out_ref, g_mlp_ref, w_up_ref, w_dn_ref,
                y_ref, *, ff_block):
    mix = jnp.dot(og_ref[0], w_out_ref[:GLA_V_W, :], preferred_element_type=F32)
    mix = mix + _tn(ot_ref[0], w_out_ref[GLA_V_W:, :])
    x1 = x_ref[0] + mix
    h = (_rms(x1) * g_mlp_ref[...]).astype(BF16)
    acc = x1
    d_ff = w_up_ref.shape[1]
    for f in range(d_ff // ff_block):
        cols = slice(f * ff_block, (f + 1) * ff_block)
        u = jnp.dot(h, w_up_ref[:, cols], preferred_element_type=F32)
        a = jnp.square(jnp.maximum(u, 0.0)).astype(BF16)
        acc = acc + jnp.dot(a, w_dn_ref[cols, :], preferred_element_type=F32)
    y_ref[0] = acc


def _const_spec(shape, single_buffer=False):
    nd = len(shape)
    mode = pl.Buffered(1) if single_buffer else None
    return pl.BlockSpec(shape, lambda *_: (0,) * nd, pipeline_mode=mode)


def _pack_weights(w_in, w_gate_up, b_gate, w_uq, w_ukv, q_head_norm, k_head_norm):
    o_gate = 2 * GLA_QK_W + GLA_V_W
    o_rest = o_gate + GLA_GATE_RANK
    rest = w_in[:, o_rest:]
    assert A_W == o_gate and rest.shape[1] == B_MISC + MLA_ROPE and MISC_PE == 0
    gate_at = B_MISC + MISC_GATE
    w_b = (jnp.pad(rest, ((0, 0), (0, B_W - rest.shape[1])))
           + jnp.pad(w_in[:, o_gate:o_rest], ((0, 0), (gate_at, B_W - gate_at - GLA_GATE_RANK)))
           ).astype(BF16)

    w_gate_f = jnp.pad(w_gate_up.astype(F32),
                       ((MISC_GATE, LANES - MISC_GATE - GLA_GATE_RANK), (0, 0)))
    g_hi = w_gate_f.astype(BF16)
    g_lo = (w_gate_f - g_hi.astype(F32)).astype(BF16)
    w_gate_p = jnp.concatenate([g_hi, g_lo, g_hi], axis=0)
    b_gate_p = b_gate.reshape(1, GLA_QK_W).astype(F32)

    w_uq_h = w_uq.reshape(MLA_Q_RANK, MLA_HEADS, MLA_QK)
    w_uq_h = jnp.concatenate([w_uq_h[:, :, MLA_NOPE:], w_uq_h[:, :, :MLA_NOPE]], axis=2)
    w_uq_h = jnp.pad(w_uq_h, ((0, 0), (0, 0), (0, HEAD_PAD - MLA_QK)))
    w_uqt = w_uq_h.reshape(MLA_Q_RANK, MLA_HEADS * HEAD_PAD).T.astype(BF16)

    nope_pad = (H_NOPE, HEAD_PAD - H_NOPE - MLA_NOPE)
    w_ukv_h = w_ukv.reshape(MLA_KV_RANK, MLA_HEADS, MLA_NOPE + MLA_V)
    w_k = jnp.pad(w_ukv_h[:, :, :MLA_NOPE], ((0, 0), (0, 0), nope_pad))
    w_k = w_k.reshape(MLA_KV_RANK, MLA_HEADS * HEAD_PAD).astype(BF16)
    w_vt = w_ukv_h[:, :, MLA_NOPE:].reshape(MLA_KV_RANK, MLA_HEADS * MLA_V).T.astype(BF16)

    qg = jnp.concatenate([q_head_norm[MLA_NOPE:], q_head_norm[:MLA_NOPE]])
    qg_col = jnp.pad(qg, (0, HEAD_PAD - MLA_QK)).reshape(HEAD_PAD, 1).astype(F32)
    kg_nope = jnp.pad(k_head_norm[:MLA_NOPE], nope_pad).reshape(1, LANES).astype(F32)
    kg_rope = jnp.pad(k_head_norm[MLA_NOPE:], (MISC_PE, LANES - MISC_PE - MLA_ROPE))
    kg_rope = kg_rope.reshape(1, LANES).astype(F32)
    return w_b, w_gate_p, b_gate_p, w_uqt, w_k, w_vt, qg_col, kg_nope, kg_rope


def _layer(x, posr, invf_col, rope_sel, attn_norm, w_in, w_gate_up, b_gate, gla_out_norm,
           q_a_norm, w_uq, kv_a_norm, w_ukv, q_head_norm, k_head_norm, w_out, mlp_norm, w_up,
           w_down):
    b_, s_, d_model = x.shape
    tm = min(TOKEN_BLOCK, s_)
    tg = min(GLA_BLOCK, s_)
    tq = min(ATTN_TQ, s_)
    tk = min(ATTN_TK, tq)
    assert s_ % tm == 0 and s_ % tg == 0 and s_ % tq == 0 and tk % CHUNK == 0
    assert tq % (4 * tk) == 0
    assert tk == QUERY_GROUP and QK_LOOKAHEAD <= tq // QUERY_GROUP
    d_ff = w_up.shape[1]
    ff_block = min(FF_BLOCK, d_ff)

    (w_b, w_gate_p, b_gate_p, w_uqt, w_k, w_vt, qg_col, kg_nope,
     kg_rope) = _pack_weights(w_in, w_gate_up, b_gate, w_uq, w_ukv, q_head_norm, k_head_norm)
    row = lambda v: v.reshape(1, -1).astype(F32)

    tok = lambda w: pl.BlockSpec((1, tm, w), lambda b, i: (b, i, 0))
    tok_t = lambda hgt: pl.BlockSpec((1, hgt, tm), lambda b, i: (b, 0, i))
    consts = (w_b, w_gate_p, b_gate_p, row(q_a_norm), w_uqt, row(kv_a_norm),
              w_k, w_vt, qg_col, kg_nope, kg_rope, invf_col, rope_sel)
    w_a_spec = pl.BlockSpec((d_model, A_W), lambda b, i: (0, 0))
    zqk, loga, zv, gate, qt, kk, vt = pl.pallas_call(
        _proj_kernel,
        grid=(b_, s_ // tm),
        in_specs=([tok(d_model), tok_t(1), _const_spec((1, d_model)), w_a_spec]
                  + [_const_spec(c.shape) for c in consts]),
        out_specs=[tok(2 * GLA_QK_W), tok(GLA_QK_W), tok(GLA_V_W), tok(GLA_V_W),
                   tok_t(MLA_HEADS * HEAD_PAD), tok(MLA_HEADS * HEAD_PAD), tok_t(MLA_HEADS * V_AUG)],
        out_shape=[jax.ShapeDtypeStruct((b_, s_, 2 * GLA_QK_W), F32),
                   jax.ShapeDtypeStruct((b_, s_, GLA_QK_W), F32),
                   jax.ShapeDtypeStruct((b_, s_, GLA_V_W), BF16),
                   jax.ShapeDtypeStruct((b_, s_, GLA_V_W), BF16),
                   jax.ShapeDtypeStruct((b_, MLA_HEADS * HEAD_PAD, s_), BF16),
                   jax.ShapeDtypeStruct((b_, s_, MLA_HEADS * HEAD_PAD), BF16),
                   jax.ShapeDtypeStruct((b_, MLA_HEADS * V_AUG, s_), BF16)],
        compiler_params=pltpu.CompilerParams(dimension_semantics=("parallel", "parallel"),
                                             vmem_limit_bytes=VMEM_LIMIT),
        name="proj",
    )(x, posr, row(attn_norm), w_in, *consts)

    gtok = lambda w: pl.BlockSpec((1, tg, w), lambda b, i: (b, i, 0))
    tri = jnp.asarray(np.tril(np.ones((CHUNK, CHUNK), np.float32)), BF16)
    n_gla = s_ // tg
    n_steps = b_ * n_gla
    assert d_model % (n_steps * BF16_SUBLANES) == 0 and d_ff % (n_steps * BF16_SUBLANES) == 0
    slab = lambda rows, w: pl.BlockSpec((rows // n_steps, w), lambda b, i: (b * n_gla + i, 0))
    o_gla, w_up_bf, w_dn_bf = pl.pallas_call(
        functools.partial(_gla_kernel, n_chunks=tg // CHUNK),
        grid=(b_, n_gla),
        in_specs=[gtok(2 * GLA_QK_W), gtok(GLA_QK_W), gtok(GLA_V_W), gtok(GLA_V_W),
                  _const_spec((1, GLA_DV)), _const_spec((CHUNK, CHUNK)),
                  slab(d_model, d_ff), slab(d_ff, d_model)],
        out_specs=[gtok(GLA_V_W), slab(d_model, d_ff), slab(d_ff, d_model)],
        out_shape=[jax.ShapeDtypeStruct((b_, s_, GLA_V_W), BF16),
                   jax.ShapeDtypeStruct((d_model, d_ff), BF16),
                   jax.ShapeDtypeStruct((d_ff, d_model), BF16)],
        scratch_shapes=[pltpu.VMEM((GLA_DV, GLA_QK_W), F32)],
        compiler_params=pltpu.CompilerParams(dimension_semantics=("parallel", "arbitrary"),
                                             vmem_limit_bytes=VMEM_LIMIT),
        name="gla",
    )(zqk, loga, zv, gate, row(gla_out_norm), tri, w_up, w_down)

    o_t = pl.pallas_call(
        functools.partial(_attn_kernel, tq=tq, tk=tk),
        grid=(b_, MLA_HEADS, s_ // tq),
        in_specs=[pl.BlockSpec((1, HEAD_PAD, tq), lambda b, h, i: (b, h, i)),
                  pl.BlockSpec((1, s_, HEAD_PAD), lambda b, h, i: (b, 0, h)),
                  pl.BlockSpec((1, V_AUG, s_), lambda b, h, i: (b, h, 0))],
        out_specs=pl.BlockSpec((1, MLA_V, tq), lambda b, h, i: (b, h, i)),
        out_shape=jax.ShapeDtypeStruct((b_, MLA_HEADS * MLA_V, s_), BF16),
        scratch_shapes=[pltpu.VMEM((4, tk, tq), F32), pltpu.VMEM((V_AUG, tq), F32),
                        pltpu.VMEM((1, tq), F32), pltpu.VMEM((4, 1, tq), F32)],
        compiler_params=pltpu.CompilerParams(
            dimension_semantics=("parallel", "parallel", "arbitrary"),
            vmem_limit_bytes=VMEM_LIMIT),
        name="attn",
    )(qt, kk, vt)

    tmm = min(MLP_BLOCK, s_)
    assert s_ % tmm == 0
    mtok = lambda w: pl.BlockSpec((1, tmm, w), lambda b, i: (b, i, 0))
    y = pl.pallas_call(
        functools.partial(_mlp_kernel, ff_block=ff_block),
        grid=(b_, s_ // tmm),
        in_specs=[mtok(d_model), mtok(GLA_V_W),
                  pl.BlockSpec((1, MLA_HEADS * MLA_V, tmm), lambda b, i: (b, 0, i)),
                  _const_spec(w_out.shape, True), _const_spec((1, d_model)),
                  _const_spec(w_up.shape, True), _const_spec(w_down.shape, True)],
        out_specs=mtok(d_model),
        out_shape=jax.ShapeDtypeStruct((b_, s_, d_model), x.dtype),
        compiler_params=pltpu.CompilerParams(dimension_semantics=("parallel", "parallel"),
                                             vmem_limit_bytes=VMEM_LIMIT),
        name="mlp",
    )(x, o_gla, o_t, w_out.astype(BF16), row(mlp_norm), w_up_bf, w_dn_bf)
    return y


def kernel(x, positions, attn_norm, w_in, w_gate_up, b_gate, gla_out_norm, q_a_norm, w_uq,
           kv_a_norm, w_ukv, q_head_norm, k_head_norm, w_out, mlp_norm, w_up, w_down):
    b_, s_, _ = x.shape
    posr = positions.reshape(b_, 1, s_)
    inv_freq = ROPE_BASE ** (-jnp.arange(0, MLA_ROPE, 2, dtype=F32) / MLA_ROPE)
    invf_col = inv_freq.reshape(MLA_HALF, 1)
    i = np.arange(MLA_HALF)
    sel = np.zeros((MLA_ROPE, 3 * LANES), np.float32)
    sel[i, MISC_PE + i] = 1.0
    sel[i, MISC_PE + MLA_HALF + i] = 1.0
    sel[MLA_HALF + i, LANES + MISC_PE + i] = -1.0
    sel[MLA_HALF + i, 2 * LANES + MISC_PE + MLA_HALF + i] = 1.0
    rope_sel = jnp.asarray(np.tile(sel, (3, 1)), BF16)
    for l in range(attn_norm.shape[0]):
        x = _layer(x, posr, invf_col, rope_sel, attn_norm[l], w_in[l], w_gate_up[l],
                   b_gate[l], gla_out_norm[l], q_a_norm[l], w_uq[l], kv_a_norm[l], w_ukv[l],
                   q_head_norm[l], k_head_norm[l], w_out[l], mlp_norm[l], w_up[l], w_down[l])
    return x
```

```python
import functools
import math

import jax
import jax.numpy as jnp
import numpy as np
from jax import lax
from jax.experimental import pallas as pl
from jax.experimental.pallas import tpu as pltpu

F32 = jnp.float32
BF16 = jnp.bfloat16

CHUNK = 64
EPS = 1e-6

GLA_HEADS = 4
GLA_DK = 64
GLA_DV = 128
GLA_GATE_RANK = 16
GLA_GATE_NORMALIZER = 16.0
GLA_LOG_GATE_MIN = -1.0
GLA_QK_W = GLA_HEADS * GLA_DK
GLA_V_W = GLA_HEADS * GLA_DV

MLA_HEADS = 8
MLA_Q_RANK = 256
MLA_KV_RANK = 128
MLA_NOPE = 64
MLA_ROPE = 32
MLA_HALF = MLA_ROPE // 2
MLA_QK = MLA_NOPE + MLA_ROPE
MLA_V = 64
ROPE_BASE = 10000.0

LANES = 128
HEAD_PAD = LANES
BF16_SUBLANES = 16
V_AUG = MLA_V + BF16_SUBLANES
LOG2_E = math.log2(math.e)
NEG = -1e30

H_NOPE = MLA_ROPE
A_Q = 0
A_K = A_Q + GLA_QK_W
A_V = A_K + GLA_QK_W
A_W = A_V + GLA_V_W
B_G = 0
B_CQ = B_G + GLA_V_W
B_CKV = B_CQ + MLA_Q_RANK
B_MISC = B_CKV + MLA_KV_RANK
B_W = B_MISC + LANES
MISC_PE = 0
MISC_GATE = MLA_ROPE

PROJ_PIECE = 256
TOKEN_BLOCK = 1024
MLP_BLOCK = 1024
GLA_BLOCK = 1024
ATTN_TQ = 4096
ATTN_TK = 256
QUERY_GROUP = 256
QK_LOOKAHEAD = 3
FF_BLOCK = 1024
VMEM_LIMIT = 56 * 1024 * 1024


def _nt(a, b):
    return lax.dot_general(a, b, (((1,), (1,)), ((), ())), preferred_element_type=F32)


def _tn(a, b):
    return lax.dot_general(a, b, (((0,), (0,)), ((), ())), preferred_element_type=F32)


def _rms(v):
    return v * lax.rsqrt(jnp.mean(v * v, axis=-1, keepdims=True) + EPS)


def _proj_kernel(x_ref, posr_ref, g_attn_ref, w_a_ref, w_b_ref, w_gate_ref, b_gate_ref,
                 qa_g_ref, w_uqt_ref, kva_g_ref, w_k_ref, w_vt_ref,
                 qg_col_ref, kg_nope_ref, kg_rope_ref, invf_col_ref, rope_sel_ref,
                 zqk_ref, loga_ref, zv_ref, gate_ref, qt_ref, k_ref, vt_ref):
    hb = (_rms(x_ref[0]) * g_attn_ref[...]).astype(BF16)

    def in_proj(w_ref, c0, c1):
        return jnp.dot(hb, w_ref[:, c0:c1].astype(BF16), preferred_element_type=F32)

    zg0 = in_proj(w_b_ref, B_G, B_G + PROJ_PIECE)
    zg1 = in_proj(w_b_ref, B_G + PROJ_PIECE, B_CQ)
    z_mla = in_proj(w_b_ref, B_CQ, B_W)
    gate_ref[0, :, :PROJ_PIECE] = (zg0 * jax.nn.sigmoid(zg0)).astype(BF16)
    gate_ref[0, :, PROJ_PIECE:] = (zg1 * jax.nn.sigmoid(zg1)).astype(BF16)
    zcq = z_mla[:, :B_CKV - B_CQ]
    zckv = z_mla[:, B_CKV - B_CQ:B_MISC - B_CQ]
    misc = z_mla[:, B_MISC - B_CQ:]

    cq = (_rms(zcq) * qa_g_ref[...]).astype(BF16)
    ckv = (_rms(zckv) * kva_g_ref[...]).astype(BF16)
    vt = _nt(w_vt_ref[...], ckv).astype(BF16)
    ang_t = invf_col_ref[...] * posr_ref[0].astype(F32)
    cos_t = jnp.cos(ang_t)
    sin_t = jnp.sin(ang_t)
    cs3 = jnp.concatenate(_split3(jnp.concatenate([cos_t, sin_t], axis=0)), axis=0)
    tabs = _tn(cs3, rope_sel_ref[...])

    m_hi = misc.astype(BF16)
    m_lo = (misc - m_hi.astype(F32)).astype(BF16)
    logit = jnp.dot(jnp.concatenate([m_hi, m_hi, m_lo], axis=1), w_gate_ref[...],
                    preferred_element_type=F32)
    logit = logit + b_gate_ref[...]
    log_sig = jnp.minimum(logit, 0.0) - jnp.log1p(jnp.exp(-jnp.abs(logit)))
    loga_ref[0] = jnp.maximum(log_sig / GLA_GATE_NORMALIZER, GLA_LOG_GATE_MIN)

    ones_rows = jnp.ones((V_AUG - MLA_V, vt.shape[1]), BF16)
    for hh in range(MLA_HEADS):
        vt_ref[0, hh * V_AUG:hh * V_AUG + MLA_V, :] = vt[hh * MLA_V:(hh + 1) * MLA_V, :]
        vt_ref[0, hh * V_AUG + MLA_V:(hh + 1) * V_AUG, :] = ones_rows

    qscale = MLA_QK ** -0.5 * LOG2_E
    r1 = MLA_HALF
    r2 = H_NOPE
    r3 = H_NOPE + MLA_NOPE

    def q_head(hh, blk):
        base = hh * HEAD_PAD
        ssq = jnp.sum(blk * blk, axis=0, keepdims=True)
        rs = lax.rsqrt(ssq * (1.0 / MLA_QK) + EPS) * qscale
        qn = blk * rs * qg_col_ref[...]
        x1 = qn[:r1, :]
        x2 = qn[r1:r2, :]
        qt_ref[0, base:base + r1, :] = (x1 * cos_t - x2 * sin_t).astype(BF16)
        qt_ref[0, base + r1:base + r2, :] = (x1 * sin_t + x2 * cos_t).astype(BF16)
        qt_ref[0, base + r2:base + r3, :] = qn[r2:r3, :].astype(BF16)
        qt_ref[0, base + r3:base + HEAD_PAD, :] = jnp.zeros((HEAD_PAD - r3, blk.shape[1]), BF16)

    lane = lax.broadcasted_iota(jnp.int32, (1, LANES), 1)
    is_pe = (lane >= MISC_PE) & (lane < MISC_PE + MLA_ROPE)
    kpe = jnp.where(is_pe, misc, 0.0)
    ssq_pe = jnp.sum(kpe * kpe, axis=-1, keepdims=True)
    xg = kpe * kg_rope_ref[...]
    rot = (xg * tabs[:, :LANES]
           + pltpu.roll(xg, LANES - MLA_HALF, 1) * tabs[:, LANES:2 * LANES]
           + pltpu.roll(xg, MLA_HALF, 1) * tabs[:, 2 * LANES:])

    def k_head(hh, kn_h):
        base = hh * HEAD_PAD
        ssq = jnp.sum(kn_h * kn_h, axis=-1, keepdims=True) + ssq_pe
        rs = lax.rsqrt(ssq * (1.0 / MLA_QK) + EPS)
        k_ref[0, :, base:base + HEAD_PAD] = ((kn_h * kg_nope_ref[...] + rot) * rs).astype(BF16)

    pieces = [(zqk_ref, 0, A_Q), (zqk_ref, PROJ_PIECE, A_K),
              (zv_ref, 0, A_V), (zv_ref, PROJ_PIECE, A_V + PROJ_PIECE)]
    group = MLA_HEADS // len(pieces)
    for i, (ref, lane0, c0) in enumerate(pieces):
        rows = slice(i * group * HEAD_PAD, (i + 1) * group * HEAD_PAD)
        ref[0, :, lane0:lane0 + PROJ_PIECE] = in_proj(w_a_ref, c0, c0 + PROJ_PIECE).astype(ref.dtype)
        qt = _nt(w_uqt_ref[rows, :], cq)
        kn = jnp.dot(ckv, w_k_ref[:, rows], preferred_element_type=F32)
        for n in range(group):
            q_head(i * group + n, qt[n * HEAD_PAD:(n + 1) * HEAD_PAD, :])
            k_head(i * group + n, kn[:, n * HEAD_PAD:(n + 1) * HEAD_PAD])


def _split3(v):
    hi = v.astype(BF16)
    r1 = v - hi.astype(F32)
    mid = r1.astype(BF16)
    lo = (r1 - mid.astype(F32)).astype(BF16)
    return hi, mid, lo


def _gla_kernel(zqk_ref, loga_ref, zv_ref, gate_ref, gout_ref, tri_ref, o_ref, st_ref, *,
                n_chunks):
    @pl.when(pl.program_id(1) == 0)
    def _():
        st_ref[...] = jnp.zeros(st_ref.shape, F32)

    sr = lax.broadcasted_iota(jnp.int32, (GLA_HEADS * CHUNK, CHUNK), 0)
    sc_ = lax.broadcasted_iota(jnp.int32, (GLA_HEADS * CHUNK, CHUNK), 1)
    causal4 = (sr % CHUNK) >= sc_
    qr = lax.broadcasted_iota(jnp.int32, (GLA_HEADS * CHUNK, GLA_QK_W), 0)
    qc = lax.broadcasted_iota(jnp.int32, (GLA_HEADS * CHUNK, GLA_QK_W), 1)
    head_sel = (qr // CHUNK) == (qc // GLA_DK)
    lane_head = lax.broadcasted_iota(jnp.int32, (GLA_DV, GLA_QK_W), 1) // GLA_DK
    gout = gout_ref[...]
    chunks = [slice(ci * CHUNK, (ci + 1) * CHUNK) for ci in range(n_chunks)]

    la = loga_ref[0]
    parts = [p for r in chunks for p in _split3(la[r])]
    tri_out = jnp.dot(tri_ref[...], jnp.concatenate(parts, axis=1), preferred_element_type=F32)
    w = GLA_QK_W
    cum = [tri_out[:, (3 * ci) * w:(3 * ci + 1) * w] + tri_out[:, (3 * ci + 1) * w:(3 * ci + 2) * w]
           + tri_out[:, (3 * ci + 2) * w:(3 * ci + 3) * w] for ci in range(n_chunks)]
    cl = [c[CHUNK - 1:CHUNK, :] for c in cum]

    def prep(ci):
        r = chunks[ci]
        q = zqk_ref[0, r, 0:GLA_QK_W]
        k = zqk_ref[0, r, GLA_QK_W:2 * GLA_QK_W]
        qd = (q * (GLA_DK ** -0.5) * jnp.exp(cum[ci])).astype(BF16)
        k_inv = (k * jnp.exp(-cum[ci])).astype(BF16)
        k_end = (k * jnp.exp(cl[ci] - cum[ci])).astype(BF16)
        q_stack = jnp.where(head_sel, jnp.concatenate([qd] * GLA_HEADS, axis=0), 0.0)
        return q_stack, k_inv, k_end

    def intra_scores(ci):
        return jnp.where(causal4, _nt(pre[ci][0], pre[ci][1]), 0.0).astype(BF16)

    def intra_out_and_kv(ci):
        v = zv_ref[0, chunks[ci], :]
        full = [jnp.dot(scores[ci][hh * CHUNK:(hh + 1) * CHUNK, :],
                        v[:, hh * GLA_DV:(hh + 1) * GLA_DV], preferred_element_type=F32)
                for hh in range(GLA_HEADS)]
        kv_t = _tn(v, pre[ci][2])
        kv = kv_t[(GLA_HEADS - 1) * GLA_DV:, :]
        for hh in range(GLA_HEADS - 2, -1, -1):
            kv = jnp.where(lane_head == hh, kv_t[hh * GLA_DV:(hh + 1) * GLA_DV, :], kv)
        return full, kv

    pre, scores, intra = {}, {}, {}
    for t in range(n_chunks + 2):
        if t < n_chunks:
            pre[t] = prep(t)
        if 0 <= t - 1 < n_chunks:
            scores[t - 1] = intra_scores(t - 1)
        if 0 <= t - 2 < n_chunks:
            intra[t - 2] = intra_out_and_kv(t - 2)

    st = st_ref[...]
    st_in = []
    for ci in range(n_chunks):
        st_in.append(st.astype(BF16))
        st = st * jnp.exp(cl[ci]) + intra[ci][1]
    st_ref[...] = st

    def finish(ci, inter):
        r = chunks[ci]
        for hh in range(GLA_HEADS):
            cols = slice(hh * GLA_DV, (hh + 1) * GLA_DV)
            o_h = inter[hh * CHUNK:(hh + 1) * CHUNK, :] + intra[ci][0][hh]
            g_h = gate_ref[0, r, cols].astype(F32)
            o_ref[0, r, cols] = (_rms(o_h) * gout * g_h).astype(BF16)

    inter = {}
    for t in range(n_chunks + 1):
        if t < n_chunks:
            inter[t] = _nt(pre[t][0], st_in[t])
        if t >= 1:
            finish(t - 1, inter[t - 1])


def _attn_kernel(qt_ref, k_ref, vt_ref, w_up_ref, w_dn_ref, o_ref, w_up_bf_ref, w_dn_bf_ref,
                 s_ref, acc_ref, m_ref, bm_ref, *, tq, tk):
    qi = pl.program_id(2)
    ndiag = tq // tk

    m_ref[...] = jnp.full(m_ref.shape, NEG, F32)
    acc_ref[...] = jnp.zeros(acc_ref.shape, F32)

    kr = lax.broadcasted_iota(jnp.int32, (tk, tk), 0)
    qc = lax.broadcasted_iota(jnp.int32, (tk, tk), 1)
    square_mask = (kr // CHUNK) <= (qc // CHUNK)

    def qk(j, slot, c0=0, c1=tq):
        start = pl.multiple_of(j * tk, tk)
        s = jnp.dot(k_ref[0, pl.ds(start, tk), :], qt_ref[0, :, c0:c1],
                    preferred_element_type=F32)
        s_ref[slot, :, c0:c1] = s
        bm_ref[slot, :, c0:c1] = jnp.max(s, axis=0, keepdims=True)

    def mask_square(slot, c0):
        cols = slice(c0, c0 + tk)
        s = jnp.where(square_mask, s_ref[slot, :, cols], NEG)
        s_ref[slot, :, cols] = s
        bm_ref[slot, :, cols] = jnp.max(s, axis=0, keepdims=True)

    def softmax_pv(j, slots, c0=0, c1=tq):
        m_old = m_ref[:, c0:c1]
        m_new = m_old
        for slot in slots:
            m_new = jnp.maximum(m_new, bm_ref[slot, :, c0:c1])
        m_ref[:, c0:c1] = m_new
        p = jnp.concatenate([jnp.exp2(s_ref[slot, :, c0:c1] - m_new).astype(BF16)
                             for slot in slots], axis=0)
        start = pl.multiple_of(j * tk, tk)
        pv = jnp.dot(vt_ref[0, :, pl.ds(start, len(slots) * tk)], p,
                     preferred_element_type=F32)
        acc_ref[:, c0:c1] = jnp.exp2(m_old - m_new) * acc_ref[:, c0:c1] + pv

    ngroups = tq // QUERY_GROUP
    per_iter = 2 * ngroups

    def cols(g):
        return g * QUERY_GROUP, (g + 1) * QUERY_GROUP

    def qk_item(j, ps, g, both=True):
        qk(j, 2 * ps, *cols(g))
        if both:
            qk(j + 1, 2 * ps + 1, *cols(g))

    for u in range(QK_LOOKAHEAD):
        qk_item(0, 0, u)

    def body(i, carry):
        for u in range(per_iter):
            v = u + QK_LOOKAHEAD
            if v < per_iter:
                qk_item(4 * i + 2 * (v // ngroups), v // ngroups, v % ngroups)
            else:
                qk_item(4 * i + 4, 0, v - per_iter)
            ps, g = divmod(u, ngroups)
            softmax_pv(4 * i + 2 * ps, (2 * ps, 2 * ps + 1), *cols(g))
        return carry

    lax.fori_loop(0, qi * (ndiag // 4), body, 0)

    nfull = qi * ndiag
    items = [(p, g) for p in range(ndiag // 2) for g in range(2 * p, ngroups)]
    for t, (p, g) in enumerate(items):
        if t == len(items) // 2:
            w_up_bf_ref[...] = w_up_ref[...].astype(BF16)
            w_dn_bf_ref[...] = w_dn_ref[...].astype(BF16)
        if t + QK_LOOKAHEAD < len(items):
            p2, g2 = items[t + QK_LOOKAHEAD]
            qk_item(nfull + 2 * p2, p2 % 2, g2, both=(g2 != 2 * p2))
        d = 2 * p
        slots = (2 * (p % 2), 2 * (p % 2) + 1)
        if g == d:
            mask_square(slots[0], d * tk)
            softmax_pv(nfull + d, slots[:1], *cols(g))
        else:
            if g == d + 1:
                mask_square(slots[1], (d + 1) * tk)
            softmax_pv(nfull + d, slots, *cols(g))

    acc = acc_ref[...]
    o_ref[0] = (acc[:MLA_V, :] / acc[MLA_V:MLA_V + 1, :]).astype(BF16)


def _mlp_kernel(x_ref, og_ref, ot_ref, w_out_ref, g_mlp_ref, w_up_ref, w_dn_ref,
                y_ref, *, ff_block):
    mix = jnp.dot(og_ref[0], w_out_ref[:GLA_V_W, :], preferred_element_type=F32)
    mix = mix + _tn(ot_ref[0], w_out_ref[GLA_V_W:, :])
    x1 = x_ref[0] + mix
    h = (_rms(x1) * g_mlp_ref[...]).astype(BF16)
    acc = x1
    d_ff = w_up_ref.shape[1]
    for f in range(d_ff // ff_block):
        cols = slice(f * ff_block, (f + 1) * ff_block)
        u = jnp.dot(h, w_up_ref[:, cols], preferred_element_type=F32)
        a = jnp.square(jnp.maximum(u, 0.0)).astype(BF16)
        acc = acc + jnp.dot(a, w_dn_ref[cols, :], preferred_element_type=F32)
    y_ref[0] = acc


def _const_spec(shape, single_buffer=False):
    nd = len(shape)
    mode = pl.Buffered(1) if single_buffer else None
    return pl.BlockSpec(shape, lambda *_: (0,) * nd, pipeline_mode=mode)


def _pack_weights(w_in, w_gate_up, b_gate, w_uq, w_ukv, q_head_norm, k_head_norm):
    o_gate = 2 * GLA_QK_W + GLA_V_W
    o_rest = o_gate + GLA_GATE_RANK
    rest = w_in[:, o_rest:]
    assert A_W == o_gate and rest.shape[1] == B_MISC + MLA_ROPE and MISC_PE == 0
    gate_at = B_MISC + MISC_GATE
    w_b = (jnp.pad(rest, ((0, 0), (0, B_W - rest.shape[1])))
           + jnp.pad(w_in[:, o_gate:o_rest], ((0, 0), (gate_at, B_W - gate_at - GLA_GATE_RANK)))
           ).astype(BF16)

    w_gate_f = jnp.pad(w_gate_up.astype(F32),
                       ((MISC_GATE, LANES - MISC_GATE - GLA_GATE_RANK), (0, 0)))
    g_hi = w_gate_f.astype(BF16)
    g_lo = (w_gate_f - g_hi.astype(F32)).astype(BF16)
    w_gate_p = jnp.concatenate([g_hi, g_lo, g_hi], axis=0)
    b_gate_p = b_gate.reshape(1, GLA_QK_W).astype(F32)

    w_uq_h = w_uq.reshape(MLA_Q_RANK, MLA_HEADS, MLA_QK)
    w_uq_h = jnp.concatenate([w_uq_h[:, :, MLA_NOPE:], w_uq_h[:, :, :MLA_NOPE]], axis=2)
    w_uq_h = jnp.pad(w_uq_h, ((0, 0), (0, 0), (0, HEAD_PAD - MLA_QK)))
    w_uqt = w_uq_h.reshape(MLA_Q_RANK, MLA_HEADS * HEAD_PAD).T.astype(BF16)

    nope_pad = (H_NOPE, HEAD_PAD - H_NOPE - MLA_NOPE)
    w_ukv_h = w_ukv.reshape(MLA_KV_RANK, MLA_HEADS, MLA_NOPE + MLA_V)
    w_k = jnp.pad(w_ukv_h[:, :, :MLA_NOPE], ((0, 0), (0, 0), nope_pad))
    w_k = w_k.reshape(MLA_KV_RANK, MLA_HEADS * HEAD_PAD).astype(BF16)
    w_vt = w_ukv_h[:, :, MLA_NOPE:].reshape(MLA_KV_RANK, MLA_HEADS * MLA_V).T.astype(BF16)

    qg = jnp.concatenate([q_head_norm[MLA_NOPE:], q_head_norm[:MLA_NOPE]])
    qg_col = jnp.pad(qg, (0, HEAD_PAD - MLA_QK)).reshape(HEAD_PAD, 1).astype(F32)
    kg_nope = jnp.pad(k_head_norm[:MLA_NOPE], nope_pad).reshape(1, LANES).astype(F32)
    kg_rope = jnp.pad(k_head_norm[MLA_NOPE:], (MISC_PE, LANES - MISC_PE - MLA_ROPE))
    kg_rope = kg_rope.reshape(1, LANES).astype(F32)
    return w_b, w_gate_p, b_gate_p, w_uqt, w_k, w_vt, qg_col, kg_nope, kg_rope


def _layer(x, posr, invf_col, rope_sel, attn_norm, w_in, w_gate_up, b_gate, gla_out_norm,
           q_a_norm, w_uq, kv_a_norm, w_ukv, q_head_norm, k_head_norm, w_out, mlp_norm, w_up,
           w_down):
    b_, s_, d_model = x.shape
    tm = min(TOKEN_BLOCK, s_)
    tg = min(GLA_BLOCK, s_)
    tq = min(ATTN_TQ, s_)
    tk = min(ATTN_TK, tq)
    assert s_ % tm == 0 and s_ % tg == 0 and s_ % tq == 0 and tk % CHUNK == 0
    assert tq % (4 * tk) == 0
    assert tk == QUERY_GROUP and QK_LOOKAHEAD <= tq // QUERY_GROUP
    d_ff = w_up.shape[1]
    ff_block = min(FF_BLOCK, d_ff)

    (w_b, w_gate_p, b_gate_p, w_uqt, w_k, w_vt, qg_col, kg_nope,
     kg_rope) = _pack_weights(w_in, w_gate_up, b_gate, w_uq, w_ukv, q_head_norm, k_head_norm)
    row = lambda v: v.reshape(1, -1).astype(F32)

    tok = lambda w: pl.BlockSpec((1, tm, w), lambda b, i: (b, i, 0))
    tok_t = lambda hgt: pl.BlockSpec((1, hgt, tm), lambda b, i: (b, 0, i))
    consts = (w_b, w_gate_p, b_gate_p, row(q_a_norm), w_uqt, row(kv_a_norm),
              w_k, w_vt, qg_col, kg_nope, kg_rope, invf_col, rope_sel)
    w_a_spec = pl.BlockSpec((d_model, A_W), lambda b, i: (0, 0))
    zqk, loga, zv, gate, qt, kk, vt = pl.pallas_call(
        _proj_kernel,
        grid=(b_, s_ // tm),
        in_specs=([tok(d_model), tok_t(1), _const_spec((1, d_model)), w_a_spec]
                  + [_const_spec(c.shape) for c in consts]),
        out_specs=[tok(2 * GLA_QK_W), tok(GLA_QK_W), tok(GLA_V_W), tok(GLA_V_W),
                   tok_t(MLA_HEADS * HEAD_PAD), tok(MLA_HEADS * HEAD_PAD), tok_t(MLA_HEADS * V_AUG)],
        out_shape=[jax.ShapeDtypeStruct((b_, s_, 2 * GLA_QK_W), F32),
                   jax.ShapeDtypeStruct((b_, s_, GLA_QK_W), F32),
                   jax.ShapeDtypeStruct((b_, s_, GLA_V_W), BF16),
                   jax.ShapeDtypeStruct((b_, s_, GLA_V_W), BF16),
                   jax.ShapeDtypeStruct((b_, MLA_HEADS * HEAD_PAD, s_), BF16),
                   jax.ShapeDtypeStruct((b_, s_, MLA_HEADS * HEAD_PAD), BF16),
                   jax.ShapeDtypeStruct((b_, MLA_HEADS * V_AUG, s_), BF16)],
        compiler_params=pltpu.CompilerParams(dimension_semantics=("parallel", "parallel"),
                                             vmem_limit_bytes=VMEM_LIMIT),
        name="proj",
    )(x, posr, row(attn_norm), w_in, *consts)

    gtok = lambda w: pl.BlockSpec((1, tg, w), lambda b, i: (b, i, 0))
    tri = jnp.asarray(np.tril(np.ones((CHUNK, CHUNK), np.float32)), BF16)
    o_gla = pl.pallas_call(
        functools.partial(_gla_kernel, n_chunks=tg // CHUNK),
        grid=(b_, s_ // tg),
        in_specs=[gtok(2 * GLA_QK_W), gtok(GLA_QK_W), gtok(GLA_V_W), gtok(GLA_V_W),
                  _const_spec((1, GLA_DV)), _const_spec((CHUNK, CHUNK))],
        out_specs=gtok(GLA_V_W),
        out_shape=jax.ShapeDtypeStruct((b_, s_, GLA_V_W), BF16),
        scratch_shapes=[pltpu.VMEM((GLA_DV, GLA_QK_W), F32)],
        compiler_params=pltpu.CompilerParams(dimension_semantics=("parallel", "arbitrary"),
                                             vmem_limit_bytes=VMEM_LIMIT),
        name="gla",
    )(zqk, loga, zv, gate, row(gla_out_norm), tri)

    n_q = s_ // tq
    n_steps = b_ * MLA_HEADS * n_q
    assert d_model % (n_steps * BF16_SUBLANES) == 0 and d_ff % (n_steps * BF16_SUBLANES) == 0
    slab = lambda rows, w: pl.BlockSpec((rows // n_steps, w),
                                        lambda b, h, i: ((b * MLA_HEADS + h) * n_q + i, 0))
    o_t, w_up_bf, w_dn_bf = pl.pallas_call(
        functools.partial(_attn_kernel, tq=tq, tk=tk),
        grid=(b_, MLA_HEADS, n_q),
        in_specs=[pl.BlockSpec((1, HEAD_PAD, tq), lambda b, h, i: (b, h, i)),
                  pl.BlockSpec((1, s_, HEAD_PAD), lambda b, h, i: (b, 0, h)),
                  pl.BlockSpec((1, V_AUG, s_), lambda b, h, i: (b, h, 0)),
                  slab(d_model, d_ff), slab(d_ff, d_model)],
        out_specs=[pl.BlockSpec((1, MLA_V, tq), lambda b, h, i: (b, h, i)),
                   slab(d_model, d_ff), slab(d_ff, d_model)],
        out_shape=[jax.ShapeDtypeStruct((b_, MLA_HEADS * MLA_V, s_), BF16),
                   jax.ShapeDtypeStruct((d_model, d_ff), BF16),
                   jax.ShapeDtypeStruct((d_ff, d_model), BF16)],
        scratch_shapes=[pltpu.VMEM((4, tk, tq), F32), pltpu.VMEM((V_AUG, tq), F32),
                        pltpu.VMEM((1, tq), F32), pltpu.VMEM((4, 1, tq), F32)],
        compiler_params=pltpu.CompilerParams(
            dimension_semantics=("parallel", "parallel", "arbitrary"),
            vmem_limit_bytes=VMEM_LIMIT),
        name="attn",
    )(qt, kk, vt, w_up, w_down)

    tmm = min(MLP_BLOCK, s_)
    assert s_ % tmm == 0
    mtok = lambda w: pl.BlockSpec((1, tmm, w), lambda b, i: (b, i, 0))
    y = pl.pallas_call(
        functools.partial(_mlp_kernel, ff_block=ff_block),
        grid=(b_, s_ // tmm),
        in_specs=[mtok(d_model), mtok(GLA_V_W),
                  pl.BlockSpec((1, MLA_HEADS * MLA_V, tmm), lambda b, i: (b, 0, i)),
                  _const_spec(w_out.shape, True), _const_spec((1, d_model)),
                  _const_spec(w_up.shape, True), _const_spec(w_down.shape, True)],
        out_specs=mtok(d_model),
        out_shape=jax.ShapeDtypeStruct((b_, s_, d_model), x.dtype),
        compiler_params=pltpu.CompilerParams(dimension_semantics=("parallel", "parallel"),
                                             vmem_limit_bytes=VMEM_LIMIT),
        name="mlp",
    )(x, o_gla, o_t, w_out.astype(BF16), row(mlp_norm), w_up_bf, w_dn_bf)
    return y


def kernel(x, positions, attn_norm, w_in, w_gate_up, b_gate, gla_out_norm, q_a_norm, w_uq,
           kv_a_norm, w_ukv, q_head_norm, k_head_norm, w_out, mlp_norm, w_up, w_down):
    b_, s_, _ = x.shape
    posr = positions.reshape(b_, 1, s_)
    inv_freq = ROPE_BASE ** (-jnp.arange(0, MLA_ROPE, 2, dtype=F32) / MLA_ROPE)
    invf_col = inv_freq.reshape(MLA_HALF, 1)
    i = np.arange(MLA_HALF)
    sel = np.zeros((MLA_ROPE, 3 * LANES), np.float32)
    sel[i, MISC_PE + i] = 1.0
    sel[i, MISC_PE + MLA_HALF + i] = 1.0
    sel[MLA_HALF + i, LANES + MISC_PE + i] = -1.0
    sel[MLA_HALF + i, 2 * LANES + MISC_PE + MLA_HALF + i] = 1.0
    rope_sel = jnp.asarray(np.tile(sel, (3, 1)), BF16)
    for l in range(attn_norm.shape[0]):
        x = _layer(x, posr, invf_col, rope_sel, attn_norm[l], w_in[l], w_gate_up[l],
                   b_gate[l], gla_out_norm[l], q_a_norm[l], w_uq[l], kv_a_norm[l], w_ukv[l],
                   q_head_norm[l], k_head_norm[l], w_out[l], mlp_norm[l], w_up[l], w_down[l])
    return x
```

```python
import functools
import math

import jax
import jax.numpy as jnp
import numpy as np
from jax import lax
from jax.experimental import pallas as pl
from jax.experimental.pallas import tpu as pltpu

F32 = jnp.float32
BF16 = jnp.bfloat16

CHUNK = 64
EPS = 1e-6

GLA_HEADS = 4
GLA_DK = 64
GLA_DV = 128
GLA_GATE_RANK = 16
GLA_GATE_NORMALIZER = 16.0
GLA_LOG_GATE_MIN = -1.0
GLA_QK_W = GLA_HEADS * GLA_DK
GLA_V_W = GLA_HEADS * GLA_DV

MLA_HEADS = 8
MLA_Q_RANK = 256
MLA_KV_RANK = 128
MLA_NOPE = 64
MLA_ROPE = 32
MLA_HALF = MLA_ROPE // 2
MLA_QK = MLA_NOPE + MLA_ROPE
MLA_V = 64
ROPE_BASE = 10000.0

LANES = 128
HEAD_PAD = LANES
BF16_SUBLANES = 16
V_AUG = MLA_V + BF16_SUBLANES
LOG2_E = math.log2(math.e)
NEG = -1e30

H_NOPE = MLA_ROPE
A_Q = 0
A_K = A_Q + GLA_QK_W
A_V = A_K + GLA_QK_W
A_W = A_V + GLA_V_W
B_G = 0
B_CQ = B_G + GLA_V_W
B_CKV = B_CQ + MLA_Q_RANK
B_MISC = B_CKV + MLA_KV_RANK
B_W = B_MISC + LANES
MISC_PE = 0
MISC_GATE = MLA_ROPE
ROT_NEG = MLA_ROPE
ROT_POS = 2 * MLA_ROPE

PROJ_PIECE = 256
TOKEN_BLOCK = 1024
MLP_BLOCK = 1024
ATTN_TQ = 4096
ATTN_TK = 256
QUERY_GROUP = 256
QK_LOOKAHEAD = 3
FF_BLOCK = 1024
VMEM_LIMIT = 56 * 1024 * 1024


def _nt(a, b):
    return lax.dot_general(a, b, (((1,), (1,)), ((), ())), preferred_element_type=F32)


def _tn(a, b):
    return lax.dot_general(a, b, (((0,), (0,)), ((), ())), preferred_element_type=F32)


def _rms(v):
    return v * lax.rsqrt(jnp.mean(v * v, axis=-1, keepdims=True) + EPS)


def _proj_kernel(x_ref, posr_ref, vec_ref, w_a_ref, w_b_ref, w_gate_ref, w_uqt_ref, w_k_ref,
                 w_vt_ref, qg_col_ref, invf_col_ref, rope_sel_ref, gout_ref, tri_ref,
                 o_gla_ref, qt_ref, k_ref, vt_ref,
                 zqk_ref, loga_ref, zv_ref, gate_ref, st_ref):
    d_model = x_ref.shape[2]
    offs = np.cumsum([0, d_model, MLA_Q_RANK, MLA_KV_RANK, GLA_QK_W, LANES, LANES])
    g_attn, qa_g, kva_g, b_gate, kg_nope, kg_rope = (
        vec_ref[:, int(a):int(b)] for a, b in zip(offs[:-1], offs[1:]))
    hb = (_rms(x_ref[0]) * g_attn).astype(BF16)

    def in_proj(w_ref, c0, c1):
        return jnp.dot(hb, w_ref[:, c0:c1].astype(BF16), preferred_element_type=F32)

    zg0 = in_proj(w_b_ref, B_G, B_G + PROJ_PIECE)
    zg1 = in_proj(w_b_ref, B_G + PROJ_PIECE, B_CQ)
    z_mla = in_proj(w_b_ref, B_CQ, B_W)
    gate_ref[0, :, :PROJ_PIECE] = (zg0 * jax.nn.sigmoid(zg0)).astype(BF16)
    gate_ref[0, :, PROJ_PIECE:] = (zg1 * jax.nn.sigmoid(zg1)).astype(BF16)
    zcq = z_mla[:, :B_CKV - B_CQ]
    zckv = z_mla[:, B_CKV - B_CQ:B_MISC - B_CQ]
    misc = z_mla[:, B_MISC - B_CQ:]

    cq = (_rms(zcq) * qa_g).astype(BF16)
    ckv = (_rms(zckv) * kva_g).astype(BF16)
    vt = _nt(w_vt_ref[...], ckv).astype(BF16)
    ang_t = invf_col_ref[...] * posr_ref[0].astype(F32)
    cos_t = jnp.cos(ang_t)
    sin_t = jnp.sin(ang_t)
    cs3 = jnp.concatenate(_split3(jnp.concatenate([cos_t, sin_t], axis=0)), axis=0)
    tab = _tn(cs3, rope_sel_ref[...])

    m_hi = misc.astype(BF16)
    m_lo = (misc - m_hi.astype(F32)).astype(BF16)
    logit = jnp.dot(jnp.concatenate([m_hi, m_hi, m_lo], axis=1), w_gate_ref[...],
                    preferred_element_type=F32)
    logit = logit + b_gate
    log_sig = jnp.minimum(logit, 0.0) - jnp.log1p(jnp.exp(-jnp.abs(logit)))
    loga_ref[0] = jnp.maximum(log_sig / GLA_GATE_NORMALIZER, GLA_LOG_GATE_MIN)

    ones_rows = jnp.ones((V_AUG - MLA_V, vt.shape[1]), BF16)
    for hh in range(MLA_HEADS):
        vt_ref[0, hh * V_AUG:hh * V_AUG + MLA_V, :] = vt[hh * MLA_V:(hh + 1) * MLA_V, :]
        vt_ref[0, hh * V_AUG + MLA_V:(hh + 1) * V_AUG, :] = ones_rows

    qscale = MLA_QK ** -0.5 * LOG2_E
    r1 = MLA_HALF
    r2 = H_NOPE
    r3 = H_NOPE + MLA_NOPE

    def q_head(hh, blk):
        base = hh * HEAD_PAD
        ssq = jnp.sum(blk * blk, axis=0, keepdims=True)
        rs = lax.rsqrt(ssq * (1.0 / MLA_QK) + EPS) * qscale
        qn = blk * rs * qg_col_ref[...]
        x1 = qn[:r1, :]
        x2 = qn[r1:r2, :]
        qt_ref[0, base:base + r1, :] = (x1 * cos_t - x2 * sin_t).astype(BF16)
        qt_ref[0, base + r1:base + r2, :] = (x1 * sin_t + x2 * cos_t).astype(BF16)
        qt_ref[0, base + r2:base + r3, :] = qn[r2:r3, :].astype(BF16)
        qt_ref[0, base + r3:base + HEAD_PAD, :] = jnp.zeros((HEAD_PAD - r3, blk.shape[1]), BF16)

    lane = lax.broadcasted_iota(jnp.int32, (1, LANES), 1)
    is_pe = (lane >= MISC_PE) & (lane < MISC_PE + MLA_ROPE)
    kpe = jnp.where(is_pe, misc, 0.0)
    ssq_pe = jnp.sum(kpe * kpe, axis=-1, keepdims=True)
    xg = kpe * kg_rope
    rot = (xg * tab
           + pltpu.roll(xg, LANES - MLA_HALF, 1) * pltpu.roll(tab, LANES - ROT_NEG, 1)
           + pltpu.roll(xg, MLA_HALF, 1) * pltpu.roll(tab, LANES - ROT_POS, 1))

    def k_head(hh, kn_h):
        base = hh * HEAD_PAD
        ssq = jnp.sum(kn_h * kn_h, axis=-1, keepdims=True) + ssq_pe
        rs = lax.rsqrt(ssq * (1.0 / MLA_QK) + EPS)
        k_ref[0, :, base:base + HEAD_PAD] = ((kn_h * kg_nope + rot) * rs).astype(BF16)

    pieces = [(zqk_ref, 0, A_Q), (zqk_ref, PROJ_PIECE, A_K),
              (zv_ref, 0, A_V), (zv_ref, PROJ_PIECE, A_V + PROJ_PIECE)]
    group = MLA_HEADS // len(pieces)
    for i, (ref, lane0, c0) in enumerate(pieces):
        rows = slice(i * group * HEAD_PAD, (i + 1) * group * HEAD_PAD)
        ref[0, :, lane0:lane0 + PROJ_PIECE] = in_proj(w_a_ref, c0, c0 + PROJ_PIECE).astype(ref.dtype)
        qt = _nt(w_uqt_ref[rows, :], cq)
        kn = jnp.dot(ckv, w_k_ref[:, rows], preferred_element_type=F32)
        for n in range(group):
            q_head(i * group + n, qt[n * HEAD_PAD:(n + 1) * HEAD_PAD, :])
            k_head(i * group + n, kn[:, n * HEAD_PAD:(n + 1) * HEAD_PAD])

    _gla_block(zqk_ref, loga_ref, zv_ref, gate_ref, gout_ref, tri_ref, o_gla_ref, st_ref,
               n_chunks=x_ref.shape[1] // CHUNK)


def _split3(v):
    hi = v.astype(BF16)
    r1 = v - hi.astype(F32)
    mid = r1.astype(BF16)
    lo = (r1 - mid.astype(F32)).astype(BF16)
    return hi, mid, lo


def _gla_block(zqk_ref, loga_ref, zv_ref, gate_ref, gout_ref, tri_ref, o_ref, st_ref, *,
               n_chunks):
    @pl.when(pl.program_id(1) == 0)
    def _():
        st_ref[...] = jnp.zeros(st_ref.shape, F32)

    sr = lax.broadcasted_iota(jnp.int32, (GLA_HEADS * CHUNK, CHUNK), 0)
    sc_ = lax.broadcasted_iota(jnp.int32, (GLA_HEADS * CHUNK, CHUNK), 1)
    causal4 = (sr % CHUNK) >= sc_
    qr = lax.broadcasted_iota(jnp.int32, (GLA_HEADS * CHUNK, GLA_QK_W), 0)
    qc = lax.broadcasted_iota(jnp.int32, (GLA_HEADS * CHUNK, GLA_QK_W), 1)
    head_sel = (qr // CHUNK) == (qc // GLA_DK)
    lane_head = lax.broadcasted_iota(jnp.int32, (GLA_DV, GLA_QK_W), 1) // GLA_DK
    gout = gout_ref[...]
    chunks = [slice(ci * CHUNK, (ci + 1) * CHUNK) for ci in range(n_chunks)]

    la = loga_ref[0]
    parts = [p for r in chunks for p in _split3(la[r])]
    tri_out = jnp.dot(tri_ref[...], jnp.concatenate(parts, axis=1), preferred_element_type=F32)
    w = GLA_QK_W
    cum = [tri_out[:, (3 * ci) * w:(3 * ci + 1) * w] + tri_out[:, (3 * ci + 1) * w:(3 * ci + 2) * w]
           + tri_out[:, (3 * ci + 2) * w:(3 * ci + 3) * w] for ci in range(n_chunks)]
    cl = [c[CHUNK - 1:CHUNK, :] for c in cum]

    def prep(ci):
        r = chunks[ci]
        q = zqk_ref[0, r, 0:GLA_QK_W]
        k = zqk_ref[0, r, GLA_QK_W:2 * GLA_QK_W]
        qd = (q * (GLA_DK ** -0.5) * jnp.exp(cum[ci])).astype(BF16)
        k_inv = (k * jnp.exp(-cum[ci])).astype(BF16)
        k_end = (k * jnp.exp(cl[ci] - cum[ci])).astype(BF16)
        q_stack = jnp.where(head_sel, jnp.concatenate([qd] * GLA_HEADS, axis=0), 0.0)
        return q_stack, k_inv, k_end

    def intra_scores(ci):
        return jnp.where(causal4, _nt(pre[ci][0], pre[ci][1]), 0.0).astype(BF16)

    def intra_out_and_kv(ci):
        v = zv_ref[0, chunks[ci], :]
        full = [jnp.dot(scores[ci][hh * CHUNK:(hh + 1) * CHUNK, :],
                        v[:, hh * GLA_DV:(hh + 1) * GLA_DV], preferred_element_type=F32)
                for hh in range(GLA_HEADS)]
        kv_t = _tn(v, pre[ci][2])
        kv = kv_t[(GLA_HEADS - 1) * GLA_DV:, :]
        for hh in range(GLA_HEADS - 2, -1, -1):
            kv = jnp.where(lane_head == hh, kv_t[hh * GLA_DV:(hh + 1) * GLA_DV, :], kv)
        return full, kv

    pre, scores, intra = {}, {}, {}
    for t in range(n_chunks + 2):
        if t < n_chunks:
            pre[t] = prep(t)
        if 0 <= t - 1 < n_chunks:
            scores[t - 1] = intra_scores(t - 1)
        if 0 <= t - 2 < n_chunks:
            intra[t - 2] = intra_out_and_kv(t - 2)

    st = st_ref[...]
    st_in = []
    for ci in range(n_chunks):
        st_in.append(st.astype(BF16))
        st = st * jnp.exp(cl[ci]) + intra[ci][1]
    st_ref[...] = st

    def finish(ci, inter):
        r = chunks[ci]
        for hh in range(GLA_HEADS):
            cols = slice(hh * GLA_DV, (hh + 1) * GLA_DV)
            o_h = inter[hh * CHUNK:(hh + 1) * CHUNK, :] + intra[ci][0][hh]
            g_h = gate_ref[0, r, cols].astype(F32)
            o_ref[0, r, cols] = (_rms(o_h) * gout * g_h).astype(BF16)

    inter = {}
    for t in range(n_chunks + 1):
        if t < n_chunks:
            inter[t] = _nt(pre[t][0], st_in[t])
        if t >= 1:
            finish(t - 1, inter[t - 1])


def _attn_kernel(qt_ref, k_ref, vt_ref, w_up_ref, w_dn_ref, o_ref, w_up_bf_ref, w_dn_bf_ref,
                 s_ref, acc_ref, m_ref, bm_ref, *, tq, tk):
    qi = pl.program_id(2)
    ndiag = tq // tk

    m_ref[...] = jnp.full(m_ref.shape, NEG, F32)
    acc_ref[...] = jnp.zeros(acc_ref.shape, F32)

    kr = lax.broadcasted_iota(jnp.int32, (tk, tk), 0)
    qc = lax.broadcasted_iota(jnp.int32, (tk, tk), 1)
    square_mask = (kr // CHUNK) <= (qc // CHUNK)

    def qk(j, slot, c0=0, c1=tq):
        start = pl.multiple_of(j * tk, tk)
        s = jnp.dot(k_ref[0, pl.ds(start, tk), :], qt_ref[0, :, c0:c1],
                    preferred_element_type=F32)
        s_ref[slot, :, c0:c1] = s
        bm_ref[slot, :, c0:c1] = jnp.max(s, axis=0, keepdims=True)

    def mask_square(slot, c0):
        cols = slice(c0, c0 + tk)
        s = jnp.where(square_mask, s_ref[slot, :, cols], NEG)
        s_ref[slot, :, cols] = s
        bm_ref[slot, :, cols] = jnp.max(s, axis=0, keepdims=True)

    def softmax_pv(j, slots, c0=0, c1=tq):
        m_old = m_ref[:, c0:c1]
        m_new = m_old
        for slot in slots:
            m_new = jnp.maximum(m_new, bm_ref[slot, :, c0:c1])
        m_ref[:, c0:c1] = m_new
        p = jnp.concatenate([jnp.exp2(s_ref[slot, :, c0:c1] - m_new).astype(BF16)
                             for slot in slots], axis=0)
        start = pl.multiple_of(j * tk, tk)
        pv = jnp.dot(vt_ref[0, :, pl.ds(start, len(slots) * tk)], p,
                     preferred_element_type=F32)
        acc_ref[:, c0:c1] = jnp.exp2(m_old - m_new) * acc_ref[:, c0:c1] + pv

    ngroups = tq // QUERY_GROUP
    per_iter = 2 * ngroups

    def cols(g):
        return g * QUERY_GROUP, (g + 1) * QUERY_GROUP

    def qk_item(j, ps, g, both=True):
        qk(j, 2 * ps, *cols(g))
        if both:
            qk(j + 1, 2 * ps + 1, *cols(g))

    for u in range(QK_LOOKAHEAD):
        qk_item(0, 0, u)

    def body(i, carry):
        for u in range(per_iter):
            v = u + QK_LOOKAHEAD
            if v < per_iter:
                qk_item(4 * i + 2 * (v // ngroups), v // ngroups, v % ngroups)
            else:
                qk_item(4 * i + 4, 0, v - per_iter)
            ps, g = divmod(u, ngroups)
            softmax_pv(4 * i + 2 * ps, (2 * ps, 2 * ps + 1), *cols(g))
        return carry

    lax.fori_loop(0, qi * (ndiag // 4), body, 0)

    nfull = qi * ndiag
    items = [(p, g) for p in range(ndiag // 2) for g in range(2 * p, ngroups)]
    for t, (p, g) in enumerate(items):
        if t == len(items) // 2:
            w_up_bf_ref[...] = w_up_ref[...].astype(BF16)
            w_dn_bf_ref[...] = w_dn_ref[...].astype(BF16)
        if t + QK_LOOKAHEAD < len(items):
            p2, g2 = items[t + QK_LOOKAHEAD]
            qk_item(nfull + 2 * p2, p2 % 2, g2, both=(g2 != 2 * p2))
        d = 2 * p
        slots = (2 * (p % 2), 2 * (p % 2) + 1)
        if g == d:
            mask_square(slots[0], d * tk)
            softmax_pv(nfull + d, slots[:1], *cols(g))
        else:
            if g == d + 1:
                mask_square(slots[1], (d + 1) * tk)
            softmax_pv(nfull + d, slots, *cols(g))

    acc = acc_ref[...]
    o_ref[0] = (acc[:MLA_V, :] / acc[MLA_V:MLA_V + 1, :]).astype(BF16)


def _mlp_kernel(x_ref, og_ref, ot_ref, w_out_ref, g_mlp_ref, w_up_ref, w_dn_ref,
                y_ref, *, ff_block):
    mix = jnp.dot(og_ref[0], w_out_ref[:GLA_V_W, :], preferred_element_type=F32)
    mix = mix + _tn(ot_ref[0], w_out_ref[GLA_V_W:, :])
    x1 = x_ref[0] + mix
    h = (_rms(x1) * g_mlp_ref[...]).astype(BF16)
    acc = x1
    d_ff = w_up_ref.shape[1]
    for f in range(d_ff // ff_block):
        cols = slice(f * ff_block, (f + 1) * ff_block)
        u = jnp.dot(h, w_up_ref[:, cols], preferred_element_type=F32)
        a = jnp.square(jnp.maximum(u, 0.0)).astype(BF16)
        acc = acc + jnp.dot(a, w_dn_ref[cols, :], preferred_element_type=F32)
    y_ref[0] = acc


def _const_spec(shape, single_buffer=False):
    nd = len(shape)
    mode = pl.Buffered(1) if single_buffer else None
    return pl.BlockSpec(shape, lambda *_: (0,) * nd, pipeline_mode=mode)


def _pack_weights(attn_norm, w_in, w_gate_up, b_gate, q_a_norm, w_uq, kv_a_norm, w_ukv,
                  q_head_norm, k_head_norm):
    o_gate = 2 * GLA_QK_W + GLA_V_W
    o_rest = o_gate + GLA_GATE_RANK
    n_rest = w_in.shape[1] - o_rest
    assert A_W == o_gate and n_rest == B_MISC + MLA_ROPE and MISC_PE == 0
    off_rest, off_gate = lax.optimization_barrier((jnp.int32(o_rest), jnp.int32(o_gate)))
    rest = lax.dynamic_slice(w_in, (0, off_rest), (w_in.shape[0], n_rest))
    gate = lax.dynamic_slice(w_in, (0, off_gate), (w_in.shape[0], GLA_GATE_RANK))
    gate_at = B_MISC + MISC_GATE
    w_b = (jnp.pad(rest, ((0, 0), (0, B_W - n_rest)))
           + jnp.pad(gate, ((0, 0), (gate_at, B_W - gate_at - GLA_GATE_RANK)))
           ).astype(BF16)

    w_gate_f = jnp.pad(w_gate_up.astype(F32),
                       ((MISC_GATE, LANES - MISC_GATE - GLA_GATE_RANK), (0, 0)))
    g_hi = w_gate_f.astype(BF16)
    g_lo = (w_gate_f - g_hi.astype(F32)).astype(BF16)
    w_gate_p = jnp.concatenate([g_hi, g_lo, g_hi], axis=0)

    w_uq_h = w_uq.reshape(MLA_Q_RANK, MLA_HEADS, MLA_QK)
    w_uq_h = jnp.concatenate([w_uq_h[:, :, MLA_NOPE:], w_uq_h[:, :, :MLA_NOPE]], axis=2)
    w_uq_h = jnp.pad(w_uq_h, ((0, 0), (0, 0), (0, HEAD_PAD - MLA_QK)))
    w_uqt = w_uq_h.reshape(MLA_Q_RANK, MLA_HEADS * HEAD_PAD).T.astype(BF16)

    nope_pad = (H_NOPE, HEAD_PAD - H_NOPE - MLA_NOPE)
    w_ukv_h = w_ukv.reshape(MLA_KV_RANK, MLA_HEADS, MLA_NOPE + MLA_V)
    w_k = jnp.pad(w_ukv_h[:, :, :MLA_NOPE], ((0, 0), (0, 0), nope_pad))
    w_k = w_k.reshape(MLA_KV_RANK, MLA_HEADS * HEAD_PAD).astype(BF16)
    w_vt = w_ukv_h[:, :, MLA_NOPE:].reshape(MLA_KV_RANK, MLA_HEADS * MLA_V).T.astype(BF16)

    qg = jnp.concatenate([q_head_norm[MLA_NOPE:], q_head_norm[:MLA_NOPE]])
    qg_col = jnp.pad(qg, (0, HEAD_PAD - MLA_QK)).reshape(HEAD_PAD, 1).astype(F32)
    zeros = lambda n: jnp.zeros((n,), F32)
    vec = jnp.concatenate([
        attn_norm, q_a_norm, kv_a_norm, b_gate,
        zeros(nope_pad[0]), k_head_norm[:MLA_NOPE], zeros(nope_pad[1]),
        zeros(MISC_PE), k_head_norm[MLA_NOPE:], zeros(LANES - MISC_PE - MLA_ROPE),
    ]).astype(F32).reshape(1, -1)
    return vec, w_b, w_gate_p, w_uqt, w_k, w_vt, qg_col


def _layer(x, posr, invf_col, rope_sel, attn_norm, w_in, w_gate_up, b_gate, gla_out_norm,
           q_a_norm, w_uq, kv_a_norm, w_ukv, q_head_norm, k_head_norm, w_out, mlp_norm, w_up,
           w_down):
    b_, s_, d_model = x.shape
    tm = min(TOKEN_BLOCK, s_)
    tq = min(ATTN_TQ, s_)
    tk = min(ATTN_TK, tq)
    assert s_ % tm == 0 and s_ % tq == 0 and tk % CHUNK == 0 and tm % CHUNK == 0
    assert tq % (4 * tk) == 0
    assert tk == QUERY_GROUP and QK_LOOKAHEAD <= tq // QUERY_GROUP
    d_ff = w_up.shape[1]
    ff_block = min(FF_BLOCK, d_ff)

    vec, w_b, w_gate_p, w_uqt, w_k, w_vt, qg_col = _pack_weights(
        attn_norm, w_in, w_gate_up, b_gate, q_a_norm, w_uq, kv_a_norm, w_ukv, q_head_norm,
        k_head_norm)
    row = lambda v: v.reshape(1, -1).astype(F32)

    tok = lambda w: pl.BlockSpec((1, tm, w), lambda b, i: (b, i, 0))
    tok_t = lambda hgt: pl.BlockSpec((1, hgt, tm), lambda b, i: (b, 0, i))
    consts = (w_b, w_gate_p, w_uqt, w_k, w_vt, qg_col, invf_col, rope_sel)
    w_a_spec = pl.BlockSpec((d_model, A_W), lambda b, i: (0, 0))
    tri = jnp.asarray(np.tril(np.ones((CHUNK, CHUNK), np.float32)), BF16)
    o_gla, qt, kk, vt = pl.pallas_call(
        _proj_kernel,
        grid=(b_, s_ // tm),
        in_specs=([tok(d_model), tok_t(1), _const_spec(vec.shape), w_a_spec]
                  + [_const_spec(c.shape) for c in consts]
                  + [_const_spec((1, GLA_DV)), _const_spec((CHUNK, CHUNK))]),
        out_specs=[tok(GLA_V_W), tok_t(MLA_HEADS * HEAD_PAD), tok(MLA_HEADS * HEAD_PAD),
                   tok_t(MLA_HEADS * V_AUG)],
        out_shape=[jax.ShapeDtypeStruct((b_, s_, GLA_V_W), BF16),
                   jax.ShapeDtypeStruct((b_, MLA_HEADS * HEAD_PAD, s_), BF16),
                   jax.ShapeDtypeStruct((b_, s_, MLA_HEADS * HEAD_PAD), BF16),
                   jax.ShapeDtypeStruct((b_, MLA_HEADS * V_AUG, s_), BF16)],
        scratch_shapes=[pltpu.VMEM((1, tm, 2 * GLA_QK_W), F32), pltpu.VMEM((1, tm, GLA_QK_W), F32),
                        pltpu.VMEM((1, tm, GLA_V_W), BF16), pltpu.VMEM((1, tm, GLA_V_W), BF16),
                        pltpu.VMEM((GLA_DV, GLA_QK_W), F32)],
        compiler_params=pltpu.CompilerParams(dimension_semantics=("parallel", "arbitrary"),
                                             vmem_limit_bytes=VMEM_LIMIT),
        name="proj_gla",
    )(x, posr, vec, w_in, *consts, row(gla_out_norm), tri)

    n_q = s_ // tq
    n_steps = b_ * MLA_HEADS * n_q
    assert d_model % (n_steps * BF16_SUBLANES) == 0 and d_ff % (n_steps * BF16_SUBLANES) == 0
    slab = lambda rows, w: pl.BlockSpec((rows // n_steps, w),
                                        lambda b, h, i: ((b * MLA_HEADS + h) * n_q + i, 0))
    o_t, w_up_bf, w_dn_bf = pl.pallas_call(
        functools.partial(_attn_kernel, tq=tq, tk=tk),
        grid=(b_, MLA_HEADS, n_q),
        in_specs=[pl.BlockSpec((1, HEAD_PAD, tq), lambda b, h, i: (b, h, i)),
                  pl.BlockSpec((1, s_, HEAD_PAD), lambda b, h, i: (b, 0, h)),
                  pl.BlockSpec((1, V_AUG, s_), lambda b, h, i: (b, h, 0)),
                  slab(d_model, d_ff), slab(d_ff, d_model)],
        out_specs=[pl.BlockSpec((1, MLA_V, tq), lambda b, h, i: (b, h, i)),
                   slab(d_model, d_ff), slab(d_ff, d_model)],
        out_shape=[jax.ShapeDtypeStruct((b_, MLA_HEADS * MLA_V, s_), BF16),
                   jax.ShapeDtypeStruct((d_model, d_ff), BF16),
                   jax.ShapeDtypeStruct((d_ff, d_model), BF16)],
        scratch_shapes=[pltpu.VMEM((4, tk, tq), F32), pltpu.VMEM((V_AUG, tq), F32),
                        pltpu.VMEM((1, tq), F32), pltpu.VMEM((4, 1, tq), F32)],
        compiler_params=pltpu.CompilerParams(
            dimension_semantics=("parallel", "parallel", "arbitrary"),
            vmem_limit_bytes=VMEM_LIMIT),
        name="attn",
    )(qt, kk, vt, w_up, w_down)

    tmm = min(MLP_BLOCK, s_)
    assert s_ % tmm == 0
    mtok = lambda w: pl.BlockSpec((1, tmm, w), lambda b, i: (b, i, 0))
    y = pl.pallas_call(
        functools.partial(_mlp_kernel, ff_block=ff_block),
        grid=(b_, s_ // tmm),
        in_specs=[mtok(d_model), mtok(GLA_V_W),
                  pl.BlockSpec((1, MLA_HEADS * MLA_V, tmm), lambda b, i: (b, 0, i)),
                  _const_spec(w_out.shape, True), _const_spec((1, d_model)),
                  _const_spec(w_up.shape, True), _const_spec(w_down.shape, True)],
        out_specs=mtok(d_model),
        out_shape=jax.ShapeDtypeStruct((b_, s_, d_model), x.dtype),
        compiler_params=pltpu.CompilerParams(dimension_semantics=("parallel", "parallel"),
                                             vmem_limit_bytes=VMEM_LIMIT),
        name="mlp",
    )(x, o_gla, o_t, w_out.astype(BF16), row(mlp_norm), w_up_bf, w_dn_bf)
    return y


def kernel(x, positions, attn_norm, w_in, w_gate_up, b_gate, gla_out_norm, q_a_norm, w_uq,
           kv_a_norm, w_ukv, q_head_norm, k_head_norm, w_out, mlp_norm, w_up, w_down):
    b_, s_, _ = x.shape
    posr = positions.reshape(b_, 1, s_)
    inv_freq = ROPE_BASE ** (-jnp.arange(0, MLA_ROPE, 2, dtype=F32) / MLA_ROPE)
    invf_col = inv_freq.reshape(MLA_HALF, 1)
    i = np.arange(MLA_HALF)
    sel = np.zeros((MLA_ROPE, LANES), np.float32)
    sel[i, MISC_PE + i] = 1.0
    sel[i, MISC_PE + MLA_HALF + i] = 1.0
    sel[MLA_HALF + i, ROT_NEG + MISC_PE + i] = -1.0
    sel[MLA_HALF + i, ROT_POS + MISC_PE + MLA_HALF + i] = 1.0
    rope_sel = jnp.asarray(np.tile(sel, (3, 1)), BF16)
    for l in range(attn_norm.shape[0]):
        x = _layer(x, posr, invf_col, rope_sel, attn_norm[l], w_in[l], w_gate_up[l],
                   b_gate[l], gla_out_norm[l], q_a_norm[l], w_uq[l], kv_a_norm[l], w_ukv[l],
                   q_head_norm[l], k_head_norm[l], w_out[l], mlp_norm[l], w_up[l], w_down[l])
    return x
```

```python
import functools
import math

import jax
import jax.numpy as jnp
import numpy as np
from jax import lax
from jax.experimental import pallas as pl
from jax.experimental.pallas import tpu as pltpu

F32 = jnp.float32
BF16 = jnp.bfloat16

CHUNK = 64
EPS = 1e-6

GLA_HEADS = 4
GLA_DK = 64
GLA_DV = 128
GLA_GATE_RANK = 16
GLA_GATE_NORMALIZER = 16.0
GLA_LOG_GATE_MIN = -1.0
GLA_QK_W = GLA_HEADS * GLA_DK
GLA_V_W = GLA_HEADS * GLA_DV

MLA_HEADS = 8
MLA_Q_RANK = 256
MLA_KV_RANK = 128
MLA_NOPE = 64
MLA_ROPE = 32
MLA_HALF = MLA_ROPE // 2
MLA_QK = MLA_NOPE + MLA_ROPE
MLA_V = 64
ROPE_BASE = 10000.0

LANES = 128
HEAD_PAD = LANES
BF16_SUBLANES = 16
V_AUG = MLA_V + BF16_SUBLANES
LOG2_E = math.log2(math.e)
NEG = -1e30

H_NOPE = MLA_ROPE
A_Q = 0
A_K = A_Q + GLA_QK_W
A_V = A_K + GLA_QK_W
A_W = A_V + GLA_V_W
B_G = 0
B_CQ = B_G + GLA_V_W
B_CKV = B_CQ + MLA_Q_RANK
B_MISC = B_CKV + MLA_KV_RANK
B_W = B_MISC + LANES
MISC_PE = 0
MISC_GATE = MLA_ROPE
ROT_NEG = MLA_ROPE
ROT_POS = 2 * MLA_ROPE

PROJ_PIECE = 256
TOKEN_BLOCK = 1024
MLP_BLOCK = 1024
ATTN_TQ = 4096
ATTN_TK = 256
QUERY_GROUP = 256
QK_LOOKAHEAD = 3
FF_BLOCK = 1024
VMEM_LIMIT = 56 * 1024 * 1024


def _nt(a, b):
    return lax.dot_general(a, b, (((1,), (1,)), ((), ())), preferred_element_type=F32)


def _tn(a, b):
    return lax.dot_general(a, b, (((0,), (0,)), ((), ())), preferred_element_type=F32)


def _rms(v):
    return v * lax.rsqrt(jnp.mean(v * v, axis=-1, keepdims=True) + EPS)


def _proj_kernel(x_ref, posr_ref, vec_ref, w_a_ref, w_gate_ref, w_uqt_ref, w_k_ref,
                 w_vt_ref, qg_col_ref, invf_col_ref, rope_sel_ref, gout_ref, tri_ref,
                 o_gla_ref, qt_ref, k_ref, vt_ref,
                 zqk_ref, loga_ref, zv_ref, gate_ref, st_ref, w_b_ref):
    d_model = x_ref.shape[2]
    offs = np.cumsum([0, d_model, MLA_Q_RANK, MLA_KV_RANK, GLA_QK_W, LANES, LANES])
    g_attn, qa_g, kva_g, b_gate, kg_nope, kg_rope = (
        vec_ref[:, int(a):int(b)] for a, b in zip(offs[:-1], offs[1:]))
    @pl.when((pl.program_id(0) == 0) & (pl.program_id(1) == 0))
    def _():
        o_rest = A_W + GLA_GATE_RANK
        rest = w_a_ref[:, o_rest:]
        gate = w_a_ref[:, A_W:o_rest]
        fill = jnp.zeros((rest.shape[0], B_W - rest.shape[1] - GLA_GATE_RANK), F32)
        w_b_ref[...] = jnp.concatenate([rest, gate, fill], axis=1).astype(BF16)

    hb = (_rms(x_ref[0]) * g_attn).astype(BF16)

    def in_proj(w_ref, c0, c1):
        return jnp.dot(hb, w_ref[:, c0:c1].astype(BF16), preferred_element_type=F32)

    zg0 = in_proj(w_b_ref, B_G, B_G + PROJ_PIECE)
    zg1 = in_proj(w_b_ref, B_G + PROJ_PIECE, B_CQ)
    z_mla = in_proj(w_b_ref, B_CQ, B_W)
    gate_ref[0, :, :PROJ_PIECE] = (zg0 * jax.nn.sigmoid(zg0)).astype(BF16)
    gate_ref[0, :, PROJ_PIECE:] = (zg1 * jax.nn.sigmoid(zg1)).astype(BF16)
    zcq = z_mla[:, :B_CKV - B_CQ]
    zckv = z_mla[:, B_CKV - B_CQ:B_MISC - B_CQ]
    misc = z_mla[:, B_MISC - B_CQ:]

    cq = (_rms(zcq) * qa_g).astype(BF16)
    ckv = (_rms(zckv) * kva_g).astype(BF16)
    vt = _nt(w_vt_ref[...], ckv).astype(BF16)
    ang_t = invf_col_ref[...] * posr_ref[0].astype(F32)
    cos_t = jnp.cos(ang_t)
    sin_t = jnp.sin(ang_t)
    cs3 = jnp.concatenate(_split3(jnp.concatenate([cos_t, sin_t], axis=0)), axis=0)
    tab = _tn(cs3, rope_sel_ref[...])

    m_hi = misc.astype(BF16)
    m_lo = (misc - m_hi.astype(F32)).astype(BF16)
    logit = jnp.dot(jnp.concatenate([m_hi, m_hi, m_lo], axis=1), w_gate_ref[...],
                    preferred_element_type=F32)
    logit = logit + b_gate
    log_sig = jnp.minimum(logit, 0.0) - jnp.log1p(jnp.exp(-jnp.abs(logit)))
    loga_ref[0] = jnp.maximum(log_sig / GLA_GATE_NORMALIZER, GLA_LOG_GATE_MIN)

    ones_rows = jnp.ones((V_AUG - MLA_V, vt.shape[1]), BF16)
    for hh in range(MLA_HEADS):
        vt_ref[0, hh * V_AUG:hh * V_AUG + MLA_V, :] = vt[hh * MLA_V:(hh + 1) * MLA_V, :]
        vt_ref[0, hh * V_AUG + MLA_V:(hh + 1) * V_AUG, :] = ones_rows

    qscale = MLA_QK ** -0.5 * LOG2_E
    r1 = MLA_HALF
    r2 = H_NOPE
    r3 = H_NOPE + MLA_NOPE

    def q_head(hh, blk):
        base = hh * HEAD_PAD
        ssq = jnp.sum(blk * blk, axis=0, keepdims=True)
        rs = lax.rsqrt(ssq * (1.0 / MLA_QK) + EPS) * qscale
        qn = blk * rs * qg_col_ref[...]
        x1 = qn[:r1, :]
        x2 = qn[r1:r2, :]
        qt_ref[0, base:base + r1, :] = (x1 * cos_t - x2 * sin_t).astype(BF16)
        qt_ref[0, base + r1:base + r2, :] = (x1 * sin_t + x2 * cos_t).astype(BF16)
        qt_ref[0, base + r2:base + r3, :] = qn[r2:r3, :].astype(BF16)
        qt_ref[0, base + r3:base + HEAD_PAD, :] = jnp.zeros((HEAD_PAD - r3, blk.shape[1]), BF16)

    lane = lax.broadcasted_iota(jnp.int32, (1, LANES), 1)
    is_pe = (lane >= MISC_PE) & (lane < MISC_PE + MLA_ROPE)
    kpe = jnp.where(is_pe, misc, 0.0)
    ssq_pe = jnp.sum(kpe * kpe, axis=-1, keepdims=True)
    xg = kpe * kg_rope
    rot = (xg * tab
           + pltpu.roll(xg, LANES - MLA_HALF, 1) * pltpu.roll(tab, LANES - ROT_NEG, 1)
           + pltpu.roll(xg, MLA_HALF, 1) * pltpu.roll(tab, LANES - ROT_POS, 1))

    def k_head(hh, kn_h):
        base = hh * HEAD_PAD
        ssq = jnp.sum(kn_h * kn_h, axis=-1, keepdims=True) + ssq_pe
        rs = lax.rsqrt(ssq * (1.0 / MLA_QK) + EPS)
        k_ref[0, :, base:base + HEAD_PAD] = ((kn_h * kg_nope + rot) * rs).astype(BF16)

    pieces = [(zqk_ref, 0, A_Q), (zqk_ref, PROJ_PIECE, A_K),
              (zv_ref, 0, A_V), (zv_ref, PROJ_PIECE, A_V + PROJ_PIECE)]
    group = MLA_HEADS // len(pieces)
    for i, (ref, lane0, c0) in enumerate(pieces):
        rows = slice(i * group * HEAD_PAD, (i + 1) * group * HEAD_PAD)
        ref[0, :, lane0:lane0 + PROJ_PIECE] = in_proj(w_a_ref, c0, c0 + PROJ_PIECE).astype(ref.dtype)
        qt = _nt(w_uqt_ref[rows, :], cq)
        kn = jnp.dot(ckv, w_k_ref[:, rows], preferred_element_type=F32)
        for n in range(group):
            q_head(i * group + n, qt[n * HEAD_PAD:(n + 1) * HEAD_PAD, :])
            k_head(i * group + n, kn[:, n * HEAD_PAD:(n + 1) * HEAD_PAD])

    _gla_block(zqk_ref, loga_ref, zv_ref, gate_ref, gout_ref, tri_ref, o_gla_ref, st_ref,
               n_chunks=x_ref.shape[1] // CHUNK)


def _split3(v):
    hi = v.astype(BF16)
    r1 = v - hi.astype(F32)
    mid = r1.astype(BF16)
    lo = (r1 - mid.astype(F32)).astype(BF16)
    return hi, mid, lo


def _gla_block(zqk_ref, loga_ref, zv_ref, gate_ref, gout_ref, tri_ref, o_ref, st_ref, *,
               n_chunks):
    @pl.when(pl.program_id(1) == 0)
    def _():
        st_ref[...] = jnp.zeros(st_ref.shape, F32)

    sr = lax.broadcasted_iota(jnp.int32, (GLA_HEADS * CHUNK, CHUNK), 0)
    sc_ = lax.broadcasted_iota(jnp.int32, (GLA_HEADS * CHUNK, CHUNK), 1)
    causal4 = (sr % CHUNK) >= sc_
    qr = lax.broadcasted_iota(jnp.int32, (GLA_HEADS * CHUNK, GLA_QK_W), 0)
    qc = lax.broadcasted_iota(jnp.int32, (GLA_HEADS * CHUNK, GLA_QK_W), 1)
    head_sel = (qr // CHUNK) == (qc // GLA_DK)
    lane_head = lax.broadcasted_iota(jnp.int32, (GLA_DV, GLA_QK_W), 1) // GLA_DK
    gout = gout_ref[...]
    chunks = [slice(ci * CHUNK, (ci + 1) * CHUNK) for ci in range(n_chunks)]

    la = loga_ref[0]
    parts = [p for r in chunks for p in _split3(la[r])]
    tri_out = jnp.dot(tri_ref[...], jnp.concatenate(parts, axis=1), preferred_element_type=F32)
    w = GLA_QK_W
    cum = [tri_out[:, (3 * ci) * w:(3 * ci + 1) * w] + tri_out[:, (3 * ci + 1) * w:(3 * ci + 2) * w]
           + tri_out[:, (3 * ci + 2) * w:(3 * ci + 3) * w] for ci in range(n_chunks)]
    cl = [c[CHUNK - 1:CHUNK, :] for c in cum]

    def prep(ci):
        r = chunks[ci]
        q = zqk_ref[0, r, 0:GLA_QK_W]
        k = zqk_ref[0, r, GLA_QK_W:2 * GLA_QK_W]
        qd = (q * (GLA_DK ** -0.5) * jnp.exp(cum[ci])).astype(BF16)
        k_inv = (k * jnp.exp(-cum[ci])).astype(BF16)
        k_end = (k * jnp.exp(cl[ci] - cum[ci])).astype(BF16)
        q_stack = jnp.where(head_sel, jnp.concatenate([qd] * GLA_HEADS, axis=0), 0.0)
        return q_stack, k_inv, k_end

    def intra_scores(ci):
        return jnp.where(causal4, _nt(pre[ci][0], pre[ci][1]), 0.0).astype(BF16)

    def intra_out_and_kv(ci):
        v = zv_ref[0, chunks[ci], :]
        full = [jnp.dot(scores[ci][hh * CHUNK:(hh + 1) * CHUNK, :],
                        v[:, hh * GLA_DV:(hh + 1) * GLA_DV], preferred_element_type=F32)
                for hh in range(GLA_HEADS)]
        kv_t = _tn(v, pre[ci][2])
        kv = kv_t[(GLA_HEADS - 1) * GLA_DV:, :]
        for hh in range(GLA_HEADS - 2, -1, -1):
            kv = jnp.where(lane_head == hh, kv_t[hh * GLA_DV:(hh + 1) * GLA_DV, :], kv)
        return full, kv

    pre, scores, intra = {}, {}, {}
    for t in range(n_chunks + 2):
        if t < n_chunks:
            pre[t] = prep(t)
        if 0 <= t - 1 < n_chunks:
            scores[t - 1] = intra_scores(t - 1)
        if 0 <= t - 2 < n_chunks:
            intra[t - 2] = intra_out_and_kv(t - 2)

    st = st_ref[...]
    st_in = []
    for ci in range(n_chunks):
        st_in.append(st.astype(BF16))
        st = st * jnp.exp(cl[ci]) + intra[ci][1]
    st_ref[...] = st

    def finish(ci, inter):
        r = chunks[ci]
        for hh in range(GLA_HEADS):
            cols = slice(hh * GLA_DV, (hh + 1) * GLA_DV)
            o_h = inter[hh * CHUNK:(hh + 1) * CHUNK, :] + intra[ci][0][hh]
            g_h = gate_ref[0, r, cols].astype(F32)
            o_ref[0, r, cols] = (_rms(o_h) * gout * g_h).astype(BF16)

    inter = {}
    for t in range(n_chunks + 1):
        if t < n_chunks:
            inter[t] = _nt(pre[t][0], st_in[t])
        if t >= 1:
            finish(t - 1, inter[t - 1])


def _attn_kernel(qt_ref, k_ref, vt_ref, w_up_ref, w_dn_ref, o_ref, w_up_bf_ref, w_dn_bf_ref,
                 s_ref, acc_ref, m_ref, bm_ref, *, tq, tk):
    qi = pl.program_id(2)
    ndiag = tq // tk

    m_ref[...] = jnp.full(m_ref.shape, NEG, F32)
    acc_ref[...] = jnp.zeros(acc_ref.shape, F32)

    kr = lax.broadcasted_iota(jnp.int32, (tk, tk), 0)
    qc = lax.broadcasted_iota(jnp.int32, (tk, tk), 1)
    square_mask = (kr // CHUNK) <= (qc // CHUNK)

    def qk(j, slot, c0=0, c1=tq):
        start = pl.multiple_of(j * tk, tk)
        s = jnp.dot(k_ref[0, pl.ds(start, tk), :], qt_ref[0, :, c0:c1],
                    preferred_element_type=F32)
        s_ref[slot, :, c0:c1] = s
        bm_ref[slot, :, c0:c1] = jnp.max(s, axis=0, keepdims=True)

    def mask_square(slot, c0):
        cols = slice(c0, c0 + tk)
        s = jnp.where(square_mask, s_ref[slot, :, cols], NEG)
        s_ref[slot, :, cols] = s
        bm_ref[slot, :, cols] = jnp.max(s, axis=0, keepdims=True)

    def softmax_pv(j, slots, c0=0, c1=tq):
        m_old = m_ref[:, c0:c1]
        m_new = m_old
        for slot in slots:
            m_new = jnp.maximum(m_new, bm_ref[slot, :, c0:c1])
        m_ref[:, c0:c1] = m_new
        p = jnp.concatenate([jnp.exp2(s_ref[slot, :, c0:c1] - m_new).astype(BF16)
                             for slot in slots], axis=0)
        start = pl.multiple_of(j * tk, tk)
        pv = jnp.dot(vt_ref[0, :, pl.ds(start, len(slots) * tk)], p,
                     preferred_element_type=F32)
        acc_ref[:, c0:c1] = jnp.exp2(m_old - m_new) * acc_ref[:, c0:c1] + pv

    ngroups = tq // QUERY_GROUP
    per_iter = 2 * ngroups

    def cols(g):
        return g * QUERY_GROUP, (g + 1) * QUERY_GROUP

    def qk_item(j, ps, g, both=True):
        qk(j, 2 * ps, *cols(g))
        if both:
            qk(j + 1, 2 * ps + 1, *cols(g))

    for u in range(QK_LOOKAHEAD):
        qk_item(0, 0, u)

    def body(i, carry):
        for u in range(per_iter):
            v = u + QK_LOOKAHEAD
            if v < per_iter:
                qk_item(4 * i + 2 * (v // ngroups), v // ngroups, v % ngroups)
            else:
                qk_item(4 * i + 4, 0, v - per_iter)
            ps, g = divmod(u, ngroups)
            softmax_pv(4 * i + 2 * ps, (2 * ps, 2 * ps + 1), *cols(g))
        return carry

    lax.fori_loop(0, qi * (ndiag // 4), body, 0)

    nfull = qi * ndiag
    items = [(p, g) for p in range(ndiag // 2) for g in range(2 * p, ngroups)]
    for t, (p, g) in enumerate(items):
        if t == len(items) // 2:
            w_up_bf_ref[...] = w_up_ref[...].astype(BF16)
            w_dn_bf_ref[...] = w_dn_ref[...].astype(BF16)
        if t + QK_LOOKAHEAD < len(items):
            p2, g2 = items[t + QK_LOOKAHEAD]
            qk_item(nfull + 2 * p2, p2 % 2, g2, both=(g2 != 2 * p2))
        d = 2 * p
        slots = (2 * (p % 2), 2 * (p % 2) + 1)
        if g == d:
            mask_square(slots[0], d * tk)
            softmax_pv(nfull + d, slots[:1], *cols(g))
        else:
            if g == d + 1:
                mask_square(slots[1], (d + 1) * tk)
            softmax_pv(nfull + d, slots, *cols(g))

    acc = acc_ref[...]
    o_ref[0] = (acc[:MLA_V, :] / acc[MLA_V:MLA_V + 1, :]).astype(BF16)


def _mlp_kernel(x_ref, og_ref, ot_ref, w_out_ref, g_mlp_ref, w_up_ref, w_dn_ref,
                y_ref, *, ff_block):
    mix = jnp.dot(og_ref[0], w_out_ref[:GLA_V_W, :], preferred_element_type=F32)
    mix = mix + _tn(ot_ref[0], w_out_ref[GLA_V_W:, :])
    x1 = x_ref[0] + mix
    h = (_rms(x1) * g_mlp_ref[...]).astype(BF16)
    acc = x1
    d_ff = w_up_ref.shape[1]
    for f in range(d_ff // ff_block):
        cols = slice(f * ff_block, (f + 1) * ff_block)
        u = jnp.dot(h, w_up_ref[:, cols], preferred_element_type=F32)
        a = jnp.square(jnp.maximum(u, 0.0)).astype(BF16)
        acc = acc + jnp.dot(a, w_dn_ref[cols, :], preferred_element_type=F32)
    y_ref[0] = acc


def _const_spec(shape, single_buffer=False):
    nd = len(shape)
    mode = pl.Buffered(1) if single_buffer else None
    return pl.BlockSpec(shape, lambda *_: (0,) * nd, pipeline_mode=mode)


def _pack_weights(attn_norm, w_in, w_gate_up, b_gate, q_a_norm, w_uq, kv_a_norm, w_ukv,
                  q_head_norm, k_head_norm):
    o_gate = 2 * GLA_QK_W + GLA_V_W
    assert A_W == o_gate and w_in.shape[1] - o_gate - GLA_GATE_RANK == B_MISC + MLA_ROPE
    assert MISC_PE == 0 and MISC_GATE == MLA_ROPE

    w_gate_f = jnp.pad(w_gate_up.astype(F32),
                       ((MISC_GATE, LANES - MISC_GATE - GLA_GATE_RANK), (0, 0)))
    g_hi = w_gate_f.astype(BF16)
    g_lo = (w_gate_f - g_hi.astype(F32)).astype(BF16)
    w_gate_p = jnp.concatenate([g_hi, g_lo, g_hi], axis=0)

    w_uq_h = w_uq.reshape(MLA_Q_RANK, MLA_HEADS, MLA_QK)
    w_uq_h = jnp.concatenate([w_uq_h[:, :, MLA_NOPE:], w_uq_h[:, :, :MLA_NOPE]], axis=2)
    w_uq_h = jnp.pad(w_uq_h, ((0, 0), (0, 0), (0, HEAD_PAD - MLA_QK)))
    w_uqt = w_uq_h.reshape(MLA_Q_RANK, MLA_HEADS * HEAD_PAD).T.astype(BF16)

    nope_pad = (H_NOPE, HEAD_PAD - H_NOPE - MLA_NOPE)
    w_ukv_h = w_ukv.reshape(MLA_KV_RANK, MLA_HEADS, MLA_NOPE + MLA_V)
    w_k = jnp.pad(w_ukv_h[:, :, :MLA_NOPE], ((0, 0), (0, 0), nope_pad))
    w_k = w_k.reshape(MLA_KV_RANK, MLA_HEADS * HEAD_PAD).astype(BF16)
    w_vt = w_ukv_h[:, :, MLA_NOPE:].reshape(MLA_KV_RANK, MLA_HEADS * MLA_V).T.astype(BF16)

    qg = jnp.concatenate([q_head_norm[MLA_NOPE:], q_head_norm[:MLA_NOPE]])
    qg_col = jnp.pad(qg, (0, HEAD_PAD - MLA_QK)).reshape(HEAD_PAD, 1).astype(F32)
    zeros = lambda n: jnp.zeros((n,), F32)
    vec = jnp.concatenate([
        attn_norm, q_a_norm, kv_a_norm, b_gate,
        zeros(nope_pad[0]), k_head_norm[:MLA_NOPE], zeros(nope_pad[1]),
        zeros(MISC_PE), k_head_norm[MLA_NOPE:], zeros(LANES - MISC_PE - MLA_ROPE),
    ]).astype(F32).reshape(1, -1)
    return vec, w_gate_p, w_uqt, w_k, w_vt, qg_col


def _layer(x, posr, invf_col, rope_sel, attn_norm, w_in, w_gate_up, b_gate, gla_out_norm,
           q_a_norm, w_uq, kv_a_norm, w_ukv, q_head_norm, k_head_norm, w_out, mlp_norm, w_up,
           w_down):
    b_, s_, d_model = x.shape
    tm = min(TOKEN_BLOCK, s_)
    tq = min(ATTN_TQ, s_)
    tk = min(ATTN_TK, tq)
    assert s_ % tm == 0 and s_ % tq == 0 and tk % CHUNK == 0 and tm % CHUNK == 0
    assert tq % (4 * tk) == 0
    assert tk == QUERY_GROUP and QK_LOOKAHEAD <= tq // QUERY_GROUP
    d_ff = w_up.shape[1]
    ff_block = min(FF_BLOCK, d_ff)

    vec, w_gate_p, w_uqt, w_k, w_vt, qg_col = _pack_weights(
        attn_norm, w_in, w_gate_up, b_gate, q_a_norm, w_uq, kv_a_norm, w_ukv, q_head_norm,
        k_head_norm)
    row = lambda v: v.reshape(1, -1).astype(F32)

    tok = lambda w: pl.BlockSpec((1, tm, w), lambda b, i: (b, i, 0))
    tok_t = lambda hgt: pl.BlockSpec((1, hgt, tm), lambda b, i: (b, 0, i))
    consts = (w_gate_p, w_uqt, w_k, w_vt, qg_col, invf_col, rope_sel)
    w_a_spec = _const_spec(w_in.shape, True)
    tri = jnp.asarray(np.tril(np.ones((CHUNK, CHUNK), np.float32)), BF16)
    o_gla, qt, kk, vt = pl.pallas_call(
        _proj_kernel,
        grid=(b_, s_ // tm),
        in_specs=([tok(d_model), tok_t(1), _const_spec(vec.shape), w_a_spec]
                  + [_const_spec(c.shape) for c in consts]
                  + [_const_spec((1, GLA_DV)), _const_spec((CHUNK, CHUNK))]),
        out_specs=[tok(GLA_V_W), tok_t(MLA_HEADS * HEAD_PAD), tok(MLA_HEADS * HEAD_PAD),
                   tok_t(MLA_HEADS * V_AUG)],
        out_shape=[jax.ShapeDtypeStruct((b_, s_, GLA_V_W), BF16),
                   jax.ShapeDtypeStruct((b_, MLA_HEADS * HEAD_PAD, s_), BF16),
                   jax.ShapeDtypeStruct((b_, s_, MLA_HEADS * HEAD_PAD), BF16),
                   jax.ShapeDtypeStruct((b_, MLA_HEADS * V_AUG, s_), BF16)],
        scratch_shapes=[pltpu.VMEM((1, tm, 2 * GLA_QK_W), F32), pltpu.VMEM((1, tm, GLA_QK_W), F32),
                        pltpu.VMEM((1, tm, GLA_V_W), BF16), pltpu.VMEM((1, tm, GLA_V_W), BF16),
                        pltpu.VMEM((GLA_DV, GLA_QK_W), F32), pltpu.VMEM((d_model, B_W), BF16)],
        compiler_params=pltpu.CompilerParams(dimension_semantics=("arbitrary", "arbitrary"),
                                             vmem_limit_bytes=VMEM_LIMIT),
        name="proj_gla",
    )(x, posr, vec, w_in, *consts, row(gla_out_norm), tri)

    n_q = s_ // tq
    n_steps = b_ * MLA_HEADS * n_q
    assert d_model % (n_steps * BF16_SUBLANES) == 0 and d_ff % (n_steps * BF16_SUBLANES) == 0
    slab = lambda rows, w: pl.BlockSpec((rows // n_steps, w),
                                        lambda b, h, i: ((b * MLA_HEADS + h) * n_q + i, 0))
    o_t, w_up_bf, w_dn_bf = pl.pallas_call(
        functools.partial(_attn_kernel, tq=tq, tk=tk),
        grid=(b_, MLA_HEADS, n_q),
        in_specs=[pl.BlockSpec((1, HEAD_PAD, tq), lambda b, h, i: (b, h, i)),
                  pl.BlockSpec((1, s_, HEAD_PAD), lambda b, h, i: (b, 0, h)),
                  pl.BlockSpec((1, V_AUG, s_), lambda b, h, i: (b, h, 0)),
                  slab(d_model, d_ff), slab(d_ff, d_model)],
        out_specs=[pl.BlockSpec((1, MLA_V, tq), lambda b, h, i: (b, h, i)),
                   slab(d_model, d_ff), slab(d_ff, d_model)],
        out_shape=[jax.ShapeDtypeStruct((b_, MLA_HEADS * MLA_V, s_), BF16),
                   jax.ShapeDtypeStruct((d_model, d_ff), BF16),
                   jax.ShapeDtypeStruct((d_ff, d_model), BF16)],
        scratch_shapes=[pltpu.VMEM((4, tk, tq), F32), pltpu.VMEM((V_AUG, tq), F32),
                        pltpu.VMEM((1, tq), F32), pltpu.VMEM((4, 1, tq), F32)],
        compiler_params=pltpu.CompilerParams(
            dimension_semantics=("parallel", "parallel", "arbitrary"),
            vmem_limit_bytes=VMEM_LIMIT),
        name="attn",
    )(qt, kk, vt, w_up, w_down)

    tmm = min(MLP_BLOCK, s_)
    assert s_ % tmm == 0
    mtok = lambda w: pl.BlockSpec((1, tmm, w), lambda b, i: (b, i, 0))
    y = pl.pallas_call(
        functools.partial(_mlp_kernel, ff_block=ff_block),
        grid=(b_, s_ // tmm),
        in_specs=[mtok(d_model), mtok(GLA_V_W),
                  pl.BlockSpec((1, MLA_HEADS * MLA_V, tmm), lambda b, i: (b, 0, i)),
                  _const_spec(w_out.shape, True), _const_spec((1, d_model)),
                  _const_spec(w_up.shape, True), _const_spec(w_down.shape, True)],
        out_specs=mtok(d_model),
        out_shape=jax.ShapeDtypeStruct((b_, s_, d_model), x.dtype),
        compiler_params=pltpu.CompilerParams(dimension_semantics=("parallel", "parallel"),
                                             vmem_limit_bytes=VMEM_LIMIT),
        name="mlp",
    )(x, o_gla, o_t, w_out.astype(BF16), row(mlp_norm), w_up_bf, w_dn_bf)
    return y


def kernel(x, positions, attn_norm, w_in, w_gate_up, b_gate, gla_out_norm, q_a_norm, w_uq,
           kv_a_norm, w_ukv, q_head_norm, k_head_norm, w_out, mlp_norm, w_up, w_down):
    b_, s_, _ = x.shape
    posr = positions.reshape(b_, 1, s_)
    inv_freq = ROPE_BASE ** (-jnp.arange(0, MLA_ROPE, 2, dtype=F32) / MLA_ROPE)
    invf_col = inv_freq.reshape(MLA_HALF, 1)
    i = np.arange(MLA_HALF)
    sel = np.zeros((MLA_ROPE, LANES), np.float32)
    sel[i, MISC_PE + i] = 1.0
    sel[i, MISC_PE + MLA_HALF + i] = 1.0
    sel[MLA_HALF + i, ROT_NEG + MISC_PE + i] = -1.0
    sel[MLA_HALF + i, ROT_POS + MISC_PE + MLA_HALF + i] = 1.0
    rope_sel = jnp.asarray(np.tile(sel, (3, 1)), BF16)
    for l in range(attn_norm.shape[0]):
        x = _layer(x, posr, invf_col, rope_sel, attn_norm[l], w_in[l], w_gate_up[l],
                   b_gate[l], gla_out_norm[l], q_a_norm[l], w_uq[l], kv_a_norm[l], w_ukv[l],
                   q_head_norm[l], k_head_norm[l], w_out[l], mlp_norm[l], w_up[l], w_down[l])
    return x
```

```python
import functools
import math

import jax
import jax.numpy as jnp
import numpy as np
from jax import lax
from jax.experimental import pallas as pl
from jax.experimental.pallas import tpu as pltpu

F32 = jnp.float32
BF16 = jnp.bfloat16

CHUNK = 64
EPS = 1e-6

GLA_HEADS = 4
GLA_DK = 64
GLA_DV = 128
GLA_GATE_RANK = 16
GLA_GATE_NORMALIZER = 16.0
GLA_LOG_GATE_MIN = -1.0
GLA_QK_W = GLA_HEADS * GLA_DK
GLA_V_W = GLA_HEADS * GLA_DV

MLA_HEADS = 8
MLA_Q_RANK = 256
MLA_KV_RANK = 128
MLA_NOPE = 64
MLA_ROPE = 32
MLA_HALF = MLA_ROPE // 2
MLA_QK = MLA_NOPE + MLA_ROPE
MLA_V = 64
ROPE_BASE = 10000.0

LANES = 128
HEAD_PAD = LANES
BF16_SUBLANES = 16
V_AUG = MLA_V + BF16_SUBLANES
LOG2_E = math.log2(math.e)
NEG = -1e30

H_NOPE = MLA_ROPE
A_Q = 0
A_K = A_Q + GLA_QK_W
A_V = A_K + GLA_QK_W
A_W = A_V + GLA_V_W
B_G = 0
B_CQ = B_G + GLA_V_W
B_CKV = B_CQ + MLA_Q_RANK
B_MISC = B_CKV + MLA_KV_RANK
B_W = B_MISC + LANES
MISC_PE = 0
MISC_GATE = MLA_ROPE
ROT_NEG = MLA_ROPE
ROT_POS = 2 * MLA_ROPE

PROJ_PIECE = 256
TOKEN_BLOCK = 1024
MLP_BLOCK = 1024
ATTN_TQ = 4096
ATTN_TK = 256
QUERY_GROUP = 256
QK_LOOKAHEAD = 3
FF_BLOCK = 1024
VMEM_LIMIT = 56 * 1024 * 1024


def _nt(a, b):
    return lax.dot_general(a, b, (((1,), (1,)), ((), ())), preferred_element_type=F32)


def _tn(a, b):
    return lax.dot_general(a, b, (((0,), (0,)), ((), ())), preferred_element_type=F32)


def _rms(v):
    return v * lax.rsqrt(jnp.mean(v * v, axis=-1, keepdims=True) + EPS)


def _proj_kernel(x_ref, posr_ref, vec_ref, w_a_ref, w_b_ref, w_gate_ref, w_uqt_ref, w_k_ref,
                 w_vt_ref, qg_col_ref, invf_col_ref, rope_sel_ref, gout_ref, tri_ref,
                 o_gla_ref, qt_ref, k_ref, vt_ref,
                 zqk_ref, loga_ref, zv_ref, gate_ref, st_ref):
    d_model = x_ref.shape[2]
    offs = np.cumsum([0, d_model, MLA_Q_RANK, MLA_KV_RANK, GLA_QK_W, LANES, LANES])
    g_attn, qa_g, kva_g, b_gate, kg_nope, kg_rope = (
        vec_ref[:, int(a):int(b)] for a, b in zip(offs[:-1], offs[1:]))
    hb = (_rms(x_ref[0]) * g_attn).astype(BF16)

    def in_proj(w_ref, c0, c1):
        return jnp.dot(hb, w_ref[:, c0:c1].astype(BF16), preferred_element_type=F32)

    zg0 = in_proj(w_b_ref, B_G, B_G + PROJ_PIECE)
    zg1 = in_proj(w_b_ref, B_G + PROJ_PIECE, B_CQ)
    z_mla = in_proj(w_b_ref, B_CQ, B_W)
    gate_ref[0, :, :PROJ_PIECE] = (zg0 * jax.nn.sigmoid(zg0)).astype(BF16)
    gate_ref[0, :, PROJ_PIECE:] = (zg1 * jax.nn.sigmoid(zg1)).astype(BF16)
    zcq = z_mla[:, :B_CKV - B_CQ]
    zckv = z_mla[:, B_CKV - B_CQ:B_MISC - B_CQ]
    misc = z_mla[:, B_MISC - B_CQ:]

    cq = (_rms(zcq) * qa_g).astype(BF16)
    ckv = (_rms(zckv) * kva_g).astype(BF16)
    vt = _nt(w_vt_ref[...], ckv).astype(BF16)
    ang_t = invf_col_ref[...] * posr_ref[0].astype(F32)
    cos_t = jnp.cos(ang_t)
    sin_t = jnp.sin(ang_t)
    cs3 = jnp.concatenate(_split3(jnp.concatenate([cos_t, sin_t], axis=0)), axis=0)
    tab = _tn(cs3, rope_sel_ref[...])

    m_hi = misc.astype(BF16)
    m_lo = (misc - m_hi.astype(F32)).astype(BF16)
    logit = jnp.dot(jnp.concatenate([m_hi, m_hi, m_lo], axis=1), w_gate_ref[...],
                    preferred_element_type=F32)
    logit = logit + b_gate
    log_sig = jnp.minimum(logit, 0.0) - jnp.log1p(jnp.exp(-jnp.abs(logit)))
    loga_ref[0] = jnp.maximum(log_sig / GLA_GATE_NORMALIZER, GLA_LOG_GATE_MIN)

    ones_rows = jnp.ones((V_AUG - MLA_V, vt.shape[1]), BF16)
    for hh in range(MLA_HEADS):
        vt_ref[0, hh * V_AUG:hh * V_AUG + MLA_V, :] = vt[hh * MLA_V:(hh + 1) * MLA_V, :]
        vt_ref[0, hh * V_AUG + MLA_V:(hh + 1) * V_AUG, :] = ones_rows

    qscale = MLA_QK ** -0.5 * LOG2_E
    r1 = MLA_HALF
    r2 = H_NOPE
    r3 = H_NOPE + MLA_NOPE

    def q_head(hh, blk):
        base = hh * HEAD_PAD
        ssq = jnp.sum(blk * blk, axis=0, keepdims=True)
        rs = lax.rsqrt(ssq * (1.0 / MLA_QK) + EPS) * qscale
        qn = blk * rs * qg_col_ref[...]
        x1 = qn[:r1, :]
        x2 = qn[r1:r2, :]
        qt_ref[0, base:base + r1, :] = (x1 * cos_t - x2 * sin_t).astype(BF16)
        qt_ref[0, base + r1:base + r2, :] = (x1 * sin_t + x2 * cos_t).astype(BF16)
        qt_ref[0, base + r2:base + r3, :] = qn[r2:r3, :].astype(BF16)
        qt_ref[0, base + r3:base + HEAD_PAD, :] = jnp.zeros((HEAD_PAD - r3, blk.shape[1]), BF16)

    lane = lax.broadcasted_iota(jnp.int32, (1, LANES), 1)
    is_pe = (lane >= MISC_PE) & (lane < MISC_PE + MLA_ROPE)
    kpe = jnp.where(is_pe, misc, 0.0)
    ssq_pe = jnp.sum(kpe * kpe, axis=-1, keepdims=True)
    xg = kpe * kg_rope
    rot = (xg * tab
           + pltpu.roll(xg, LANES - MLA_HALF, 1) * pltpu.roll(tab, LANES - ROT_NEG, 1)
           + pltpu.roll(xg, MLA_HALF, 1) * pltpu.roll(tab, LANES - ROT_POS, 1))

    def k_head(hh, kn_h):
        base = hh * HEAD_PAD
        ssq = jnp.sum(kn_h * kn_h, axis=-1, keepdims=True) + ssq_pe
        rs = lax.rsqrt(ssq * (1.0 / MLA_QK) + EPS)
        k_ref[0, :, base:base + HEAD_PAD] = ((kn_h * kg_nope + rot) * rs).astype(BF16)

    pieces = [(zqk_ref, 0, A_Q), (zqk_ref, PROJ_PIECE, A_K),
              (zv_ref, 0, A_V), (zv_ref, PROJ_PIECE, A_V + PROJ_PIECE)]
    group = MLA_HEADS // len(pieces)
    for i, (ref, lane0, c0) in enumerate(pieces):
        rows = slice(i * group * HEAD_PAD, (i + 1) * group * HEAD_PAD)
        ref[0, :, lane0:lane0 + PROJ_PIECE] = in_proj(w_a_ref, c0, c0 + PROJ_PIECE).astype(ref.dtype)
        qt = _nt(w_uqt_ref[rows, :], cq)
        kn = jnp.dot(ckv, w_k_ref[:, rows], preferred_element_type=F32)
        for n in range(group):
            q_head(i * group + n, qt[n * HEAD_PAD:(n + 1) * HEAD_PAD, :])
            k_head(i * group + n, kn[:, n * HEAD_PAD:(n + 1) * HEAD_PAD])

    _gla_block(zqk_ref, loga_ref, zv_ref, gate_ref, gout_ref, tri_ref, o_gla_ref, st_ref,
               n_chunks=x_ref.shape[1] // CHUNK)


def _split3(v):
    hi = v.astype(BF16)
    r1 = v - hi.astype(F32)
    mid = r1.astype(BF16)
    lo = (r1 - mid.astype(F32)).astype(BF16)
    return hi, mid, lo


def _gla_block(zqk_ref, loga_ref, zv_ref, gate_ref, gout_ref, tri_ref, o_ref, st_ref, *,
               n_chunks):
    @pl.when(pl.program_id(1) == 0)
    def _():
        st_ref[...] = jnp.zeros(st_ref.shape, F32)

    sr = lax.broadcasted_iota(jnp.int32, (GLA_HEADS * CHUNK, CHUNK), 0)
    sc_ = lax.broadcasted_iota(jnp.int32, (GLA_HEADS * CHUNK, CHUNK), 1)
    causal4 = (sr % CHUNK) >= sc_
    qr = lax.broadcasted_iota(jnp.int32, (GLA_HEADS * CHUNK, GLA_QK_W), 0)
    qc = lax.broadcasted_iota(jnp.int32, (GLA_HEADS * CHUNK, GLA_QK_W), 1)
    head_sel = (qr // CHUNK) == (qc // GLA_DK)
    lane_head = lax.broadcasted_iota(jnp.int32, (GLA_DV, GLA_QK_W), 1) // GLA_DK
    gout = gout_ref[...]
    chunks = [slice(ci * CHUNK, (ci + 1) * CHUNK) for ci in range(n_chunks)]

    la = loga_ref[0]
    parts = [p for r in chunks for p in _split3(la[r])]
    tri_out = jnp.dot(tri_ref[...], jnp.concatenate(parts, axis=1), preferred_element_type=F32)
    w = GLA_QK_W
    cum = [tri_out[:, (3 * ci) * w:(3 * ci + 1) * w] + tri_out[:, (3 * ci + 1) * w:(3 * ci + 2) * w]
           + tri_out[:, (3 * ci + 2) * w:(3 * ci + 3) * w] for ci in range(n_chunks)]
    cl = [c[CHUNK - 1:CHUNK, :] for c in cum]

    def prep(ci):
        r = chunks[ci]
        q = zqk_ref[0, r, 0:GLA_QK_W]
        k = zqk_ref[0, r, GLA_QK_W:2 * GLA_QK_W]
        qd = (q * (GLA_DK ** -0.5) * jnp.exp(cum[ci])).astype(BF16)
        k_inv = (k * jnp.exp(-cum[ci])).astype(BF16)
        k_end = (k * jnp.exp(cl[ci] - cum[ci])).astype(BF16)
        q_stack = jnp.where(head_sel, jnp.concatenate([qd] * GLA_HEADS, axis=0), 0.0)
        return q_stack, k_inv, k_end

    def intra_scores(ci):
        return jnp.where(causal4, _nt(pre[ci][0], pre[ci][1]), 0.0).astype(BF16)

    def intra_out_and_kv(ci):
        v = zv_ref[0, chunks[ci], :]
        full = [jnp.dot(scores[ci][hh * CHUNK:(hh + 1) * CHUNK, :],
                        v[:, hh * GLA_DV:(hh + 1) * GLA_DV], preferred_element_type=F32)
                for hh in range(GLA_HEADS)]
        kv_t = _tn(v, pre[ci][2])
        kv = kv_t[(GLA_HEADS - 1) * GLA_DV:, :]
        for hh in range(GLA_HEADS - 2, -1, -1):
            kv = jnp.where(lane_head == hh, kv_t[hh * GLA_DV:(hh + 1) * GLA_DV, :], kv)
        return full, kv

    pre, scores, intra = {}, {}, {}
    for t in range(n_chunks + 2):
        if t < n_chunks:
            pre[t] = prep(t)
        if 0 <= t - 1 < n_chunks:
            scores[t - 1] = intra_scores(t - 1)
        if 0 <= t - 2 < n_chunks:
            intra[t - 2] = intra_out_and_kv(t - 2)

    st = st_ref[...]
    st_in = []
    for ci in range(n_chunks):
        st_in.append(st.astype(BF16))
        st = st * jnp.exp(cl[ci]) + intra[ci][1]
    st_ref[...] = st

    def finish(ci, inter):
        r = chunks[ci]
        for hh in range(GLA_HEADS):
            cols = slice(hh * GLA_DV, (hh + 1) * GLA_DV)
            o_h = inter[hh * CHUNK:(hh + 1) * CHUNK, :] + intra[ci][0][hh]
            g_h = gate_ref[0, r, cols].astype(F32)
            o_ref[0, r, cols] = (_rms(o_h) * gout * g_h).astype(BF16)

    inter = {}
    for t in range(n_chunks + 1):
        if t < n_chunks:
            inter[t] = _nt(pre[t][0], st_in[t])
        if t >= 1:
            finish(t - 1, inter[t - 1])


def _attn_kernel(qt_ref, k_ref, vt_ref, w_up_ref, w_dn_ref, o_ref, w_up_bf_ref, w_dn_bf_ref,
                 s_ref, acc_ref, m_ref, bm_ref, *, tq, tk):
    qi = pl.program_id(2)
    ndiag = tq // tk

    m_ref[...] = jnp.full(m_ref.shape, NEG, F32)
    acc_ref[...] = jnp.zeros(acc_ref.shape, F32)

    kr = lax.broadcasted_iota(jnp.int32, (tk, tk), 0)
    qc = lax.broadcasted_iota(jnp.int32, (tk, tk), 1)
    square_mask = (kr // CHUNK) <= (qc // CHUNK)

    def qk(j, slot, c0=0, c1=tq):
        start = pl.multiple_of(j * tk, tk)
        s = jnp.dot(k_ref[0, pl.ds(start, tk), :], qt_ref[0, :, c0:c1],
                    preferred_element_type=F32)
        s_ref[slot, :, c0:c1] = s
        bm_ref[slot, :, c0:c1] = jnp.max(s, axis=0, keepdims=True)

    def mask_square(slot, c0):
        cols = slice(c0, c0 + tk)
        s = jnp.where(square_mask, s_ref[slot, :, cols], NEG)
        s_ref[slot, :, cols] = s
        bm_ref[slot, :, cols] = jnp.max(s, axis=0, keepdims=True)

    def softmax_pv(j, slots, c0=0, c1=tq):
        m_old = m_ref[:, c0:c1]
        m_new = m_old
        for slot in slots:
            m_new = jnp.maximum(m_new, bm_ref[slot, :, c0:c1])
        m_ref[:, c0:c1] = m_new
        p = jnp.concatenate([jnp.exp2(s_ref[slot, :, c0:c1] - m_new).astype(BF16)
                             for slot in slots], axis=0)
        start = pl.multiple_of(j * tk, tk)
        pv = jnp.dot(vt_ref[0, :, pl.ds(start, len(slots) * tk)], p,
                     preferred_element_type=F32)
        acc_ref[:, c0:c1] = jnp.exp2(m_old - m_new) * acc_ref[:, c0:c1] + pv

    ngroups = tq // QUERY_GROUP
    per_iter = 2 * ngroups

    def cols(g):
        return g * QUERY_GROUP, (g + 1) * QUERY_GROUP

    def qk_item(j, ps, g, both=True):
        qk(j, 2 * ps, *cols(g))
        if both:
            qk(j + 1, 2 * ps + 1, *cols(g))

    for u in range(QK_LOOKAHEAD):
        qk_item(0, 0, u)

    def body(i, carry):
        for u in range(per_iter):
            v = u + QK_LOOKAHEAD
            if v < per_iter:
                qk_item(4 * i + 2 * (v // ngroups), v // ngroups, v % ngroups)
            else:
                qk_item(4 * i + 4, 0, v - per_iter)
            ps, g = divmod(u, ngroups)
            softmax_pv(4 * i + 2 * ps, (2 * ps, 2 * ps + 1), *cols(g))
        return carry

    lax.fori_loop(0, qi * (ndiag // 4), body, 0)

    nfull = qi * ndiag
    items = [(p, g) for p in range(ndiag // 2) for g in range(2 * p, ngroups)]
    for t, (p, g) in enumerate(items):
        if t == len(items) // 2:
            w_up_bf_ref[...] = w_up_ref[...].astype(BF16)
            w_dn_bf_ref[...] = w_dn_ref[...].astype(BF16)
        if t + QK_LOOKAHEAD < len(items):
            p2, g2 = items[t + QK_LOOKAHEAD]
            qk_item(nfull + 2 * p2, p2 % 2, g2, both=(g2 != 2 * p2))
        d = 2 * p
        slots = (2 * (p % 2), 2 * (p % 2) + 1)
        if g == d:
            mask_square(slots[0], d * tk)
            softmax_pv(nfull + d, slots[:1], *cols(g))
        else:
            if g == d + 1:
                mask_square(slots[1], (d + 1) * tk)
            softmax_pv(nfull + d, slots, *cols(g))

    acc = acc_ref[...]
    o_ref[0] = (acc[:MLA_V, :] / acc[MLA_V:MLA_V + 1, :]).astype(BF16)


def _mlp_kernel(x_ref, og_ref, ot_ref, w_out_ref, g_mlp_ref, w_up_ref, w_dn_ref,
                y_ref, *, ff_block):
    mix = jnp.dot(og_ref[0], w_out_ref[:GLA_V_W, :], preferred_element_type=F32)
    mix = mix + _tn(ot_ref[0], w_out_ref[GLA_V_W:, :])
    x1 = x_ref[0] + mix
    h = (_rms(x1) * g_mlp_ref[...]).astype(BF16)
    acc = x1
    d_ff = w_up_ref.shape[1]
    for f in range(d_ff // ff_block):
        cols = slice(f * ff_block, (f + 1) * ff_block)
        u = jnp.dot(h, w_up_ref[:, cols], preferred_element_type=F32)
        a = jnp.square(jnp.maximum(u, 0.0)).astype(BF16)
        acc = acc + jnp.dot(a, w_dn_ref[cols, :], preferred_element_type=F32)
    y_ref[0] = acc


def _const_spec(shape, single_buffer=False):
    nd = len(shape)
    mode = pl.Buffered(1) if single_buffer else None
    return pl.BlockSpec(shape, lambda *_: (0,) * nd, pipeline_mode=mode)


def _pack_weights(attn_norm, w_in, w_gate_up, b_gate, q_a_norm, w_uq, kv_a_norm, w_ukv,
                  q_head_norm, k_head_norm):
    o_gate = 2 * GLA_QK_W + GLA_V_W
    o_rest = o_gate + GLA_GATE_RANK
    rest = w_in[:, o_rest:]
    assert A_W == o_gate and rest.shape[1] == B_MISC + MLA_ROPE and MISC_PE == 0
    gate_at = B_MISC + MISC_GATE
    w_b = (jnp.pad(rest, ((0, 0), (0, B_W - rest.shape[1])))
           + jnp.pad(w_in[:, o_gate:o_rest], ((0, 0), (gate_at, B_W - gate_at - GLA_GATE_RANK)))
           ).astype(BF16)

    w_gate_f = jnp.pad(w_gate_up.astype(F32),
                       ((MISC_GATE, LANES - MISC_GATE - GLA_GATE_RANK), (0, 0)))
    g_hi = w_gate_f.astype(BF16)
    g_lo = (w_gate_f - g_hi.astype(F32)).astype(BF16)
    w_gate_p = jnp.concatenate([g_hi, g_lo, g_hi], axis=0)

    w_uq_h = w_uq.reshape(MLA_Q_RANK, MLA_HEADS, MLA_QK)
    w_uq_h = jnp.concatenate([w_uq_h[:, :, MLA_NOPE:], w_uq_h[:, :, :MLA_NOPE]], axis=2)
    w_uq_h = jnp.pad(w_uq_h, ((0, 0), (0, 0), (0, HEAD_PAD - MLA_QK)))
    w_uqt = w_uq_h.reshape(MLA_Q_RANK, MLA_HEADS * HEAD_PAD).T.astype(BF16)

    nope_pad = (H_NOPE, HEAD_PAD - H_NOPE - MLA_NOPE)
    w_ukv_h = w_ukv.reshape(MLA_KV_RANK, MLA_HEADS, MLA_NOPE + MLA_V)
    w_k = jnp.pad(w_ukv_h[:, :, :MLA_NOPE], ((0, 0), (0, 0), nope_pad))
    w_k = w_k.reshape(MLA_KV_RANK, MLA_HEADS * HEAD_PAD).astype(BF16)
    w_vt = w_ukv_h[:, :, MLA_NOPE:].reshape(MLA_KV_RANK, MLA_HEADS * MLA_V).T.astype(BF16)

    qg = jnp.concatenate([q_head_norm[MLA_NOPE:], q_head_norm[:MLA_NOPE]])
    qg_col = jnp.pad(qg, (0, HEAD_PAD - MLA_QK)).reshape(HEAD_PAD, 1).astype(F32)
    zeros = lambda n: jnp.zeros((n,), F32)
    vec = jnp.concatenate([
        attn_norm, q_a_norm, kv_a_norm, b_gate,
        zeros(nope_pad[0]), k_head_norm[:MLA_NOPE], zeros(nope_pad[1]),
        zeros(MISC_PE), k_head_norm[MLA_NOPE:], zeros(LANES - MISC_PE - MLA_ROPE),
    ]).astype(F32).reshape(1, -1)
    return vec, w_b, w_gate_p, w_uqt, w_k, w_vt, qg_col


def _layer(x, posr, invf_col, rope_sel, attn_norm, w_in, w_gate_up, b_gate, gla_out_norm,
           q_a_norm, w_uq, kv_a_norm, w_ukv, q_head_norm, k_head_norm, w_out, mlp_norm, w_up,
           w_down):
    b_, s_, d_model = x.shape
    tm = min(TOKEN_BLOCK, s_)
    tq = min(ATTN_TQ, s_)
    tk = min(ATTN_TK, tq)
    assert s_ % tm == 0 and s_ % tq == 0 and tk % CHUNK == 0 and tm % CHUNK == 0
    assert tq % (4 * tk) == 0
    assert tk == QUERY_GROUP and QK_LOOKAHEAD <= tq // QUERY_GROUP
    d_ff = w_up.shape[1]
    ff_block = min(FF_BLOCK, d_ff)

    vec, w_b, w_gate_p, w_uqt, w_k, w_vt, qg_col = _pack_weights(
        attn_norm, w_in, w_gate_up, b_gate, q_a_norm, w_uq, kv_a_norm, w_ukv, q_head_norm,
        k_head_norm)
    row = lambda v: v.reshape(1, -1).astype(F32)

    tok = lambda w: pl.BlockSpec((1, tm, w), lambda b, i: (b, i, 0))
    tok_t = lambda hgt: pl.BlockSpec((1, hgt, tm), lambda b, i: (b, 0, i))
    consts = (w_b, w_gate_p, w_uqt, w_k, w_vt, qg_col, invf_col, rope_sel)
    w_a_spec = pl.BlockSpec((d_model, A_W), lambda b, i: (0, 0))
    tri = jnp.asarray(np.tril(np.ones((CHUNK, CHUNK), np.float32)), BF16)
    o_gla, qt, kk, vt = pl.pallas_call(
        _proj_kernel,
        grid=(b_, s_ // tm),
        in_specs=([tok(d_model), tok_t(1), _const_spec(vec.shape), w_a_spec]
                  + [_const_spec(c.shape) for c in consts]
                  + [_const_spec((1, GLA_DV)), _const_spec((CHUNK, CHUNK))]),
        out_specs=[tok(GLA_V_W), tok_t(MLA_HEADS * HEAD_PAD), tok(MLA_HEADS * HEAD_PAD),
                   tok_t(MLA_HEADS * V_AUG)],
        out_shape=[jax.ShapeDtypeStruct((b_, s_, GLA_V_W), BF16),
                   jax.ShapeDtypeStruct((b_, MLA_HEADS * HEAD_PAD, s_), BF16),
                   jax.ShapeDtypeStruct((b_, s_, MLA_HEADS * HEAD_PAD), BF16),
                   jax.ShapeDtypeStruct((b_, MLA_HEADS * V_AUG, s_), BF16)],
        scratch_shapes=[pltpu.VMEM((1, tm, 2 * GLA_QK_W), F32), pltpu.VMEM((1, tm, GLA_QK_W), F32),
                        pltpu.VMEM((1, tm, GLA_V_W), BF16), pltpu.VMEM((1, tm, GLA_V_W), BF16),
                        pltpu.VMEM((GLA_DV, GLA_QK_W), F32)],
        compiler_params=pltpu.CompilerParams(dimension_semantics=("parallel", "arbitrary"),
                                             vmem_limit_bytes=VMEM_LIMIT),
        name="proj_gla",
    )(x, posr, vec, w_in, *consts, row(gla_out_norm), tri)

    n_q = s_ // tq
    n_steps = b_ * MLA_HEADS * n_q
    assert d_model % (n_steps * BF16_SUBLANES) == 0 and d_ff % (n_steps * BF16_SUBLANES) == 0
    slab = lambda rows, w: pl.BlockSpec((rows // n_steps, w),
                                        lambda b, h, i: ((b * MLA_HEADS + h) * n_q + i, 0))
    o_t, w_up_bf, w_dn_bf = pl.pallas_call(
        functools.partial(_attn_kernel, tq=tq, tk=tk),
        grid=(b_, MLA_HEADS, n_q),
        in_specs=[pl.BlockSpec((1, HEAD_PAD, tq), lambda b, h, i: (b, h, i)),
                  pl.BlockSpec((1, s_, HEAD_PAD), lambda b, h, i: (b, 0, h)),
                  pl.BlockSpec((1, V_AUG, s_), lambda b, h, i: (b, h, 0)),
                  slab(d_model, d_ff), slab(d_ff, d_model)],
        out_specs=[pl.BlockSpec((1, MLA_V, tq), lambda b, h, i: (b, h, i)),
                   slab(d_model, d_ff), slab(d_ff, d_model)],
        out_shape=[jax.ShapeDtypeStruct((b_, MLA_HEADS * MLA_V, s_), BF16),
                   jax.ShapeDtypeStruct((d_model, d_ff), BF16),
                   jax.ShapeDtypeStruct((d_ff, d_model), BF16)],
        scratch_shapes=[pltpu.VMEM((4, tk, tq), F32), pltpu.VMEM((V_AUG, tq), F32),
                        pltpu.VMEM((1, tq), F32), pltpu.VMEM((4, 1, tq), F32)],
        compiler_params=pltpu.CompilerParams(
            dimension_semantics=("parallel", "parallel", "arbitrary"),
            vmem_limit_bytes=VMEM_LIMIT),
        name="attn",
    )(qt, kk, vt, w_up, w_down)

    tmm = min(MLP_BLOCK, s_)
    assert s_ % tmm == 0
    mtok = lambda w: pl.BlockSpec((1, tmm, w), lambda b, i: (b, i, 0))
    y = pl.pallas_call(
        functools.partial(_mlp_kernel, ff_block=ff_block),
        grid=(b_, s_ // tmm),
        in_specs=[mtok(d_model), mtok(GLA_V_W),
                  pl.BlockSpec((1, MLA_HEADS * MLA_V, tmm), lambda b, i: (b, 0, i)),
                  _const_spec(w_out.shape, True), _const_spec((1, d_model)),
                  _const_spec(w_up.shape, True), _const_spec(w_down.shape, True)],
        out_specs=mtok(d_model),
        out_shape=jax.ShapeDtypeStruct((b_, s_, d_model), x.dtype),
        compiler_params=pltpu.CompilerParams(dimension_semantics=("parallel", "parallel"),
                                             vmem_limit_bytes=VMEM_LIMIT),
        name="mlp",
    )(x, o_gla, o_t, w_out.astype(BF16), row(mlp_norm), w_up_bf, w_dn_bf)
    return y


def kernel(x, positions, attn_norm, w_in, w_gate_up, b_gate, gla_out_norm, q_a_norm, w_uq,
           kv_a_norm, w_ukv, q_head_norm, k_head_norm, w_out, mlp_norm, w_up, w_down):
    b_, s_, _ = x.shape
    posr = positions.reshape(b_, 1, s_)
    inv_freq = ROPE_BASE ** (-jnp.arange(0, MLA_ROPE, 2, dtype=F32) / MLA_ROPE)
    invf_col = inv_freq.reshape(MLA_HALF, 1)
    i = np.arange(MLA_HALF)
    sel = np.zeros((MLA_ROPE, LANES), np.float32)
    sel[i, MISC_PE + i] = 1.0
    sel[i, MISC_PE + MLA_HALF + i] = 1.0
    sel[MLA_HALF + i, ROT_NEG + MISC_PE + i] = -1.0
    sel[MLA_HALF + i, ROT_POS + MISC_PE + MLA_HALF + i] = 1.0
    rope_sel = jnp.asarray(np.tile(sel, (3, 1)), BF16)
    for l in range(attn_norm.shape[0]):
        x = _layer(x, posr, invf_col, rope_sel, attn_norm[l], w_in[l], w_gate_up[l],
                   b_gate[l], gla_out_norm[l], q_a_norm[l], w_uq[l], kv_a_norm[l], w_ukv[l],
                   q_head_norm[l], k_head_norm[l], w_out[l], mlp_norm[l], w_up[l], w_down[l])
    return x
```

```python
import functools
import math

import jax
import jax.numpy as jnp
import numpy as np
from jax import lax
from jax.experimental import pallas as pl
from jax.experimental.pallas import tpu as pltpu

F32 = jnp.float32
BF16 = jnp.bfloat16

CHUNK = 64
EPS = 1e-6

GLA_HEADS = 4
GLA_DK = 64
GLA_DV = 128
GLA_GATE_RANK = 16
GLA_GATE_NORMALIZER = 16.0
GLA_LOG_GATE_MIN = -1.0
GLA_QK_W = GLA_HEADS * GLA_DK
GLA_V_W = GLA_HEADS * GLA_DV

MLA_HEADS = 8
MLA_Q_RANK = 256
MLA_KV_RANK = 128
MLA_NOPE = 64
MLA_ROPE = 32
MLA_HALF = MLA_ROPE // 2
MLA_QK = MLA_NOPE + MLA_ROPE
MLA_V = 64
ROPE_BASE = 10000.0

LANES = 128
HEAD_PAD = LANES
BF16_SUBLANES = 16
V_AUG = MLA_V + BF16_SUBLANES
LOG2_E = math.log2(math.e)
NEG = -1e30

H_NOPE = MLA_ROPE
A_Q = 0
A_K = A_Q + GLA_QK_W
A_V = A_K + GLA_QK_W
A_W = A_V + GLA_V_W
B_G = 0
B_CQ = B_G + GLA_V_W
B_CKV = B_CQ + MLA_Q_RANK
B_MISC = B_CKV + MLA_KV_RANK
B_W = B_MISC + LANES
MISC_PE = 0
MISC_GATE = MLA_ROPE
ROT_NEG = MLA_ROPE
ROT_POS = 2 * MLA_ROPE

PROJ_PIECE = 256
TOKEN_BLOCK = 1024
MLP_BLOCK = 1024
ATTN_TQ = 4096
ATTN_TK = 256
QUERY_GROUP = 256
QK_LOOKAHEAD = 3
FF_BLOCK = 1024
VMEM_LIMIT = 56 * 1024 * 1024


def _nt(a, b):
    return lax.dot_general(a, b, (((1,), (1,)), ((), ())), preferred_element_type=F32)


def _tn(a, b):
    return lax.dot_general(a, b, (((0,), (0,)), ((), ())), preferred_element_type=F32)


def _rms(v):
    return v * lax.rsqrt(jnp.mean(v * v, axis=-1, keepdims=True) + EPS)


def _proj_kernel(x_ref, posr_ref, vec_ref, w_a_ref, w_b_ref, w_gate_ref, w_uqt_ref, w_k_ref,
                 w_vt_ref, qg_col_ref, invf_col_ref, rope_sel_ref, gout_ref, tri_ref,
                 o_gla_ref, qt_ref, k_ref, vt_ref,
                 zqk_ref, loga_ref, zv_ref, gate_ref, st_ref):
    d_model = x_ref.shape[2]
    offs = np.cumsum([0, d_model, MLA_Q_RANK, MLA_KV_RANK, GLA_QK_W, LANES, LANES])
    g_attn, qa_g, kva_g, b_gate, kg_nope, kg_rope = (
        vec_ref[:, int(a):int(b)] for a, b in zip(offs[:-1], offs[1:]))
    hb = (_rms(x_ref[0]) * g_attn).astype(BF16)

    def in_proj(w_ref, c0, c1):
        return jnp.dot(hb, w_ref[:, c0:c1].astype(BF16), preferred_element_type=F32)

    zg0 = in_proj(w_b_ref, B_G, B_G + PROJ_PIECE)
    zg1 = in_proj(w_b_ref, B_G + PROJ_PIECE, B_CQ)
    z_mla = in_proj(w_b_ref, B_CQ, B_W)
    gate_ref[0, :, :PROJ_PIECE] = (zg0 * jax.nn.sigmoid(zg0)).astype(BF16)
    gate_ref[0, :, PROJ_PIECE:] = (zg1 * jax.nn.sigmoid(zg1)).astype(BF16)
    zcq = z_mla[:, :B_CKV - B_CQ]
    zckv = z_mla[:, B_CKV - B_CQ:B_MISC - B_CQ]
    misc = z_mla[:, B_MISC - B_CQ:]

    cq = (_rms(zcq) * qa_g).astype(BF16)
    ckv = (_rms(zckv) * kva_g).astype(BF16)
    vt = _nt(w_vt_ref[...], ckv).astype(BF16)
    ang_t = invf_col_ref[...] * posr_ref[0].astype(F32)
    cos_t = jnp.cos(ang_t)
    sin_t = jnp.sin(ang_t)
    cs3 = jnp.concatenate(_split3(jnp.concatenate([cos_t, sin_t], axis=0)), axis=0)
    tab = _tn(cs3, rope_sel_ref[...])

    m_hi = misc.astype(BF16)
    m_lo = (misc - m_hi.astype(F32)).astype(BF16)
    logit = jnp.dot(jnp.concatenate([m_hi, m_hi, m_lo], axis=1), w_gate_ref[...],
                    preferred_element_type=F32)
    logit = logit + b_gate
    log_sig = jnp.minimum(logit, 0.0) - jnp.log1p(jnp.exp(-jnp.abs(logit)))
    loga_ref[0] = jnp.maximum(log_sig / GLA_GATE_NORMALIZER, GLA_LOG_GATE_MIN)

    ones_rows = jnp.ones((V_AUG - MLA_V, vt.shape[1]), BF16)
    for hh in range(MLA_HEADS):
        vt_ref[0, hh * V_AUG:hh * V_AUG + MLA_V, :] = vt[hh * MLA_V:(hh + 1) * MLA_V, :]
        vt_ref[0, hh * V_AUG + MLA_V:(hh + 1) * V_AUG, :] = ones_rows

    qscale = MLA_QK ** -0.5 * LOG2_E
    r1 = MLA_HALF
    r2 = H_NOPE
    r3 = H_NOPE + MLA_NOPE

    def q_head(hh, blk):
        base = hh * HEAD_PAD
        ssq = jnp.sum(blk * blk, axis=0, keepdims=True)
        rs = lax.rsqrt(ssq * (1.0 / MLA_QK) + EPS) * qscale
        qn = blk * rs * qg_col_ref[...]
        x1 = qn[:r1, :]
        x2 = qn[r1:r2, :]
        qt_ref[0, base:base + r1, :] = (x1 * cos_t - x2 * sin_t).astype(BF16)
        qt_ref[0, base + r1:base + r2, :] = (x1 * sin_t + x2 * cos_t).astype(BF16)
        qt_ref[0, base + r2:base + r3, :] = qn[r2:r3, :].astype(BF16)
        qt_ref[0, base + r3:base + HEAD_PAD, :] = jnp.zeros((HEAD_PAD - r3, blk.shape[1]), BF16)

    lane = lax.broadcasted_iota(jnp.int32, (1, LANES), 1)
    is_pe = (lane >= MISC_PE) & (lane < MISC_PE + MLA_ROPE)
    kpe = jnp.where(is_pe, misc, 0.0)
    ssq_pe = jnp.sum(kpe * kpe, axis=-1, keepdims=True)
    xg = kpe * kg_rope
    rot = (xg * tab
           + pltpu.roll(xg, LANES - MLA_HALF, 1) * pltpu.roll(tab, LANES - ROT_NEG, 1)
           + pltpu.roll(xg, MLA_HALF, 1) * pltpu.roll(tab, LANES - ROT_POS, 1))

    def k_head(hh, kn_h):
        base = hh * HEAD_PAD
        ssq = jnp.sum(kn_h * kn_h, axis=-1, keepdims=True) + ssq_pe
        rs = lax.rsqrt(ssq * (1.0 / MLA_QK) + EPS)
        k_ref[0, :, base:base + HEAD_PAD] = ((kn_h * kg_nope + rot) * rs).astype(BF16)

    pieces = [(zqk_ref, 0, A_Q), (zqk_ref, PROJ_PIECE, A_K),
              (zv_ref, 0, A_V), (zv_ref, PROJ_PIECE, A_V + PROJ_PIECE)]
    group = MLA_HEADS // len(pieces)
    for i, (ref, lane0, c0) in enumerate(pieces):
        rows = slice(i * group * HEAD_PAD, (i + 1) * group * HEAD_PAD)
        ref[0, :, lane0:lane0 + PROJ_PIECE] = in_proj(w_a_ref, c0, c0 + PROJ_PIECE).astype(ref.dtype)
        qt = _nt(w_uqt_ref[rows, :], cq)
        kn = jnp.dot(ckv, w_k_ref[:, rows], preferred_element_type=F32)
        for n in range(group):
            q_head(i * group + n, qt[n * HEAD_PAD:(n + 1) * HEAD_PAD, :])
            k_head(i * group + n, kn[:, n * HEAD_PAD:(n + 1) * HEAD_PAD])

    _gla_block(zqk_ref, loga_ref, zv_ref, gate_ref, gout_ref, tri_ref, o_gla_ref, st_ref,
               n_chunks=x_ref.shape[1] // CHUNK)


def _split3(v):
    hi = v.astype(BF16)
    r1 = v - hi.astype(F32)
    mid = r1.astype(BF16)
    lo = (r1 - mid.astype(F32)).astype(BF16)
    return hi, mid, lo


def _gla_block(zqk_ref, loga_ref, zv_ref, gate_ref, gout_ref, tri_ref, o_ref, st_ref, *,
               n_chunks):
    @pl.when(pl.program_id(1) == 0)
    def _():
        st_ref[...] = jnp.zeros(st_ref.shape, F32)

    sr = lax.broadcasted_iota(jnp.int32, (GLA_HEADS * CHUNK, CHUNK), 0)
    sc_ = lax.broadcasted_iota(jnp.int32, (GLA_HEADS * CHUNK, CHUNK), 1)
    causal4 = (sr % CHUNK) >= sc_
    qr = lax.broadcasted_iota(jnp.int32, (GLA_HEADS * CHUNK, GLA_QK_W), 0)
    qc = lax.broadcasted_iota(jnp.int32, (GLA_HEADS * CHUNK, GLA_QK_W), 1)
    head_sel = (qr // CHUNK) == (qc // GLA_DK)
    lane_head = lax.broadcasted_iota(jnp.int32, (GLA_DV, GLA_QK_W), 1) // GLA_DK
    gout = gout_ref[...]
    chunks = [slice(ci * CHUNK, (ci + 1) * CHUNK) for ci in range(n_chunks)]

    la = loga_ref[0]
    parts = [p for r in chunks for p in _split3(la[r])]
    tri_out = jnp.dot(tri_ref[...], jnp.concatenate(parts, axis=1), preferred_element_type=F32)
    w = GLA_QK_W
    cum = [tri_out[:, (3 * ci) * w:(3 * ci + 1) * w] + tri_out[:, (3 * ci + 1) * w:(3 * ci + 2) * w]
           + tri_out[:, (3 * ci + 2) * w:(3 * ci + 3) * w] for ci in range(n_chunks)]
    cl = [c[CHUNK - 1:CHUNK, :] for c in cum]

    def prep(ci):
        r = chunks[ci]
        q = zqk_ref[0, r, 0:GLA_QK_W]
        k = zqk_ref[0, r, GLA_QK_W:2 * GLA_QK_W]
        qd = (q * (GLA_DK ** -0.5) * jnp.exp(cum[ci])).astype(BF16)
        k_inv = (k * jnp.exp(-cum[ci])).astype(BF16)
        k_end = (k * jnp.exp(cl[ci] - cum[ci])).astype(BF16)
        q_stack = jnp.where(head_sel, jnp.concatenate([qd] * GLA_HEADS, axis=0), 0.0)
        return q_stack, k_inv, k_end

    def intra_scores(ci):
        return jnp.where(causal4, _nt(pre[ci][0], pre[ci][1]), 0.0).astype(BF16)

    def intra_out_and_kv(ci):
        v = zv_ref[0, chunks[ci], :]
        full = [jnp.dot(scores[ci][hh * CHUNK:(hh + 1) * CHUNK, :],
                        v[:, hh * GLA_DV:(hh + 1) * GLA_DV], preferred_element_type=F32)
                for hh in range(GLA_HEADS)]
        kv_t = _tn(v, pre[ci][2])
        kv = kv_t[(GLA_HEADS - 1) * GLA_DV:, :]
        for hh in range(GLA_HEADS - 2, -1, -1):
            kv = jnp.where(lane_head == hh, kv_t[hh * GLA_DV:(hh + 1) * GLA_DV, :], kv)
        return full, kv

    pre, scores, intra = {}, {}, {}
    for t in range(n_chunks + 2):
        if t < n_chunks:
            pre[t] = prep(t)
        if 0 <= t - 1 < n_chunks:
            scores[t - 1] = intra_scores(t - 1)
        if 0 <= t - 2 < n_chunks:
            intra[t - 2] = intra_out_and_kv(t - 2)

    st = st_ref[...]
    st_in = []
    for ci in range(n_chunks):
        st_in.append(st.astype(BF16))
        st = st * jnp.exp(cl[ci]) + intra[ci][1]
    st_ref[...] = st

    def finish(ci, inter):
        r = chunks[ci]
        for hh in range(GLA_HEADS):
            cols = slice(hh * GLA_DV, (hh + 1) * GLA_DV)
            o_h = inter[hh * CHUNK:(hh + 1) * CHUNK, :] + intra[ci][0][hh]
            g_h = gate_ref[0, r, cols].astype(F32)
            o_ref[0, r, cols] = (_rms(o_h) * gout * g_h).astype(BF16)

    inter = {}
    for t in range(n_chunks + 1):
        if t < n_chunks:
            inter[t] = _nt(pre[t][0], st_in[t])
        if t >= 1:
            finish(t - 1, inter[t - 1])


def _attn_kernel(qt_ref, k_ref, vt_ref, w_up_ref, w_dn_ref, w_out_ref, o_ref, w_up_bf_ref,
                 w_dn_bf_ref, w_out_bf_ref, s_ref, acc_ref, m_ref, bm_ref, *, tq, tk):
    qi = pl.program_id(2)
    ndiag = tq // tk

    m_ref[...] = jnp.full(m_ref.shape, NEG, F32)
    acc_ref[...] = jnp.zeros(acc_ref.shape, F32)

    kr = lax.broadcasted_iota(jnp.int32, (tk, tk), 0)
    qc = lax.broadcasted_iota(jnp.int32, (tk, tk), 1)
    square_mask = (kr // CHUNK) <= (qc // CHUNK)

    def qk(j, slot, c0=0, c1=tq):
        start = pl.multiple_of(j * tk, tk)
        s = jnp.dot(k_ref[0, pl.ds(start, tk), :], qt_ref[0, :, c0:c1],
                    preferred_element_type=F32)
        s_ref[slot, :, c0:c1] = s
        bm_ref[slot, :, c0:c1] = jnp.max(s, axis=0, keepdims=True)

    def mask_square(slot, c0):
        cols = slice(c0, c0 + tk)
        s = jnp.where(square_mask, s_ref[slot, :, cols], NEG)
        s_ref[slot, :, cols] = s
        bm_ref[slot, :, cols] = jnp.max(s, axis=0, keepdims=True)

    def softmax_pv(j, slots, c0=0, c1=tq):
        m_old = m_ref[:, c0:c1]
        m_new = m_old
        for slot in slots:
            m_new = jnp.maximum(m_new, bm_ref[slot, :, c0:c1])
        m_ref[:, c0:c1] = m_new
        p = jnp.concatenate([jnp.exp2(s_ref[slot, :, c0:c1] - m_new).astype(BF16)
                             for slot in slots], axis=0)
        start = pl.multiple_of(j * tk, tk)
        pv = jnp.dot(vt_ref[0, :, pl.ds(start, len(slots) * tk)], p,
                     preferred_element_type=F32)
        acc_ref[:, c0:c1] = jnp.exp2(m_old - m_new) * acc_ref[:, c0:c1] + pv

    ngroups = tq // QUERY_GROUP
    per_iter = 2 * ngroups

    def cols(g):
        return g * QUERY_GROUP, (g + 1) * QUERY_GROUP

    def qk_item(j, ps, g, both=True):
        qk(j, 2 * ps, *cols(g))
        if both:
            qk(j + 1, 2 * ps + 1, *cols(g))

    for u in range(QK_LOOKAHEAD):
        qk_item(0, 0, u)

    def body(i, carry):
        for u in range(per_iter):
            v = u + QK_LOOKAHEAD
            if v < per_iter:
                qk_item(4 * i + 2 * (v // ngroups), v // ngroups, v % ngroups)
            else:
                qk_item(4 * i + 4, 0, v - per_iter)
            ps, g = divmod(u, ngroups)
            softmax_pv(4 * i + 2 * ps, (2 * ps, 2 * ps + 1), *cols(g))
        return carry

    lax.fori_loop(0, qi * (ndiag // 4), body, 0)

    nfull = qi * ndiag
    items = [(p, g) for p in range(ndiag // 2) for g in range(2 * p, ngroups)]
    for t, (p, g) in enumerate(items):
        if t == len(items) // 2:
            w_up_bf_ref[...] = w_up_ref[...].astype(BF16)
            w_dn_bf_ref[...] = w_dn_ref[...].astype(BF16)
            w_out_bf_ref[...] = w_out_ref[...].astype(BF16)
        if t + QK_LOOKAHEAD < len(items):
            p2, g2 = items[t + QK_LOOKAHEAD]
            qk_item(nfull + 2 * p2, p2 % 2, g2, both=(g2 != 2 * p2))
        d = 2 * p
        slots = (2 * (p % 2), 2 * (p % 2) + 1)
        if g == d:
            mask_square(slots[0], d * tk)
            softmax_pv(nfull + d, slots[:1], *cols(g))
        else:
            if g == d + 1:
                mask_square(slots[1], (d + 1) * tk)
            softmax_pv(nfull + d, slots, *cols(g))

    acc = acc_ref[...]
    o_ref[0] = (acc[:MLA_V, :] / acc[MLA_V:MLA_V + 1, :]).astype(BF16)


def _mlp_kernel(x_ref, og_ref, ot_ref, w_out_ref, g_mlp_ref, w_up_ref, w_dn_ref,
                y_ref, *, ff_block):
    mix = jnp.dot(og_ref[0], w_out_ref[:GLA_V_W, :], preferred_element_type=F32)
    mix = mix + _tn(ot_ref[0], w_out_ref[GLA_V_W:, :])
    x1 = x_ref[0] + mix
    h = (_rms(x1) * g_mlp_ref[...]).astype(BF16)
    acc = x1
    d_ff = w_up_ref.shape[1]
    for f in range(d_ff // ff_block):
        cols = slice(f * ff_block, (f + 1) * ff_block)
        u = jnp.dot(h, w_up_ref[:, cols], preferred_element_type=F32)
        a = jnp.square(jnp.maximum(u, 0.0)).astype(BF16)
        acc = acc + jnp.dot(a, w_dn_ref[cols, :], preferred_element_type=F32)
    y_ref[0] = acc


def _const_spec(shape, single_buffer=False):
    nd = len(shape)
    mode = pl.Buffered(1) if single_buffer else None
    return pl.BlockSpec(shape, lambda *_: (0,) * nd, pipeline_mode=mode)


def _pack_weights(attn_norm, w_in, w_gate_up, b_gate, q_a_norm, w_uq, kv_a_norm, w_ukv,
                  q_head_norm, k_head_norm):
    o_gate = 2 * GLA_QK_W + GLA_V_W
    o_rest = o_gate + GLA_GATE_RANK
    rest = w_in[:, o_rest:]
    assert A_W == o_gate and rest.shape[1] == B_MISC + MLA_ROPE and MISC_PE == 0
    gate_at = B_MISC + MISC_GATE
    w_b = (jnp.pad(rest, ((0, 0), (0, B_W - rest.shape[1])))
           + jnp.pad(w_in[:, o_gate:o_rest], ((0, 0), (gate_at, B_W - gate_at - GLA_GATE_RANK)))
           ).astype(BF16)

    w_gate_f = jnp.pad(w_gate_up.astype(F32),
                       ((MISC_GATE, LANES - MISC_GATE - GLA_GATE_RANK), (0, 0)))
    g_hi = w_gate_f.astype(BF16)
    g_lo = (w_gate_f - g_hi.astype(F32)).astype(BF16)
    w_gate_p = jnp.concatenate([g_hi, g_lo, g_hi], axis=0)

    w_uq_h = w_uq.reshape(MLA_Q_RANK, MLA_HEADS, MLA_QK)
    w_uq_h = jnp.concatenate([w_uq_h[:, :, MLA_NOPE:], w_uq_h[:, :, :MLA_NOPE]], axis=2)
    w_uq_h = jnp.pad(w_uq_h, ((0, 0), (0, 0), (0, HEAD_PAD - MLA_QK)))
    w_uqt = w_uq_h.reshape(MLA_Q_RANK, MLA_HEADS * HEAD_PAD).T.astype(BF16)

    nope_pad = (H_NOPE, HEAD_PAD - H_NOPE - MLA_NOPE)
    w_ukv_h = w_ukv.reshape(MLA_KV_RANK, MLA_HEADS, MLA_NOPE + MLA_V)
    w_k = jnp.pad(w_ukv_h[:, :, :MLA_NOPE], ((0, 0), (0, 0), nope_pad))
    w_k = w_k.reshape(MLA_KV_RANK, MLA_HEADS * HEAD_PAD).astype(BF16)
    w_vt = w_ukv_h[:, :, MLA_NOPE:].reshape(MLA_KV_RANK, MLA_HEADS * MLA_V).T.astype(BF16)

    qg = jnp.concatenate([q_head_norm[MLA_NOPE:], q_head_norm[:MLA_NOPE]])
    qg_col = jnp.pad(qg, (0, HEAD_PAD - MLA_QK)).reshape(HEAD_PAD, 1).astype(F32)
    zeros = lambda n: jnp.zeros((n,), F32)
    vec = jnp.concatenate([
        attn_norm, q_a_norm, kv_a_norm, b_gate,
        zeros(nope_pad[0]), k_head_norm[:MLA_NOPE], zeros(nope_pad[1]),
        zeros(MISC_PE), k_head_norm[MLA_NOPE:], zeros(LANES - MISC_PE - MLA_ROPE),
    ]).astype(F32).reshape(1, -1)
    return vec, w_b, w_gate_p, w_uqt, w_k, w_vt, qg_col


def _layer(x, posr, invf_col, rope_sel, attn_norm, w_in, w_gate_up, b_gate, gla_out_norm,
           q_a_norm, w_uq, kv_a_norm, w_ukv, q_head_norm, k_head_norm, w_out, mlp_norm, w_up,
           w_down):
    b_, s_, d_model = x.shape
    tm = min(TOKEN_BLOCK, s_)
    tq = min(ATTN_TQ, s_)
    tk = min(ATTN_TK, tq)
    assert s_ % tm == 0 and s_ % tq == 0 and tk % CHUNK == 0 and tm % CHUNK == 0
    assert tq % (4 * tk) == 0
    assert tk == QUERY_GROUP and QK_LOOKAHEAD <= tq // QUERY_GROUP
    d_ff = w_up.shape[1]
    ff_block = min(FF_BLOCK, d_ff)

    vec, w_b, w_gate_p, w_uqt, w_k, w_vt, qg_col = _pack_weights(
        attn_norm, w_in, w_gate_up, b_gate, q_a_norm, w_uq, kv_a_norm, w_ukv, q_head_norm,
        k_head_norm)
    row = lambda v: v.reshape(1, -1).astype(F32)

    tok = lambda w: pl.BlockSpec((1, tm, w), lambda b, i: (b, i, 0))
    tok_t = lambda hgt: pl.BlockSpec((1, hgt, tm), lambda b, i: (b, 0, i))
    consts = (w_b, w_gate_p, w_uqt, w_k, w_vt, qg_col, invf_col, rope_sel)
    w_a_spec = pl.BlockSpec((d_model, A_W), lambda b, i: (0, 0))
    tri = jnp.asarray(np.tril(np.ones((CHUNK, CHUNK), np.float32)), BF16)
    o_gla, qt, kk, vt = pl.pallas_call(
        _proj_kernel,
        grid=(b_, s_ // tm),
        in_specs=([tok(d_model), tok_t(1), _const_spec(vec.shape), w_a_spec]
                  + [_const_spec(c.shape) for c in consts]
                  + [_const_spec((1, GLA_DV)), _const_spec((CHUNK, CHUNK))]),
        out_specs=[tok(GLA_V_W), tok_t(MLA_HEADS * HEAD_PAD), tok(MLA_HEADS * HEAD_PAD),
                   tok_t(MLA_HEADS * V_AUG)],
        out_shape=[jax.ShapeDtypeStruct((b_, s_, GLA_V_W), BF16),
                   jax.ShapeDtypeStruct((b_, MLA_HEADS * HEAD_PAD, s_), BF16),
                   jax.ShapeDtypeStruct((b_, s_, MLA_HEADS * HEAD_PAD), BF16),
                   jax.ShapeDtypeStruct((b_, MLA_HEADS * V_AUG, s_), BF16)],
        scratch_shapes=[pltpu.VMEM((1, tm, 2 * GLA_QK_W), F32), pltpu.VMEM((1, tm, GLA_QK_W), F32),
                        pltpu.VMEM((1, tm, GLA_V_W), BF16), pltpu.VMEM((1, tm, GLA_V_W), BF16),
                        pltpu.VMEM((GLA_DV, GLA_QK_W), F32)],
        compiler_params=pltpu.CompilerParams(dimension_semantics=("parallel", "arbitrary"),
                                             vmem_limit_bytes=VMEM_LIMIT),
        name="proj_gla",
    )(x, posr, vec, w_in, *consts, row(gla_out_norm), tri)

    n_q = s_ // tq
    n_steps = b_ * MLA_HEADS * n_q
    assert d_model % (n_steps * BF16_SUBLANES) == 0 and d_ff % (n_steps * BF16_SUBLANES) == 0
    assert w_out.shape[0] % (n_steps * BF16_SUBLANES) == 0
    slab = lambda rows, w: pl.BlockSpec((rows // n_steps, w),
                                        lambda b, h, i: ((b * MLA_HEADS + h) * n_q + i, 0))
    o_t, w_up_bf, w_dn_bf, w_out_bf = pl.pallas_call(
        functools.partial(_attn_kernel, tq=tq, tk=tk),
        grid=(b_, MLA_HEADS, n_q),
        in_specs=[pl.BlockSpec((1, HEAD_PAD, tq), lambda b, h, i: (b, h, i)),
                  pl.BlockSpec((1, s_, HEAD_PAD), lambda b, h, i: (b, 0, h)),
                  pl.BlockSpec((1, V_AUG, s_), lambda b, h, i: (b, h, 0)),
                  slab(d_model, d_ff), slab(d_ff, d_model), slab(w_out.shape[0], d_model)],
        out_specs=[pl.BlockSpec((1, MLA_V, tq), lambda b, h, i: (b, h, i)),
                   slab(d_model, d_ff), slab(d_ff, d_model), slab(w_out.shape[0], d_model)],
        out_shape=[jax.ShapeDtypeStruct((b_, MLA_HEADS * MLA_V, s_), BF16),
                   jax.ShapeDtypeStruct((d_model, d_ff), BF16),
                   jax.ShapeDtypeStruct((d_ff, d_model), BF16),
                   jax.ShapeDtypeStruct(w_out.shape, BF16)],
        scratch_shapes=[pltpu.VMEM((4, tk, tq), F32), pltpu.VMEM((V_AUG, tq), F32),
                        pltpu.VMEM((1, tq), F32), pltpu.VMEM((4, 1, tq), F32)],
        compiler_params=pltpu.CompilerParams(
            dimension_semantics=("parallel", "parallel", "arbitrary"),
            vmem_limit_bytes=VMEM_LIMIT),
        name="attn",
    )(qt, kk, vt, w_up, w_down, w_out)

    tmm = min(MLP_BLOCK, s_)
    assert s_ % tmm == 0
    mtok = lambda w: pl.BlockSpec((1, tmm, w), lambda b, i: (b, i, 0))
    y = pl.pallas_call(
        functools.partial(_mlp_kernel, ff_block=ff_block),
        grid=(b_, s_ // tmm),
        in_specs=[mtok(d_model), mtok(GLA_V_W),
                  pl.BlockSpec((1, MLA_HEADS * MLA_V, tmm), lambda b, i: (b, 0, i)),
                  _const_spec(w_out.shape, True), _const_spec((1, d_model)),
                  _const_spec(w_up.shape, True), _const_spec(w_down.shape, True)],
        out_specs=mtok(d_model),
        out_shape=jax.ShapeDtypeStruct((b_, s_, d_model), x.dtype),
        compiler_params=pltpu.CompilerParams(dimension_semantics=("parallel", "parallel"),
                                             vmem_limit_bytes=VMEM_LIMIT),
        name="mlp",
    )(x, o_gla, o_t, w_out_bf, row(mlp_norm), w_up_bf, w_dn_bf)
    return y


def kernel(x, positions, attn_norm, w_in, w_gate_up, b_gate, gla_out_norm, q_a_norm, w_uq,
           kv_a_norm, w_ukv, q_head_norm, k_head_norm, w_out, mlp_norm, w_up, w_down):
    b_, s_, _ = x.shape
    posr = positions.reshape(b_, 1, s_)
    inv_freq = ROPE_BASE ** (-jnp.arange(0, MLA_ROPE, 2, dtype=F32) / MLA_ROPE)
    invf_col = inv_freq.reshape(MLA_HALF, 1)
    i = np.arange(MLA_HALF)
    sel = np.zeros((MLA_ROPE, LANES), np.float32)
    sel[i, MISC_PE + i] = 1.0
    sel[i, MISC_PE + MLA_HALF + i] = 1.0
    sel[MLA_HALF + i, ROT_NEG + MISC_PE + i] = -1.0
    sel[MLA_HALF + i, ROT_POS + MISC_PE + MLA_HALF + i] = 1.0
    rope_sel = jnp.asarray(np.tile(sel, (3, 1)), BF16)
    for l in range(attn_norm.shape[0]):
        x = _layer(x, posr, invf_col, rope_sel, attn_norm[l], w_in[l], w_gate_up[l],
                   b_gate[l], gla_out_norm[l], q_a_norm[l], w_uq[l], kv_a_norm[l], w_ukv[l],
                   q_head_norm[l], k_head_norm[l], w_out[l], mlp_norm[l], w_up[l], w_down[l])
    return x
```
